```python
import jax, jax.numpy as jnp
from jax import lax
import numpy as np

D_MODEL = 1024
BATCH = 4
SEQ = 8192
DEPTH = 2

HEAD_DIM = 64
Q_BLOCK = 128
ROPE_THETA = 10000.0
NORM_EPS = 1e-6
NEG_INF = -1e30
N_BRANCHES = 4

FOX_HEADS = 4
FOX_W = FOX_HEADS * HEAD_DIM
FOX_BIAS_CENTER = 3.0
LRU_WIDTH = 256
LRU_BLOCKS = 4
LRU_BW = LRU_WIDTH // LRU_BLOCKS
LRU_CONV = 4
LRU_C = 8.0
DIL_PAIRS = ((128, 1), (512, 4), (2048, 16))
DIL_GROUPS = len(DIL_PAIRS)
DIL_HEADS_PER_GROUP = 2
DIL_HEADS = DIL_GROUPS * DIL_HEADS_PER_GROUP
DIL_W = DIL_HEADS * HEAD_DIM
DIL_OUT_W = DIL_HEADS_PER_GROUP * HEAD_DIM
DIL_KEYS = DIL_PAIRS[0][0] // DIL_PAIRS[0][1] + 1
NSA_HEADS = 4
NSA_KV_HEADS = 1
NSA_W = NSA_HEADS * HEAD_DIM
NSA_KV_W = NSA_KV_HEADS * HEAD_DIM
NSA_CMP_BLOCK = 32
NSA_CMP_STRIDE = 16
NSA_CMP_HIDDEN = 256
NSA_SLC_BLOCK = 64
NSA_TOP_N = 16
NSA_WINDOW = 512
NSA_FORCE_SCORE = 1e6
PEER_HEADS = 8
PEER_N_KEYS = 128
PEER_N_EXPERTS = PEER_N_KEYS * PEER_N_KEYS
PEER_TOPK = 16
PEER_D_KEY = 256
PEER_BLOCK = 128

IN_SPLITS = (FOX_W, FOX_W, FOX_W, FOX_HEADS,
             LRU_WIDTH, LRU_WIDTH,
             DIL_W, DIL_W, DIL_W,
             NSA_W, NSA_KV_W, NSA_KV_W, NSA_KV_W, NSA_KV_W, NSA_KV_W, NSA_KV_W, NSA_HEADS * 3,
             N_BRANCHES * D_MODEL)
IN_WIDTH = sum(IN_SPLITS)

kernel_name = "hybrid_fox_lru_dilated_nsa_peer"


def rms_norm(x, g):
    xf = x.astype(jnp.float32)
    y = xf * lax.rsqrt(jnp.mean(xf * xf, axis=-1, keepdims=True) + NORM_EPS)
    return (y * g.astype(jnp.float32)).astype(x.dtype)


def rope(x, pos):
    half = x.shape[-1] // 2
    freqs = ROPE_THETA ** (-jnp.arange(half, dtype=jnp.float32) / half)
    ang = pos.astype(jnp.float32)[:, None] * freqs[None, :]
    cos = jnp.cos(ang)[None, :, None, :]
    sin = jnp.sin(ang)[None, :, None, :]
    xf = x.astype(jnp.float32)
    x1, x2 = xf[..., :half], xf[..., half:]
    return jnp.concatenate([x1 * cos - x2 * sin, x2 * cos + x1 * sin], axis=-1).astype(x.dtype)


def _unblock(y):
    y = jnp.moveaxis(y, 0, 1)
    return y.reshape((y.shape[0], y.shape[1] * y.shape[2]) + y.shape[3:])


def masked_softmax(logits, mask):
    logits = jnp.where(mask, logits, NEG_INF)
    m = jnp.max(logits, axis=-1, keepdims=True)
    e = jnp.where(mask, jnp.exp(logits - m), 0.0)
    return e / jnp.maximum(jnp.sum(e, axis=-1, keepdims=True), 1e-30)


def fox_attention(q, k, v, f_logit, b_f):
    B, S, H, dh = q.shape
    log_f = jax.nn.log_sigmoid(f_logit.astype(jnp.float32) + b_f.astype(jnp.float32))
    cum = jnp.transpose(jnp.cumsum(log_f, axis=1), (0, 2, 1))
    scale = dh ** -0.5
    k_pos = jnp.arange(S)

    def block(i):
        s0 = i * Q_BLOCK
        qb = lax.dynamic_slice_in_dim(q, s0, Q_BLOCK, axis=1)
        cq = lax.dynamic_slice_in_dim(cum, s0, Q_BLOCK, axis=2)
        logits = jnp.einsum('bqhd,bkhd->bhqk', qb, k).astype(jnp.float32) * scale
        logits = logits + cq[..., None] - cum[:, :, None, :]
        q_pos = s0 + jnp.arange(Q_BLOCK)
        logits = jnp.where(k_pos[None, :] <= q_pos[:, None], logits, NEG_INF)
        p = jax.nn.softmax(logits, axis=-1)
        return jnp.einsum('bhqk,bkhd->bqhd', p.astype(v.dtype), v)

    return _unblock(lax.map(block, jnp.arange(S // Q_BLOCK)))


def rg_lru_branch(x_in, gate, conv_w, conv_b, w_a, b_a, w_x, b_x, lam):
    B, S, C = x_in.shape
    xc = lax.conv_general_dilated(x_in, conv_w[:, None, :], window_strides=(1,),
                                  padding=[(LRU_CONV - 1, 0)],
                                  dimension_numbers=('NWC', 'WIO', 'NWC'),
                                  feature_group_count=C) + conv_b
    xr = xc.reshape(B, S, LRU_BLOCKS, C // LRU_BLOCKS)
    r = jax.nn.sigmoid(jnp.einsum('bsnc,ncd->bsnd', xr, w_a).reshape(B, S, C) + b_a)
    i_g = jax.nn.sigmoid(jnp.einsum('bsnc,ncd->bsnd', xr, w_x).reshape(B, S, C) + b_x)
    log_a = -LRU_C * r.astype(jnp.float32) * jax.nn.softplus(-lam.astype(jnp.float32))
    a = jnp.exp(log_a)
    u = jnp.sqrt(-jnp.expm1(2.0 * log_a)) * (i_g * xc).astype(jnp.float32)

    def combine(left, right):
        a_l, b_l = left
        a_r, b_r = right
        return a_l * a_r, a_r * b_l + b_r

    _, h = lax.associative_scan(combine, (a, u), axis=1)
    return h.astype(x_in.dtype) * jax.nn.gelu(gate)


def dilated_attention(q, k, v):
    B, S, H, dh = q.shape
    G, Hg = DIL_GROUPS, DIL_HEADS_PER_GROUP
    qg = q.reshape(B, S, G, Hg, dh)
    kg = k.reshape(B, S, G, Hg, dh)
    vg = v.reshape(B, S, G, Hg, dh)
    dil = jnp.array([d for _, d in DIL_PAIRS], dtype=jnp.int32)
    steps = jnp.arange(DIL_KEYS)
    g_idx = jnp.arange(G)[:, None, None]
    scale = dh ** -0.5

    def block(i):
        s0 = i * Q_BLOCK
        q_pos = s0 + jnp.arange(Q_BLOCK)
        k_pos = q_pos[None, :, None] - dil[:, None, None] * steps[None, None, :]
        valid = k_pos >= 0
        k_idx = jnp.maximum(k_pos, 0)
        kb = kg[:, k_idx, g_idx]
        vb = vg[:, k_idx, g_idx]
        qb = lax.dynamic_slice_in_dim(qg, s0, Q_BLOCK, axis=1)
        logits = jnp.einsum('bqghd,bgqkhd->bgqhk', qb, kb).astype(jnp.float32) * scale
        logits = jnp.where(valid[None, :, :, None, :], logits, NEG_INF)
        m = jnp.max(logits, axis=-1, keepdims=True)
        e = jnp.exp(logits - m)
        l = jnp.sum(e, axis=-1)
        o = jnp.einsum('bgqhk,bgqkhd->bgqhd', e, vb.astype(jnp.float32)) / l[..., None]
        lse = m[..., 0] + jnp.log(l)
        w = jax.nn.softmax(lse, axis=1)
        return jnp.einsum('bgqh,bgqhd->bqhd', w, o).astype(v.dtype)

    return _unblock(lax.map(block, jnp.arange(S // Q_BLOCK)))


def nsa_attention(q, k_cmp, v_cmp, k_slc, v_slc, k_win, v_win, gate_logits,
                  pe_k, pe_v, w1_k, w2_k, w1_v, w2_v):
    B, S, H, dh = q.shape
    Hkv = k_cmp.shape[2]
    G = H // Hkv
    pos = jnp.arange(S)
    q_rot = rope(q, pos)
    k_slc = rope(k_slc, pos)
    k_win = rope(k_win, pos)
    gates = jax.nn.sigmoid(gate_logits).reshape(B, S, H, 3)
    scale = dh ** -0.5

    n_cmp = (S - NSA_CMP_BLOCK) // NSA_CMP_STRIDE + 1
    cmp_idx = jnp.arange(n_cmp)[:, None] * NSA_CMP_STRIDE + jnp.arange(NSA_CMP_BLOCK)[None, :]

    def compress(kv, pe, w1, w2):
        blocks = kv[:, cmp_idx] + pe[None, None, :, None, :]
        flat = jnp.moveaxis(blocks, 3, 2).reshape(B, n_cmp, Hkv, NSA_CMP_BLOCK * dh)
        return jax.nn.gelu(flat @ w1) @ w2

    kc = compress(k_cmp, pe_k, w1_k, w2_k)
    vc = compress(v_cmp, pe_v, w1_v, w2_v)
    cmp_start = jnp.arange(n_cmp) * NSA_CMP_STRIDE
    cmp_end = cmp_start + NSA_CMP_BLOCK - 1

    n_slc = S // NSA_SLC_BLOCK
    top_n = min(NSA_TOP_N, n_slc)
    slc_start = jnp.arange(n_slc) * NSA_SLC_BLOCK
    overlap = ((cmp_start[:, None] < slc_start[None, :] + NSA_SLC_BLOCK)
               & (cmp_start[:, None] + NSA_CMP_BLOCK > slc_start[None, :])).astype(jnp.float32)
    ks_blocks = jnp.transpose(k_slc.reshape(B, n_slc, NSA_SLC_BLOCK, Hkv, dh), (0, 3, 1, 2, 4))
    vs_blocks = jnp.transpose(v_slc.reshape(B, n_slc, NSA_SLC_BLOCK, Hkv, dh), (0, 3, 1, 2, 4))
    pad = ((0, 0), (NSA_WINDOW, 0), (0, 0), (0, 0))
    k_win_pad = jnp.pad(k_win, pad)
    v_win_pad = jnp.pad(v_win, pad)
    b_idx = jnp.arange(B)[:, None, None, None]
    h_idx = jnp.arange(Hkv)[None, :, None, None]
    blk_ids = jnp.arange(n_slc)

    def block(i):
        s0 = i * Q_BLOCK
        q_pos = s0 + jnp.arange(Q_BLOCK)
        qb = lax.dynamic_slice_in_dim(q, s0, Q_BLOCK, axis=1).reshape(B, Q_BLOCK, Hkv, G, dh)
        qrb = lax.dynamic_slice_in_dim(q_rot, s0, Q_BLOCK, axis=1).reshape(B, Q_BLOCK, Hkv, G, dh)
        lc = jnp.einsum('bqkgd,bckd->bkgqc', qb, kc).astype(jnp.float32) * scale
        p_c = masked_softmax(lc, (cmp_end[None, :] <= q_pos[:, None])[None, None, None])
        o_c = jnp.einsum('bkgqc,bckd->bqkgd', p_c, vc.astype(jnp.float32))
        imp = jnp.einsum('bkgqc,cn->bkqn', p_c, overlap)
        cur = q_pos // NSA_SLC_BLOCK
        forced = ((blk_ids[None, :] == 0) | (blk_ids[None, :] == cur[:, None])
                  | (blk_ids[None, :] == cur[:, None] - 1))
        imp = jnp.where(forced, NSA_FORCE_SCORE, imp)
        imp = jnp.where(blk_ids[None, :] <= cur[:, None], imp, NEG_INF)
        top_val, top_idx = lax.top_k(imp, top_n)
        sel_ok = top_val > 0.5 * NEG_INF
        ksel = ks_blocks[b_idx, h_idx, top_idx]
        vsel = vs_blocks[b_idx, h_idx, top_idx]
        sel_pos = top_idx[..., None] * NSA_SLC_BLOCK + jnp.arange(NSA_SLC_BLOCK)
        sel_mask = sel_ok[..., None] & (sel_pos <= q_pos[None, None, :, None, None])
        n_sel_keys = top_n * NSA_SLC_BLOCK
        ls = jnp.einsum('bqkgd,bkqnld->bkgqnl', qrb, ksel).astype(jnp.float32) * scale
        ls = ls.reshape(B, Hkv, G, Q_BLOCK, n_sel_keys)
        p_s = masked_softmax(ls, sel_mask.reshape(B, Hkv, 1, Q_BLOCK, n_sel_keys))
        o_s = jnp.einsum('bkgqm,bkqmd->bqkgd', p_s,
                         vsel.reshape(B, Hkv, Q_BLOCK, n_sel_keys, dh).astype(jnp.float32))
        kw = lax.dynamic_slice_in_dim(k_win_pad, s0, NSA_WINDOW + Q_BLOCK, axis=1)
        vw = lax.dynamic_slice_in_dim(v_win_pad, s0, NSA_WINDOW + Q_BLOCK, axis=1)
        w_pos = s0 - NSA_WINDOW + jnp.arange(NSA_WINDOW + Q_BLOCK)
        dist = q_pos[:, None] - w_pos[None, :]
        w_mask = (dist >= 0) & (dist < NSA_WINDOW) & (w_pos[None, :] >= 0)
        lw = jnp.einsum('bqkgd,bwkd->bkgqw', qrb, kw).astype(jnp.float32) * scale
        p_w = masked_softmax(lw, w_mask)
        o_w = jnp.einsum('bkgqw,bwkd->bqkgd', p_w, vw.astype(jnp.float32))
        gb = lax.dynamic_slice_in_dim(gates, s0, Q_BLOCK, axis=1).reshape(B, Q_BLOCK, Hkv, G, 3)
        gb = gb.astype(jnp.float32)
        out = gb[..., 0:1] * o_c + gb[..., 1:2] * o_s + gb[..., 2:3] * o_w
        return out.reshape(B, Q_BLOCK, H, dh).astype(q.dtype)

    return _unblock(lax.map(block, jnp.arange(S // Q_BLOCK)))


def peer_ffn(x, w_q, sub_keys, u, v):
    B, S, D = x.shape
    xt = x.reshape(B * S // PEER_BLOCK, PEER_BLOCK, D)
    k = PEER_TOPK

    def block(xb):
        q = (xb @ w_q).reshape(PEER_BLOCK, PEER_HEADS, 2, PEER_D_KEY // 2)
        s = jnp.einsum('thpd,pnd->thpn', q, sub_keys).astype(jnp.float32)
        v_half, i_half = lax.top_k(s, k)
        cand = v_half[:, :, 0, :, None] + v_half[:, :, 1, None, :]
        cand_idx = i_half[:, :, 0, :, None] * PEER_N_KEYS + i_half[:, :, 1, None, :]
        top_s, pos = lax.top_k(cand.reshape(PEER_BLOCK, PEER_HEADS, k * k), k)
        expert = jnp.take_along_axis(cand_idx.reshape(PEER_BLOCK, PEER_HEADS, k * k), pos, axis=-1)
        g = jax.nn.softmax(top_s, axis=-1)
        act = jax.nn.gelu(jnp.einsum('td,thkd->thk', xb, u[expert]))
        return jnp.einsum('thk,thkd->td', (g * act.astype(jnp.float32)).astype(x.dtype), v[expert])

    return lax.map(block, xt).reshape(B, S, D)


def setup_inputs(seed: int = 0) -> dict:
    key = jax.random.key(seed)
    keys = iter(jax.random.split(key, 40))
    L, D = DEPTH, D_MODEL
    f32 = jnp.float32

    def nrm(shape, scale):
        return jax.random.normal(next(keys), shape, f32) * scale

    a_c = jax.random.uniform(next(keys), (L, LRU_WIDTH), f32, 0.9, 0.999)
    a0 = a_c ** (1.0 / LRU_C)
    flat_cmp = NSA_CMP_BLOCK * HEAD_DIM
    return {
        "x": nrm((BATCH, SEQ, D), 1.0),
        "norm1_g": 1.0 + nrm((L, D), 0.02),
        "w_in": nrm((L, D, IN_WIDTH), D ** -0.5),
        "fox_b_f": FOX_BIAS_CENTER + nrm((L, FOX_HEADS), 0.5),
        "lru_conv_w": nrm((L, LRU_CONV, LRU_WIDTH), LRU_CONV ** -0.5),
        "lru_conv_b": nrm((L, LRU_WIDTH), 0.01),
        "lru_w_a": nrm((L, LRU_BLOCKS, LRU_BW, LRU_BW), LRU_BW ** -0.5),
        "lru_b_a": nrm((L, LRU_WIDTH), 0.01),
        "lru_w_x": nrm((L, LRU_BLOCKS, LRU_BW, LRU_BW), LRU_BW ** -0.5),
        "lru_b_x": nrm((L, LRU_WIDTH), 0.01),
        "lru_lambda": jnp.log(a0) - jnp.log1p(-a0),
        "nsa_pe_k": nrm((L, NSA_CMP_BLOCK, HEAD_DIM), 0.02),
        "nsa_pe_v": nrm((L, NSA_CMP_BLOCK, HEAD_DIM), 0.02),
        "nsa_w1_k": nrm((L, flat_cmp, NSA_CMP_HIDDEN), flat_cmp ** -0.5),
        "nsa_w2_k": nrm((L, NSA_CMP_HIDDEN, HEAD_DIM), NSA_CMP_HIDDEN ** -0.5),
        "nsa_w1_v": nrm((L, flat_cmp, NSA_CMP_HIDDEN), flat_cmp ** -0.5),
        "nsa_w2_v": nrm((L, NSA_CMP_HIDDEN, HEAD_DIM), NSA_CMP_HIDDEN ** -0.5),
        "w_br_fox": nrm((L, FOX_W, D), FOX_W ** -0.5),
        "w_br_lru": nrm((L, LRU_WIDTH, D), LRU_WIDTH ** -0.5),
        "w_br_dil": nrm((L, DIL_OUT_W, D), DIL_OUT_W ** -0.5),
        "w_br_nsa": nrm((L, NSA_W, D), NSA_W ** -0.5),
        "w_out": nrm((L, D, D), D ** -0.5),
        "norm2_g": 1.0 + nrm((L, D), 0.02),
        "peer_w_q": nrm((L, D, PEER_HEADS * PEER_D_KEY), D ** -0.5),
        "peer_sub_keys": nrm((L, 2, PEER_N_KEYS, PEER_D_KEY // 2), (PEER_D_KEY // 2) ** -0.5),
        "peer_u": nrm((L, PEER_N_EXPERTS, D), D ** -0.5),
        "peer_v": nrm((L, PEER_N_EXPERTS, D), PEER_HEADS ** -0.5),
        "final_g": 1.0 + nrm((D,), 0.02),
    }


def reference(x, norm1_g, w_in, fox_b_f, lru_conv_w, lru_conv_b, lru_w_a, lru_b_a,
              lru_w_x, lru_b_x, lru_lambda, nsa_pe_k, nsa_pe_v, nsa_w1_k, nsa_w2_k,
              nsa_w1_v, nsa_w2_v, w_br_fox, w_br_lru, w_br_dil, w_br_nsa, w_out,
              norm2_g, peer_w_q, peer_sub_keys, peer_u, peer_v, final_g):
    B, S, D = x.shape
    pos = jnp.arange(S)
    split_points = np.cumsum(IN_SPLITS)[:-1].tolist()

    def heads(t, n):
        return t.reshape(B, S, n, HEAD_DIM)

    for l in range(DEPTH):
        h = rms_norm(x, norm1_g[l])
        proj = h @ w_in[l]
        (fq, fk, fv, ff, lx, lg, dq, dk, dv, nq, nkc, nvc, nks, nvs, nkw, nvw, ng, mg) = \
            jnp.split(proj, split_points, axis=-1)

        o_fox = fox_attention(heads(fq, FOX_HEADS), heads(fk, FOX_HEADS), heads(fv, FOX_HEADS),
                              ff, fox_b_f[l]).reshape(B, S, FOX_W)
        o_lru = rg_lru_branch(lx, lg, lru_conv_w[l], lru_conv_b[l], lru_w_a[l], lru_b_a[l],
                              lru_w_x[l], lru_b_x[l], lru_lambda[l])
        o_dil = dilated_attention(rope(heads(dq, DIL_HEADS), pos), rope(heads(dk, DIL_HEADS), pos),
                                  heads(dv, DIL_HEADS)).reshape(B, S, DIL_OUT_W)
        o_nsa = nsa_attention(heads(nq, NSA_HEADS), heads(nkc, NSA_KV_HEADS), heads(nvc, NSA_KV_HEADS),
                              heads(nks, NSA_KV_HEADS), heads(nvs, NSA_KV_HEADS),
                              heads(nkw, NSA_KV_HEADS), heads(nvw, NSA_KV_HEADS), ng,
                              nsa_pe_k[l], nsa_pe_v[l], nsa_w1_k[l], nsa_w2_k[l],
                              nsa_w1_v[l], nsa_w2_v[l]).reshape(B, S, NSA_W)

        gates = jax.nn.sigmoid(mg).reshape(B, S, N_BRANCHES, D)
        merged = (gates[:, :, 0] * (o_fox @ w_br_fox[l])
                  + gates[:, :, 1] * (o_lru @ w_br_lru[l])
                  + gates[:, :, 2] * (o_dil @ w_br_dil[l])
                  + gates[:, :, 3] * (o_nsa @ w_br_nsa[l]))
        x = x + merged @ w_out[l]

        h2 = rms_norm(x, norm2_g[l])
        x = x + peer_ffn(h2, peer_w_q[l], peer_sub_keys[l], peer_u[l], peer_v[l])

    return rms_norm(x, final_g)
```

```python
import functools

import jax
import jax.numpy as jnp
import numpy as np
from jax import lax
from jax.experimental import pallas as pl
from jax.experimental.pallas import tpu as pltpu

D_MODEL = 1024
DEPTH = 2
HEAD_DIM = 64
Q_BLOCK = 128
ROPE_THETA = 10000.0
NORM_EPS = 1e-6
NEG_INF = -1e30
N_BRANCHES = 4
FOX_HEADS = 4
FOX_W = FOX_HEADS * HEAD_DIM
LRU_WIDTH = 256
LRU_BLOCKS = 4
LRU_CONV = 4
LRU_C = 8.0
DIL_PAIRS = ((128, 1), (512, 4), (2048, 16))
DIL_GROUPS = len(DIL_PAIRS)
DIL_HEADS_PER_GROUP = 2
DIL_HEADS = DIL_GROUPS * DIL_HEADS_PER_GROUP
DIL_W = DIL_HEADS * HEAD_DIM
DIL_OUT_W = DIL_HEADS_PER_GROUP * HEAD_DIM
DIL_KEYS = DIL_PAIRS[0][0] // DIL_PAIRS[0][1] + 1
NSA_HEADS = 4
NSA_KV_HEADS = 1
NSA_W = NSA_HEADS * HEAD_DIM
NSA_KV_W = NSA_KV_HEADS * HEAD_DIM
NSA_CMP_BLOCK = 32
NSA_CMP_STRIDE = 16
NSA_SLC_BLOCK = 64
NSA_TOP_N = 16
NSA_WINDOW = 512
NSA_FORCE_SCORE = 1e6
PEER_HEADS = 8
PEER_N_KEYS = 128
PEER_TOPK = 16
PEER_D_KEY = 256
PEER_BLOCK = 128

IN_SPLITS = (FOX_W, FOX_W, FOX_W, FOX_HEADS,
             LRU_WIDTH, LRU_WIDTH,
             DIL_W, DIL_W, DIL_W,
             NSA_W, NSA_KV_W, NSA_KV_W, NSA_KV_W, NSA_KV_W, NSA_KV_W, NSA_KV_W, NSA_HEADS * 3,
             N_BRANCHES * D_MODEL)


def _rms_norm_kernel(x_ref, g_ref, o_ref):
    x = x_ref[...]
    ms = jnp.mean(x * x, axis=-1, keepdims=True)
    o_ref[...] = x * lax.rsqrt(ms + NORM_EPS) * g_ref[...]


def rms_norm_pallas(x, g, rows=512):
    t, d = x.shape
    return pl.pallas_call(
        _rms_norm_kernel,
        grid=(t // rows,),
        in_specs=[pl.BlockSpec((rows, d), lambda i: (i, 0)),
                  pl.BlockSpec((1, d), lambda i: (0, 0))],
        out_specs=pl.BlockSpec((rows, d), lambda i: (i, 0)),
        out_shape=jax.ShapeDtypeStruct((t, d), x.dtype),
        name="rms_norm",
    )(x, g.reshape(1, d))


def rms_norm(x, g):
    xf = x.astype(jnp.float32)
    y = xf * lax.rsqrt(jnp.mean(xf * xf, axis=-1, keepdims=True) + NORM_EPS)
    return (y * g.astype(jnp.float32)).astype(x.dtype)


def rope(x, pos):
    half = x.shape[-1] // 2
    freqs = ROPE_THETA ** (-jnp.arange(half, dtype=jnp.float32) / half)
    ang = pos.astype(jnp.float32)[:, None] * freqs[None, :]
    cos = jnp.cos(ang)[None, :, None, :]
    sin = jnp.sin(ang)[None, :, None, :]
    xf = x.astype(jnp.float32)
    x1, x2 = xf[..., :half], xf[..., half:]
    return jnp.concatenate([x1 * cos - x2 * sin, x2 * cos + x1 * sin], axis=-1).astype(x.dtype)


def _unblock(y):
    y = jnp.moveaxis(y, 0, 1)
    return y.reshape((y.shape[0], y.shape[1] * y.shape[2]) + y.shape[3:])


def masked_softmax(logits, mask):
    logits = jnp.where(mask, logits, NEG_INF)
    m = jnp.max(logits, axis=-1, keepdims=True)
    e = jnp.where(mask, jnp.exp(logits - m), 0.0)
    return e / jnp.maximum(jnp.sum(e, axis=-1, keepdims=True), 1e-30)


def fox_attention(q, k, v, f_logit, b_f):
    B, S, H, dh = q.shape
    log_f = jax.nn.log_sigmoid(f_logit.astype(jnp.float32) + b_f.astype(jnp.float32))
    cum = jnp.transpose(jnp.cumsum(log_f, axis=1), (0, 2, 1))
    scale = dh ** -0.5
    k_pos = jnp.arange(S)

    def block(i):
        s0 = i * Q_BLOCK
        qb = lax.dynamic_slice_in_dim(q, s0, Q_BLOCK, axis=1)
        cq = lax.dynamic_slice_in_dim(cum, s0, Q_BLOCK, axis=2)
        logits = jnp.einsum('bqhd,bkhd->bhqk', qb, k).astype(jnp.float32) * scale
        logits = logits + cq[..., None] - cum[:, :, None, :]
        q_pos = s0 + jnp.arange(Q_BLOCK)
        logits = jnp.where(k_pos[None, :] <= q_pos[:, None], logits, NEG_INF)
        p = jax.nn.softmax(logits, axis=-1)
        return jnp.einsum('bhqk,bkhd->bqhd', p.astype(v.dtype), v)

    return _unblock(lax.map(block, jnp.arange(S // Q_BLOCK)))


def rg_lru_branch(x_in, gate, conv_w, conv_b, w_a, b_a, w_x, b_x, lam):
    B, S, C = x_in.shape
    xc = lax.conv_general_dilated(x_in, conv_w[:, None, :], window_strides=(1,),
                                  padding=[(LRU_CONV - 1, 0)],
                                  dimension_numbers=('NWC', 'WIO', 'NWC'),
                                  feature_group_count=C) + conv_b
    xr = xc.reshape(B, S, LRU_BLOCKS, C // LRU_BLOCKS)
    r = jax.nn.sigmoid(jnp.einsum('bsnc,ncd->bsnd', xr, w_a).reshape(B, S, C) + b_a)
    i_g = jax.nn.sigmoid(jnp.einsum('bsnc,ncd->bsnd', xr, w_x).reshape(B, S, C) + b_x)
    log_a = -LRU_C * r.astype(jnp.float32) * jax.nn.softplus(-lam.astype(jnp.float32))
    a = jnp.exp(log_a)
    u = jnp.sqrt(-jnp.expm1(2.0 * log_a)) * (i_g * xc).astype(jnp.float32)

    def combine(left, right):
        a_l, b_l = left
        a_r, b_r = right
        return a_l * a_r, a_r * b_l + b_r

    _, h = lax.associative_scan(combine, (a, u), axis=1)
    return h.astype(x_in.dtype) * jax.nn.gelu(gate)


def dilated_attention(q, k, v):
    B, S, H, dh = q.shape
    G, Hg = DIL_GROUPS, DIL_HEADS_PER_GROUP
    qg = q.reshape(B, S, G, Hg, dh)
    kg = k.reshape(B, S, G, Hg, dh)
    vg = v.reshape(B, S, G, Hg, dh)
    dil = jnp.array([d for _, d in DIL_PAIRS], dtype=jnp.int32)
    steps = jnp.arange(DIL_KEYS)
    g_idx = jnp.arange(G)[:, None, None]
    scale = dh ** -0.5

    def block(i):
        s0 = i * Q_BLOCK
        q_pos = s0 + jnp.arange(Q_BLOCK)
        k_pos = q_pos[None, :, None] - dil[:, None, None] * steps[None, None, :]
        valid = k_pos >= 0
        k_idx = jnp.maximum(k_pos, 0)
        kb = kg[:, k_idx, g_idx]
        vb = vg[:, k_idx, g_idx]
        qb = lax.dynamic_slice_in_dim(qg, s0, Q_BLOCK, axis=1)
        logits = jnp.einsum('bqghd,bgqkhd->bgqhk', qb, kb).astype(jnp.float32) * scale
        logits = jnp.where(valid[None, :, :, None, :], logits, NEG_INF)
        m = jnp.max(logits, axis=-1, keepdims=True)
        e = jnp.exp(logits - m)
        l = jnp.sum(e, axis=-1)
        o = jnp.einsum('bgqhk,bgqkhd->bgqhd', e, vb.astype(jnp.float32)) / l[..., None]
        lse = m[..., 0] + jnp.log(l)
        w = jax.nn.softmax(lse, axis=1)
        return jnp.einsum('bgqh,bgqhd->bqhd', w, o).astype(v.dtype)

    return _unblock(lax.map(block, jnp.arange(S // Q_BLOCK)))


def nsa_attention(q, k_cmp, v_cmp, k_slc, v_slc, k_win, v_win, gate_logits,
                  pe_k, pe_v, w1_k, w2_k, w1_v, w2_v):
    B, S, H, dh = q.shape
    Hkv = k_cmp.shape[2]
    G = H // Hkv
    pos = jnp.arange(S)
    q_rot = rope(q, pos)
    k_slc = rope(k_slc, pos)
    k_win = rope(k_win, pos)
    gates = jax.nn.sigmoid(gate_logits).reshape(B, S, H, 3)
    scale = dh ** -0.5

    n_cmp = (S - NSA_CMP_BLOCK) // NSA_CMP_STRIDE + 1
    cmp_idx = jnp.arange(n_cmp)[:, None] * NSA_CMP_STRIDE + jnp.arange(NSA_CMP_BLOCK)[None, :]

    def compress(kv, pe, w1, w2):
        blocks = kv[:, cmp_idx] + pe[None, None, :, None, :]
        flat = jnp.moveaxis(blocks, 3, 2).reshape(B, n_cmp, Hkv, NSA_CMP_BLOCK * dh)
        return jax.nn.gelu(flat @ w1) @ w2

    kc = compress(k_cmp, pe_k, w1_k, w2_k)
    vc = compress(v_cmp, pe_v, w1_v, w2_v)
    cmp_start = jnp.arange(n_cmp) * NSA_CMP_STRIDE
    cmp_end = cmp_start + NSA_CMP_BLOCK - 1

    n_slc = S // NSA_SLC_BLOCK
    top_n = min(NSA_TOP_N, n_slc)
    slc_start = jnp.arange(n_slc) * NSA_SLC_BLOCK
    overlap = ((cmp_start[:, None] < slc_start[None, :] + NSA_SLC_BLOCK)
               & (cmp_start[:, None] + NSA_CMP_BLOCK > slc_start[None, :])).astype(jnp.float32)
    ks_blocks = jnp.transpose(k_slc.reshape(B, n_slc, NSA_SLC_BLOCK, Hkv, dh), (0, 3, 1, 2, 4))
    vs_blocks = jnp.transpose(v_slc.reshape(B, n_slc, NSA_SLC_BLOCK, Hkv, dh), (0, 3, 1, 2, 4))
    pad = ((0, 0), (NSA_WINDOW, 0), (0, 0), (0, 0))
    k_win_pad = jnp.pad(k_win, pad)
    v_win_pad = jnp.pad(v_win, pad)
    b_idx = jnp.arange(B)[:, None, None, None]
    h_idx = jnp.arange(Hkv)[None, :, None, None]
    blk_ids = jnp.arange(n_slc)

    def block(i):
        s0 = i * Q_BLOCK
        q_pos = s0 + jnp.arange(Q_BLOCK)
        qb = lax.dynamic_slice_in_dim(q, s0, Q_BLOCK, axis=1).reshape(B, Q_BLOCK, Hkv, G, dh)
        qrb = lax.dynamic_slice_in_dim(q_rot, s0, Q_BLOCK, axis=1).reshape(B, Q_BLOCK, Hkv, G, dh)
        lc = jnp.einsum('bqkgd,bckd->bkgqc', qb, kc).astype(jnp.float32) * scale
        p_c = masked_softmax(lc, (cmp_end[None, :] <= q_pos[:, None])[None, None, None])
        o_c = jnp.einsum('bkgqc,bckd->bqkgd', p_c, vc.astype(jnp.float32))
        imp = jnp.einsum('bkgqc,cn->bkqn', p_c, overlap)
        cur = q_pos // NSA_SLC_BLOCK
        forced = ((blk_ids[None, :] == 0) | (blk_ids[None, :] == cur[:, None])
                  | (blk_ids[None, :] == cur[:, None] - 1))
        imp = jnp.where(forced, NSA_FORCE_SCORE, imp)
        imp = jnp.where(blk_ids[None, :] <= cur[:, None], imp, NEG_INF)
        top_val, top_idx = lax.top_k(imp, top_n)
        sel_ok = top_val > 0.5 * NEG_INF
        ksel = ks_blocks[b_idx, h_idx, top_idx]
        vsel = vs_blocks[b_idx, h_idx, top_idx]
        sel_pos = top_idx[..., None] * NSA_SLC_BLOCK + jnp.arange(NSA_SLC_BLOCK)
        sel_mask = sel_ok[..., None] & (sel_pos <= q_pos[None, None, :, None, None])
        n_sel_keys = top_n * NSA_SLC_BLOCK
        ls = jnp.einsum('bqkgd,bkqnld->bkgqnl', qrb, ksel).astype(jnp.float32) * scale
        ls = ls.reshape(B, Hkv, G, Q_BLOCK, n_sel_keys)
        p_s = masked_softmax(ls, sel_mask.reshape(B, Hkv, 1, Q_BLOCK, n_sel_keys))
        o_s = jnp.einsum('bkgqm,bkqmd->bqkgd', p_s,
                         vsel.reshape(B, Hkv, Q_BLOCK, n_sel_keys, dh).astype(jnp.float32))
        kw = lax.dynamic_slice_in_dim(k_win_pad, s0, NSA_WINDOW + Q_BLOCK, axis=1)
        vw = lax.dynamic_slice_in_dim(v_win_pad, s0, NSA_WINDOW + Q_BLOCK, axis=1)
        w_pos = s0 - NSA_WINDOW + jnp.arange(NSA_WINDOW + Q_BLOCK)
        dist = q_pos[:, None] - w_pos[None, :]
        w_mask = (dist >= 0) & (dist < NSA_WINDOW) & (w_pos[None, :] >= 0)
        lw = jnp.einsum('bqkgd,bwkd->bkgqw', qrb, kw).astype(jnp.float32) * scale
        p_w = masked_softmax(lw, w_mask)
        o_w = jnp.einsum('bkgqw,bwkd->bqkgd', p_w, vw.astype(jnp.float32))
        gb = lax.dynamic_slice_in_dim(gates, s0, Q_BLOCK, axis=1).reshape(B, Q_BLOCK, Hkv, G, 3)
        gb = gb.astype(jnp.float32)
        out = gb[..., 0:1] * o_c + gb[..., 1:2] * o_s + gb[..., 2:3] * o_w
        return out.reshape(B, Q_BLOCK, H, dh).astype(q.dtype)

    return _unblock(lax.map(block, jnp.arange(S // Q_BLOCK)))


def peer_ffn(x, w_q, sub_keys, u, v):
    B, S, D = x.shape
    xt = x.reshape(B * S // PEER_BLOCK, PEER_BLOCK, D)
    k = PEER_TOPK

    def block(xb):
        q = (xb @ w_q).reshape(PEER_BLOCK, PEER_HEADS, 2, PEER_D_KEY // 2)
        s = jnp.einsum('thpd,pnd->thpn', q, sub_keys).astype(jnp.float32)
        v_half, i_half = lax.top_k(s, k)
        cand = v_half[:, :, 0, :, None] + v_half[:, :, 1, None, :]
        cand_idx = i_half[:, :, 0, :, None] * PEER_N_KEYS + i_half[:, :, 1, None, :]
        top_s, pos = lax.top_k(cand.reshape(PEER_BLOCK, PEER_HEADS, k * k), k)
        expert = jnp.take_along_axis(cand_idx.reshape(PEER_BLOCK, PEER_HEADS, k * k), pos, axis=-1)
        g = jax.nn.softmax(top_s, axis=-1)
        act = jax.nn.gelu(jnp.einsum('td,thkd->thk', xb, u[expert]))
        return jnp.einsum('thk,thkd->td', (g * act.astype(jnp.float32)).astype(x.dtype), v[expert])

    return lax.map(block, xt).reshape(B, S, D)


def kernel(x, norm1_g, w_in, fox_b_f, lru_conv_w, lru_conv_b, lru_w_a, lru_b_a, lru_w_x, lru_b_x, lru_lambda, nsa_pe_k, nsa_pe_v, nsa_w1_k, nsa_w2_k, nsa_w1_v, nsa_w2_v, w_br_fox, w_br_lru, w_br_dil, w_br_nsa, w_out, norm2_g, peer_w_q, peer_sub_keys, peer_u, peer_v, final_g):
    B, S, D = x.shape
    pos = jnp.arange(S)
    split_points = np.cumsum(IN_SPLITS)[:-1].tolist()

    def heads(t, n):
        return t.reshape(B, S, n, HEAD_DIM)

    for l in range(DEPTH):
        h = rms_norm(x, norm1_g[l])
        proj = h @ w_in[l]
        (fq, fk, fv, ff, lx, lg, dq, dk, dv, nq, nkc, nvc, nks, nvs, nkw, nvw, ng, mg) = \
            jnp.split(proj, split_points, axis=-1)

        o_fox = fox_attention(heads(fq, FOX_HEADS), heads(fk, FOX_HEADS), heads(fv, FOX_HEADS),
                              ff, fox_b_f[l]).reshape(B, S, FOX_W)
        o_lru = rg_lru_branch(lx, lg, lru_conv_w[l], lru_conv_b[l], lru_w_a[l], lru_b_a[l],
                              lru_w_x[l], lru_b_x[l], lru_lambda[l])
        o_dil = dilated_attention(rope(heads(dq, DIL_HEADS), pos), rope(heads(dk, DIL_HEADS), pos),
                                  heads(dv, DIL_HEADS)).reshape(B, S, DIL_OUT_W)
        o_nsa = nsa_attention(heads(nq, NSA_HEADS), heads(nkc, NSA_KV_HEADS), heads(nvc, NSA_KV_HEADS),
                              heads(nks, NSA_KV_HEADS), heads(nvs, NSA_KV_HEADS),
                              heads(nkw, NSA_KV_HEADS), heads(nvw, NSA_KV_HEADS), ng,
                              nsa_pe_k[l], nsa_pe_v[l], nsa_w1_k[l], nsa_w2_k[l],
                              nsa_w1_v[l], nsa_w2_v[l]).reshape(B, S, NSA_W)

        gates = jax.nn.sigmoid(mg).reshape(B, S, N_BRANCHES, D)
        merged = (gates[:, :, 0] * (o_fox @ w_br_fox[l])
                  + gates[:, :, 1] * (o_lru @ w_br_lru[l])
                  + gates[:, :, 2] * (o_dil @ w_br_dil[l])
                  + gates[:, :, 3] * (o_nsa @ w_br_nsa[l]))
        x = x + merged @ w_out[l]

        h2 = rms_norm(x, norm2_g[l])
        x = x + peer_ffn(h2, peer_w_q[l], peer_sub_keys[l], peer_u[l], peer_v[l])

    return rms_norm_pallas(x.reshape(B * S, D), final_g).reshape(B, S, D)
```

```python
import functools

import jax
import jax.numpy as jnp
import numpy as np
from jax import lax
from jax.experimental import pallas as pl
from jax.experimental.pallas import tpu as pltpu

D_MODEL = 1024
DEPTH = 2
HEAD_DIM = 64
Q_BLOCK = 128
ROPE_THETA = 10000.0
NORM_EPS = 1e-6
NEG_INF = -1e30
N_BRANCHES = 4
FOX_HEADS = 4
FOX_W = FOX_HEADS * HEAD_DIM
LRU_WIDTH = 256
LRU_BLOCKS = 4
LRU_CONV = 4
LRU_C = 8.0
DIL_PAIRS = ((128, 1), (512, 4), (2048, 16))
DIL_GROUPS = len(DIL_PAIRS)
DIL_HEADS_PER_GROUP = 2
DIL_HEADS = DIL_GROUPS * DIL_HEADS_PER_GROUP
DIL_W = DIL_HEADS * HEAD_DIM
DIL_OUT_W = DIL_HEADS_PER_GROUP * HEAD_DIM
DIL_KEYS = DIL_PAIRS[0][0] // DIL_PAIRS[0][1] + 1
NSA_HEADS = 4
NSA_KV_HEADS = 1
NSA_W = NSA_HEADS * HEAD_DIM
NSA_KV_W = NSA_KV_HEADS * HEAD_DIM
NSA_CMP_BLOCK = 32
NSA_CMP_STRIDE = 16
NSA_SLC_BLOCK = 64
NSA_TOP_N = 16
NSA_WINDOW = 512
NSA_FORCE_SCORE = 1e6
PEER_HEADS = 8
PEER_N_KEYS = 128
PEER_TOPK = 16
PEER_D_KEY = 256
PEER_BLOCK = 128

IN_SPLITS = (FOX_W, FOX_W, FOX_W, FOX_HEADS,
             LRU_WIDTH, LRU_WIDTH,
             DIL_W, DIL_W, DIL_W,
             NSA_W, NSA_KV_W, NSA_KV_W, NSA_KV_W, NSA_KV_W, NSA_KV_W, NSA_KV_W, NSA_HEADS * 3,
             N_BRANCHES * D_MODEL)


def _rms_norm_kernel(x_ref, g_ref, o_ref):
    x = x_ref[...]
    ms = jnp.mean(x * x, axis=-1, keepdims=True)
    o_ref[...] = x * lax.rsqrt(ms + NORM_EPS) * g_ref[...]


def rms_norm_pallas(x, g, rows=512):
    t, d = x.shape
    return pl.pallas_call(
        _rms_norm_kernel,
        grid=(t // rows,),
        in_specs=[pl.BlockSpec((rows, d), lambda i: (i, 0)),
                  pl.BlockSpec((1, d), lambda i: (0, 0))],
        out_specs=pl.BlockSpec((rows, d), lambda i: (i, 0)),
        out_shape=jax.ShapeDtypeStruct((t, d), x.dtype),
        name="rms_norm",
    )(x, g.reshape(1, d))


def _gelu_tanh(x):
    return 0.5 * x * (1.0 + jnp.tanh(0.7978845608028654 * (x + 0.044715 * x * x * x)))


PEER_TOKENS_PER_STEP = 8
LANES = 128
SUBLANES = 8


def _peer_mix_kernel(idx_hbm, x_ref, gt_ref, res_ref, uv_hbm, o_ref,
                     idx_smem, buf, idx_sem, row_sem, *, tb, n_sel, d):
    i = pl.program_id(0)
    n = pl.num_programs(0)
    rows = tb * n_sel
    slot = i % 2

    def idx_copy(step, s):
        return pltpu.make_async_copy(idx_hbm.at[step], idx_smem.at[pl.ds(s * rows, rows)],
                                     idx_sem.at[s])

    def issue_rows(s):
        base = s * rows

        def body(jj, c):
            for k in range(SUBLANES):
                row = idx_smem[base + jj * SUBLANES + k]
                pltpu.make_async_copy(
                    uv_hbm.at[lax.shift_right_logical(row, 3), pl.ds(row & (SUBLANES - 1), 1)],
                    buf.at[s, jj, pl.ds(k, 1)], row_sem.at[s]).start()
            return c
        lax.fori_loop(0, rows // SUBLANES, body, 0)

    @pl.when(i == 0)
    def _():
        idx_copy(0, 0).start()
        idx_copy(0, 0).wait()
        issue_rows(0)

        @pl.when(n > 1)
        def _():
            idx_copy(1, 1).start()

    @pl.when(i + 1 < n)
    def _():
        idx_copy(i + 1, 1 - slot).wait()
        issue_rows(1 - slot)

    pltpu.make_async_copy(uv_hbm.at[pl.ds(0, rows // SUBLANES)], buf.at[slot],
                          row_sem.at[slot]).wait()

    @pl.when(i + 2 < n)
    def _():
        idx_copy(i + 2, slot).start()

    n_chunks = d // LANES
    groups = n_sel // SUBLANES

    def sel_rows(t, lane0):
        tile = buf[slot, pl.ds(t * groups, groups), :, pl.ds(lane0, LANES)]
        return tile.reshape(n_sel, LANES)

    for t in range(tb):
        acc = jnp.zeros((n_sel, LANES), jnp.float32)
        for c in range(n_chunks):
            xc = x_ref[pl.ds(t, 1), pl.ds(c * LANES, LANES)]
            acc = acc + sel_rows(t, c * LANES) * xc
        s = jnp.sum(acc, axis=-1, keepdims=True)
        w = gt_ref[0, :, pl.ds(t, 1)] * _gelu_tanh(s)
        wb = jnp.broadcast_to(w, (n_sel, LANES))
        for c in range(n_chunks):
            vc = sel_rows(t, d + c * LANES)
            oc = jnp.sum(wb * vc, axis=0, keepdims=True)
            o_ref[pl.ds(t, 1), pl.ds(c * LANES, LANES)] = (
                res_ref[pl.ds(t, 1), pl.ds(c * LANES, LANES)] + oc)


def peer_mix(x, res, idx, g, uv):
    t, d = x.shape
    n_sel = idx.shape[1]
    tb = PEER_TOKENS_PER_STEP
    steps = t // tb
    idx_steps = idx.reshape(steps, tb * n_sel)
    gt = jnp.transpose(g.reshape(steps, tb, n_sel), (0, 2, 1))
    kern = functools.partial(_peer_mix_kernel, tb=tb, n_sel=n_sel, d=d)
    return pl.pallas_call(
        kern,
        grid=(steps,),
        in_specs=[pl.BlockSpec(memory_space=pl.ANY),
                  pl.BlockSpec((tb, d), lambda i: (i, 0)),
                  pl.BlockSpec((1, n_sel, tb), lambda i: (i, 0, 0)),
                  pl.BlockSpec((tb, d), lambda i: (i, 0)),
                  pl.BlockSpec(memory_space=pl.ANY)],
        out_specs=pl.BlockSpec((tb, d), lambda i: (i, 0)),
        out_shape=jax.ShapeDtypeStruct((t, d), jnp.float32),
        scratch_shapes=[pltpu.SMEM((2 * tb * n_sel,), jnp.int32),
                        pltpu.VMEM((2, tb * n_sel // SUBLANES, SUBLANES, 2 * d), jnp.float32),
                        pltpu.SemaphoreType.DMA((2,)),
                        pltpu.SemaphoreType.DMA((2,))],
        compiler_params=pltpu.CompilerParams(
            dimension_semantics=("arbitrary",),
            vmem_limit_bytes=2 * tb * n_sel * 2 * d * 4 + (8 << 20)),
        name="peer_mix",
    )(idx_steps, x, gt, res, uv.reshape(uv.shape[0] // SUBLANES, SUBLANES, 2 * d))


NSA_KV_TILE = 256


def _nt_dot(a, b):
    return lax.dot_general(a, b, (((1,), (1,)), ((), ())), preferred_element_type=jnp.float32)


def _nsa_kernel(qn_ref, qr_ref, kc_ref, vc_ref, ov_ref, ks_ref, vs_ref, kw_ref, vw_ref, gl_ref,
                o_ref, m_s, l_s, acc_s, out_s, *, n_heads, qb, n_cmp_pad, n_slc):
    j = pl.program_id(1)
    s0 = j * qb
    hq = n_heads * qb
    bf16 = jnp.bfloat16
    scale = HEAD_DIM ** -0.5
    q_pos = s0 + lax.broadcasted_iota(jnp.int32, (qb, 1), 0)

    gates = jax.nn.sigmoid(gl_ref[0])

    def gate_col(br):
        return jnp.concatenate([gates[:, 3 * h + br:3 * h + br + 1] for h in range(n_heads)], axis=0)

    def masked_update(logits, mask, v_tile):
        kt = logits.shape[-1]
        lg = jnp.where(mask[None], logits.reshape(n_heads, qb, kt), NEG_INF)
        m_old = m_s[...].reshape(n_heads, qb, 1)
        m_new = jnp.maximum(m_old, jnp.max(lg, axis=-1, keepdims=True))
        p = jnp.where(mask[None], jnp.exp(lg - m_new), 0.0)
        alpha = jnp.exp(m_old - m_new)
        l_s[...] = (alpha * l_s[...].reshape(n_heads, qb, 1)
                    + jnp.sum(p, axis=-1, keepdims=True)).reshape(hq, 1)
        pv = jnp.dot(p.reshape(hq, kt).astype(bf16), v_tile, preferred_element_type=jnp.float32)
        acc_s[...] = alpha.reshape(hq, 1) * acc_s[...] + pv
        m_s[...] = m_new.reshape(hq, 1)

    def reset_state():
        m_s[...] = jnp.full((hq, 1), NEG_INF, jnp.float32)
        l_s[...] = jnp.zeros((hq, 1), jnp.float32)
        acc_s[...] = jnp.zeros((hq, HEAD_DIM), jnp.float32)

    def normalized():
        return acc_s[...] / jnp.maximum(l_s[...], 1e-30)

    qn = (qn_ref[0].reshape(hq, HEAD_DIM) * scale).astype(bf16)
    lc = _nt_dot(qn, kc_ref[0]).reshape(n_heads, qb, n_cmp_pad)
    c_end = (lax.broadcasted_iota(jnp.int32, (qb, n_cmp_pad), 1) * NSA_CMP_STRIDE
             + (NSA_CMP_BLOCK - 1))
    c_mask = c_end <= q_pos
    lc = jnp.where(c_mask[None], lc, NEG_INF)
    mc = jnp.max(lc, axis=-1, keepdims=True)
    ec = jnp.where(c_mask[None], jnp.exp(lc - mc), 0.0)
    pc = ec / jnp.maximum(jnp.sum(ec, axis=-1, keepdims=True), 1e-30)
    o_c = jnp.dot(pc.reshape(hq, n_cmp_pad).astype(bf16), vc_ref[0],
                  preferred_element_type=jnp.float32)
    out_s[...] = gate_col(0) * o_c

    p_sum = jnp.sum(pc, axis=0)
    p_hi = p_sum.astype(bf16)
    p_lo = (p_sum - p_hi.astype(jnp.float32)).astype(bf16)
    imp = (jnp.dot(p_hi, ov_ref[...], preferred_element_type=jnp.float32)
           + jnp.dot(p_lo, ov_ref[...], preferred_element_type=jnp.float32))
    blk = lax.broadcasted_iota(jnp.int32, (qb, n_slc), 1)
    cur = q_pos // NSA_SLC_BLOCK
    forced = (blk == 0) | (blk == cur) | (blk == cur - 1)
    imp = jnp.where(forced, NSA_FORCE_SCORE, imp)
    valid = blk <= cur
    work = jnp.where(valid, imp, NEG_INF)
    sel = jnp.zeros((qb, n_slc), jnp.bool_)
    for _ in range(min(NSA_TOP_N, n_slc)):
        mx = jnp.max(work, axis=-1, keepdims=True)
        first = jnp.min(jnp.where(work == mx, blk, n_slc), axis=-1, keepdims=True)
        pick = blk == first
        sel = sel | pick
        work = jnp.where(pick, -jnp.inf, work)
    sel_bf = jnp.where(sel & valid, 1.0, 0.0).astype(bf16)

    qr = (qr_ref[0].reshape(hq, HEAD_DIM) * scale).astype(bf16)
    reset_state()
    n_win_tiles = NSA_WINDOW // qb + 1
    w_first = jnp.maximum(0, n_win_tiles - 1 - j)

    def win_body(w, c):
        k0 = pl.multiple_of(s0 - NSA_WINDOW + w * qb, qb)
        k_pos = k0 + lax.broadcasted_iota(jnp.int32, (qb, qb), 1)
        dist = q_pos - k_pos
        mask = (dist >= 0) & (dist < NSA_WINDOW)
        masked_update(_nt_dot(qr, kw_ref[0, pl.ds(k0, qb), :]), mask, vw_ref[0, pl.ds(k0, qb), :])
        return c
    lax.fori_loop(w_first, n_win_tiles, win_body, 0)
    out_s[...] += gate_col(2) * normalized()

    reset_state()
    kt = NSA_KV_TILE
    blocks_per_tile = kt // NSA_SLC_BLOCK
    n_tiles = (s0 + qb - 1) // kt + 1

    def sel_body(t, c):
        k0 = pl.multiple_of(t * kt, kt)
        k_idx = lax.broadcasted_iota(jnp.int32, (n_slc, kt), 1)
        expand = (lax.broadcasted_iota(jnp.int32, (n_slc, kt), 0)
                  == t * blocks_per_tile + k_idx // NSA_SLC_BLOCK)
        picked = jnp.dot(sel_bf, jnp.where(expand, 1.0, 0.0).astype(bf16),
                         preferred_element_type=jnp.float32)
        k_pos = k0 + lax.broadcasted_iota(jnp.int32, (qb, kt), 1)
        mask = (picked > 0.5) & (k_pos <= q_pos)
        masked_update(_nt_dot(qr, ks_ref[0, pl.ds(k0, kt), :]), mask, vs_ref[0, pl.ds(k0, kt), :])
        return c
    lax.fori_loop(0, n_tiles, sel_body, 0)
    out_s[...] += gate_col(1) * normalized()

    o_ref[0] = out_s[...].reshape(n_heads, qb, HEAD_DIM)


def nsa_attention_pallas(qn, qr, kc, vc, overlap, ks, vs, kw, vw, gl):
    b, h, s, dh = qn.shape
    qb = Q_BLOCK
    n_cmp_pad = kc.shape[1]
    n_slc = overlap.shape[1]
    kern = functools.partial(_nsa_kernel, n_heads=h, qb=qb, n_cmp_pad=n_cmp_pad, n_slc=n_slc)
    q_spec = pl.BlockSpec((1, h, qb, dh), lambda i, j: (i, 0, j, 0))
    cmp_spec = pl.BlockSpec((1, n_cmp_pad, dh), lambda i, j: (i, 0, 0))
    kv_spec = pl.BlockSpec((1, s, dh), lambda i, j: (i, 0, 0))
    return pl.pallas_call(
        kern,
        grid=(b, s // qb),
        in_specs=[q_spec, q_spec, cmp_spec, cmp_spec,
                  pl.BlockSpec((n_cmp_pad, n_slc), lambda i, j: (0, 0)),
                  kv_spec, kv_spec, kv_spec, kv_spec,
                  pl.BlockSpec((1, qb, 3 * h), lambda i, j: (i, j, 0))],
        out_specs=q_spec,
        out_shape=jax.ShapeDtypeStruct((b, h, s, dh), jnp.float32),
        scratch_shapes=[pltpu.VMEM((h * qb, 1), jnp.float32),
                        pltpu.VMEM((h * qb, 1), jnp.float32),
                        pltpu.VMEM((h * qb, dh), jnp.float32),
                        pltpu.VMEM((h * qb, dh), jnp.float32)],
        compiler_params=pltpu.CompilerParams(
            dimension_semantics=("arbitrary", "arbitrary"),
            vmem_limit_bytes=40 << 20),
        name="nsa_attention",
    )(qn, qr, kc, vc, overlap, ks, vs, kw, vw, gl)


def rms_norm(x, g):
    xf = x.astype(jnp.float32)
    y = xf * lax.rsqrt(jnp.mean(xf * xf, axis=-1, keepdims=True) + NORM_EPS)
    return (y * g.astype(jnp.float32)).astype(x.dtype)


def rope(x, pos):
    half = x.shape[-1] // 2
    freqs = ROPE_THETA ** (-jnp.arange(half, dtype=jnp.float32) / half)
    ang = pos.astype(jnp.float32)[:, None] * freqs[None, :]
    cos = jnp.cos(ang)[None, :, None, :]
    sin = jnp.sin(ang)[None, :, None, :]
    xf = x.astype(jnp.float32)
    x1, x2 = xf[..., :half], xf[..., half:]
    return jnp.concatenate([x1 * cos - x2 * sin, x2 * cos + x1 * sin], axis=-1).astype(x.dtype)


def _unblock(y):
    y = jnp.moveaxis(y, 0, 1)
    return y.reshape((y.shape[0], y.shape[1] * y.shape[2]) + y.shape[3:])


def masked_softmax(logits, mask):
    logits = jnp.where(mask, logits, NEG_INF)
    m = jnp.max(logits, axis=-1, keepdims=True)
    e = jnp.where(mask, jnp.exp(logits - m), 0.0)
    return e / jnp.maximum(jnp.sum(e, axis=-1, keepdims=True), 1e-30)


def fox_attention(q, k, v, f_logit, b_f):
    B, S, H, dh = q.shape
    log_f = jax.nn.log_sigmoid(f_logit.astype(jnp.float32) + b_f.astype(jnp.float32))
    cum = jnp.transpose(jnp.cumsum(log_f, axis=1), (0, 2, 1))
    scale = dh ** -0.5
    k_pos = jnp.arange(S)

    def block(i):
        s0 = i * Q_BLOCK
        qb = lax.dynamic_slice_in_dim(q, s0, Q_BLOCK, axis=1)
        cq = lax.dynamic_slice_in_dim(cum, s0, Q_BLOCK, axis=2)
        logits = jnp.einsum('bqhd,bkhd->bhqk', qb, k).astype(jnp.float32) * scale
        logits = logits + cq[..., None] - cum[:, :, None, :]
        q_pos = s0 + jnp.arange(Q_BLOCK)
        logits = jnp.where(k_pos[None, :] <= q_pos[:, None], logits, NEG_INF)
        p = jax.nn.softmax(logits, axis=-1)
        return jnp.einsum('bhqk,bkhd->bqhd', p.astype(v.dtype), v)

    return _unblock(lax.map(block, jnp.arange(S // Q_BLOCK)))


def rg_lru_branch(x_in, gate, conv_w, conv_b, w_a, b_a, w_x, b_x, lam):
    B, S, C = x_in.shape
    xc = lax.conv_general_dilated(x_in, conv_w[:, None, :], window_strides=(1,),
                                  padding=[(LRU_CONV - 1, 0)],
                                  dimension_numbers=('NWC', 'WIO', 'NWC'),
                                  feature_group_count=C) + conv_b
    xr = xc.reshape(B, S, LRU_BLOCKS, C // LRU_BLOCKS)
    r = jax.nn.sigmoid(jnp.einsum('bsnc,ncd->bsnd', xr, w_a).reshape(B, S, C) + b_a)
    i_g = jax.nn.sigmoid(jnp.einsum('bsnc,ncd->bsnd', xr, w_x).reshape(B, S, C) + b_x)
    log_a = -LRU_C * r.astype(jnp.float32) * jax.nn.softplus(-lam.astype(jnp.float32))
    a = jnp.exp(log_a)
    u = jnp.sqrt(-jnp.expm1(2.0 * log_a)) * (i_g * xc).astype(jnp.float32)

    def combine(left, right):
        a_l, b_l = left
        a_r, b_r = right
        return a_l * a_r, a_r * b_l + b_r

    _, h = lax.associative_scan(combine, (a, u), axis=1)
    return h.astype(x_in.dtype) * jax.nn.gelu(gate)


def dilated_attention(q, k, v):
    B, S, H, dh = q.shape
    G, Hg = DIL_GROUPS, DIL_HEADS_PER_GROUP
    qg = q.reshape(B, S, G, Hg, dh)
    kg = k.reshape(B, S, G, Hg, dh)
    vg = v.reshape(B, S, G, Hg, dh)
    dil = jnp.array([d for _, d in DIL_PAIRS], dtype=jnp.int32)
    steps = jnp.arange(DIL_KEYS)
    g_idx = jnp.arange(G)[:, None, None]
    scale = dh ** -0.5

    def block(i):
        s0 = i * Q_BLOCK
        q_pos = s0 + jnp.arange(Q_BLOCK)
        k_pos = q_pos[None, :, None] - dil[:, None, None] * steps[None, None, :]
        valid = k_pos >= 0
        k_idx = jnp.maximum(k_pos, 0)
        kb = kg[:, k_idx, g_idx]
        vb = vg[:, k_idx, g_idx]
        qb = lax.dynamic_slice_in_dim(qg, s0, Q_BLOCK, axis=1)
        logits = jnp.einsum('bqghd,bgqkhd->bgqhk', qb, kb).astype(jnp.float32) * scale
        logits = jnp.where(valid[None, :, :, None, :], logits, NEG_INF)
        m = jnp.max(logits, axis=-1, keepdims=True)
        e = jnp.exp(logits - m)
        l = jnp.sum(e, axis=-1)
        o = jnp.einsum('bgqhk,bgqkhd->bgqhd', e, vb.astype(jnp.float32)) / l[..., None]
        lse = m[..., 0] + jnp.log(l)
        w = jax.nn.softmax(lse, axis=1)
        return jnp.einsum('bgqh,bgqhd->bqhd', w, o).astype(v.dtype)

    return _unblock(lax.map(block, jnp.arange(S // Q_BLOCK)))


def nsa_attention(q, k_cmp, v_cmp, k_slc, v_slc, k_win, v_win, gate_logits,
                  pe_k, pe_v, w1_k, w2_k, w1_v, w2_v):
    B, S, H, dh = q.shape
    Hkv = k_cmp.shape[2]
    G = H // Hkv
    pos = jnp.arange(S)
    q_rot = rope(q, pos)
    k_slc = rope(k_slc, pos)
    k_win = rope(k_win, pos)
    gates = jax.nn.sigmoid(gate_logits).reshape(B, S, H, 3)
    scale = dh ** -0.5

    n_cmp = (S - NSA_CMP_BLOCK) // NSA_CMP_STRIDE + 1
    cmp_idx = jnp.arange(n_cmp)[:, None] * NSA_CMP_STRIDE + jnp.arange(NSA_CMP_BLOCK)[None, :]

    def compress(kv, pe, w1, w2):
        blocks = kv[:, cmp_idx] + pe[None, None, :, None, :]
        flat = jnp.moveaxis(blocks, 3, 2).reshape(B, n_cmp, Hkv, NSA_CMP_BLOCK * dh)
        return jax.nn.gelu(flat @ w1) @ w2

    kc = compress(k_cmp, pe_k, w1_k, w2_k)
    vc = compress(v_cmp, pe_v, w1_v, w2_v)
    cmp_start = jnp.arange(n_cmp) * NSA_CMP_STRIDE
    cmp_end = cmp_start + NSA_CMP_BLOCK - 1

    n_slc = S // NSA_SLC_BLOCK
    top_n = min(NSA_TOP_N, n_slc)
    slc_start = jnp.arange(n_slc) * NSA_SLC_BLOCK
    overlap = ((cmp_start[:, None] < slc_start[None, :] + NSA_SLC_BLOCK)
               & (cmp_start[:, None] + NSA_CMP_BLOCK > slc_start[None, :])).astype(jnp.float32)
    ks_blocks = jnp.transpose(k_slc.reshape(B, n_slc, NSA_SLC_BLOCK, Hkv, dh), (0, 3, 1, 2, 4))
    vs_blocks = jnp.transpose(v_slc.reshape(B, n_slc, NSA_SLC_BLOCK, Hkv, dh), (0, 3, 1, 2, 4))
    pad = ((0, 0), (NSA_WINDOW, 0), (0, 0), (0, 0))
    k_win_pad = jnp.pad(k_win, pad)
    v_win_pad = jnp.pad(v_win, pad)
    b_idx = jnp.arange(B)[:, None, None, None]
    h_idx = jnp.arange(Hkv)[None, :, None, None]
    blk_ids = jnp.arange(n_slc)

    def block(i):
        s0 = i * Q_BLOCK
        q_pos = s0 + jnp.arange(Q_BLOCK)
        qb = lax.dynamic_slice_in_dim(q, s0, Q_BLOCK, axis=1).reshape(B, Q_BLOCK, Hkv, G, dh)
        qrb = lax.dynamic_slice_in_dim(q_rot, s0, Q_BLOCK, axis=1).reshape(B, Q_BLOCK, Hkv, G, dh)
        lc = jnp.einsum('bqkgd,bckd->bkgqc', qb, kc).astype(jnp.float32) * scale
        p_c = masked_softmax(lc, (cmp_end[None, :] <= q_pos[:, None])[None, None, None])
        o_c = jnp.einsum('bkgqc,bckd->bqkgd', p_c, vc.astype(jnp.float32))
        imp = jnp.einsum('bkgqc,cn->bkqn', p_c, overlap)
        cur = q_pos // NSA_SLC_BLOCK
        forced = ((blk_ids[None, :] == 0) | (blk_ids[None, :] == cur[:, None])
                  | (blk_ids[None, :] == cur[:, None] - 1))
        imp = jnp.where(forced, NSA_FORCE_SCORE, imp)
        imp = jnp.where(blk_ids[None, :] <= cur[:, None], imp, NEG_INF)
        top_val, top_idx = lax.top_k(imp, top_n)
        sel_ok = top_val > 0.5 * NEG_INF
        ksel = ks_blocks[b_idx, h_idx, top_idx]
        vsel = vs_blocks[b_idx, h_idx, top_idx]
        sel_pos = top_idx[..., None] * NSA_SLC_BLOCK + jnp.arange(NSA_SLC_BLOCK)
        sel_mask = sel_ok[..., None] & (sel_pos <= q_pos[None, None, :, None, None])
        n_sel_keys = top_n * NSA_SLC_BLOCK
        ls = jnp.einsum('bqkgd,bkqnld->bkgqnl', qrb, ksel).astype(jnp.float32) * scale
        ls = ls.reshape(B, Hkv, G, Q_BLOCK, n_sel_keys)
        p_s = masked_softmax(ls, sel_mask.reshape(B, Hkv, 1, Q_BLOCK, n_sel_keys))
        o_s = jnp.einsum('bkgqm,bkqmd->bqkgd', p_s,
                         vsel.reshape(B, Hkv, Q_BLOCK, n_sel_keys, dh).astype(jnp.float32))
        kw = lax.dynamic_slice_in_dim(k_win_pad, s0, NSA_WINDOW + Q_BLOCK, axis=1)
        vw = lax.dynamic_slice_in_dim(v_win_pad, s0, NSA_WINDOW + Q_BLOCK, axis=1)
        w_pos = s0 - NSA_WINDOW + jnp.arange(NSA_WINDOW + Q_BLOCK)
        dist = q_pos[:, None] - w_pos[None, :]
        w_mask = (dist >= 0) & (dist < NSA_WINDOW) & (w_pos[None, :] >= 0)
        lw = jnp.einsum('bqkgd,bwkd->bkgqw', qrb, kw).astype(jnp.float32) * scale
        p_w = masked_softmax(lw, w_mask)
        o_w = jnp.einsum('bkgqw,bwkd->bqkgd', p_w, vw.astype(jnp.float32))
        gb = lax.dynamic_slice_in_dim(gates, s0, Q_BLOCK, axis=1).reshape(B, Q_BLOCK, Hkv, G, 3)
        gb = gb.astype(jnp.float32)
        out = gb[..., 0:1] * o_c + gb[..., 1:2] * o_s + gb[..., 2:3] * o_w
        return out.reshape(B, Q_BLOCK, H, dh).astype(q.dtype)

    return _unblock(lax.map(block, jnp.arange(S // Q_BLOCK)))


def nsa_branch(nq, nkc, nvc, nks, nvs, nkw, nvw, ng, pe_k, pe_v, w1_k, w2_k, w1_v, w2_v):
    B, S, _ = nq.shape
    dh = HEAD_DIM
    bf16 = jnp.bfloat16
    pos = jnp.arange(S)
    q4 = nq.reshape(B, S, NSA_HEADS, dh)
    qn = jnp.transpose(q4, (0, 2, 1, 3))
    qr = jnp.transpose(rope(q4, pos), (0, 2, 1, 3))
    ks = rope(nks.reshape(B, S, 1, dh), pos).reshape(B, S, dh).astype(bf16)
    kw = rope(nkw.reshape(B, S, 1, dh), pos).reshape(B, S, dh).astype(bf16)

    n_cmp = (S - NSA_CMP_BLOCK) // NSA_CMP_STRIDE + 1
    n_cmp_pad = S // NSA_CMP_STRIDE
    n_slc = S // NSA_SLC_BLOCK

    def compress(kv, pe, w1, w2):
        chunks = kv.reshape(B, n_cmp_pad, NSA_CMP_STRIDE * dh)
        flat = jnp.concatenate([chunks[:, :-1], chunks[:, 1:]], axis=-1) + pe.reshape(-1)
        out = jax.nn.gelu(flat @ w1) @ w2
        return jnp.pad(out, ((0, 0), (0, n_cmp_pad - n_cmp), (0, 0))).astype(bf16)

    kc = compress(nkc, pe_k, w1_k, w2_k)
    vc = compress(nvc, pe_v, w1_v, w2_v)
    cmp_start = jnp.arange(n_cmp_pad) * NSA_CMP_STRIDE
    slc_start = jnp.arange(n_slc) * NSA_SLC_BLOCK
    overlap = ((cmp_start[:, None] < slc_start[None, :] + NSA_SLC_BLOCK)
               & (cmp_start[:, None] + NSA_CMP_BLOCK > slc_start[None, :])
               & (jnp.arange(n_cmp_pad)[:, None] < n_cmp)).astype(bf16)
    out = nsa_attention_pallas(qn, qr, kc, vc, overlap, ks, nvs.astype(bf16), kw,
                               nvw.astype(bf16), ng)
    return jnp.transpose(out, (0, 2, 1, 3)).reshape(B, S, NSA_W)


def peer_ffn(x_res, h2, w_q, sub_keys, u, v):
    T, D = h2.shape
    xt = h2.reshape(T // PEER_BLOCK, PEER_BLOCK, D)
    k = PEER_TOPK

    def block(xb):
        q = (xb @ w_q).reshape(PEER_BLOCK, PEER_HEADS, 2, PEER_D_KEY // 2)
        s = jnp.einsum('thpd,pnd->thpn', q, sub_keys).astype(jnp.float32)
        v_half, i_half = lax.top_k(s, k)
        cand = v_half[:, :, 0, :, None] + v_half[:, :, 1, None, :]
        cand_idx = i_half[:, :, 0, :, None] * PEER_N_KEYS + i_half[:, :, 1, None, :]
        top_s, pos = lax.top_k(cand.reshape(PEER_BLOCK, PEER_HEADS, k * k), k)
        expert = jnp.take_along_axis(cand_idx.reshape(PEER_BLOCK, PEER_HEADS, k * k), pos, axis=-1)
        g = jax.nn.softmax(top_s, axis=-1)
        return expert.reshape(PEER_BLOCK, PEER_HEADS * k), g.reshape(PEER_BLOCK, PEER_HEADS * k)

    expert, g = lax.map(block, xt)
    uv = jnp.concatenate([u, v], axis=1)
    return peer_mix(h2, x_res, expert.reshape(T, PEER_HEADS * k).astype(jnp.int32),
                    g.reshape(T, PEER_HEADS * k), uv)


def kernel(x, norm1_g, w_in, fox_b_f, lru_conv_w, lru_conv_b, lru_w_a, lru_b_a, lru_w_x, lru_b_x, lru_lambda, nsa_pe_k, nsa_pe_v, nsa_w1_k, nsa_w2_k, nsa_w1_v, nsa_w2_v, w_br_fox, w_br_lru, w_br_dil, w_br_nsa, w_out, norm2_g, peer_w_q, peer_sub_keys, peer_u, peer_v, final_g):
    B, S, D = x.shape
    pos = jnp.arange(S)
    split_points = np.cumsum(IN_SPLITS)[:-1].tolist()

    def heads(t, n):
        return t.reshape(B, S, n, HEAD_DIM)

    for l in range(DEPTH):
        h = rms_norm(x, norm1_g[l])
        proj = h @ w_in[l]
        (fq, fk, fv, ff, lx, lg, dq, dk, dv, nq, nkc, nvc, nks, nvs, nkw, nvw, ng, mg) = \
            jnp.split(proj, split_points, axis=-1)

        o_fox = fox_attention(heads(fq, FOX_HEADS), heads(fk, FOX_HEADS), heads(fv, FOX_HEADS),
                              ff, fox_b_f[l]).reshape(B, S, FOX_W)
        o_lru = rg_lru_branch(lx, lg, lru_conv_w[l], lru_conv_b[l], lru_w_a[l], lru_b_a[l],
                              lru_w_x[l], lru_b_x[l], lru_lambda[l])
        o_dil = dilated_attention(rope(heads(dq, DIL_HEADS), pos), rope(heads(dk, DIL_HEADS), pos),
                                  heads(dv, DIL_HEADS)).reshape(B, S, DIL_OUT_W)
        o_nsa = nsa_branch(nq, nkc, nvc, nks, nvs, nkw, nvw, ng,
                           nsa_pe_k[l], nsa_pe_v[l], nsa_w1_k[l], nsa_w2_k[l],
                           nsa_w1_v[l], nsa_w2_v[l])

        gates = jax.nn.sigmoid(mg).reshape(B, S, N_BRANCHES, D)
        merged = (gates[:, :, 0] * (o_fox @ w_br_fox[l])
                  + gates[:, :, 1] * (o_lru @ w_br_lru[l])
                  + gates[:, :, 2] * (o_dil @ w_br_dil[l])
                  + gates[:, :, 3] * (o_nsa @ w_br_nsa[l]))
        x = x + merged @ w_out[l]

        h2 = rms_norm(x, norm2_g[l])
        x = peer_ffn(x.reshape(B * S, D), h2.reshape(B * S, D), peer_w_q[l], peer_sub_keys[l],
                     peer_u[l], peer_v[l]).reshape(B, S, D)

    return rms_norm_pallas(x.reshape(B * S, D), final_g).reshape(B, S, D)
```

```python
import functools

import jax
import jax.numpy as jnp
import numpy as np
from jax import lax
from jax.experimental import pallas as pl
from jax.experimental.pallas import tpu as pltpu

D_MODEL = 1024
DEPTH = 2
HEAD_DIM = 64
Q_BLOCK = 128
ROPE_THETA = 10000.0
NORM_EPS = 1e-6
NEG_INF = -1e30
N_BRANCHES = 4
FOX_HEADS = 4
FOX_W = FOX_HEADS * HEAD_DIM
LRU_WIDTH = 256
LRU_BLOCKS = 4
LRU_CONV = 4
LRU_C = 8.0
DIL_PAIRS = ((128, 1), (512, 4), (2048, 16))
DIL_GROUPS = len(DIL_PAIRS)
DIL_HEADS_PER_GROUP = 2
DIL_HEADS = DIL_GROUPS * DIL_HEADS_PER_GROUP
DIL_W = DIL_HEADS * HEAD_DIM
DIL_OUT_W = DIL_HEADS_PER_GROUP * HEAD_DIM
DIL_KEYS = DIL_PAIRS[0][0] // DIL_PAIRS[0][1] + 1
NSA_HEADS = 4
NSA_KV_HEADS = 1
NSA_W = NSA_HEADS * HEAD_DIM
NSA_KV_W = NSA_KV_HEADS * HEAD_DIM
NSA_CMP_BLOCK = 32
NSA_CMP_STRIDE = 16
NSA_SLC_BLOCK = 64
NSA_TOP_N = 16
NSA_WINDOW = 512
NSA_FORCE_SCORE = 1e6
PEER_HEADS = 8
PEER_N_KEYS = 128
PEER_TOPK = 16
PEER_D_KEY = 256
PEER_BLOCK = 128

IN_SPLITS = (FOX_W, FOX_W, FOX_W, FOX_HEADS,
             LRU_WIDTH, LRU_WIDTH,
             DIL_W, DIL_W, DIL_W,
             NSA_W, NSA_KV_W, NSA_KV_W, NSA_KV_W, NSA_KV_W, NSA_KV_W, NSA_KV_W, NSA_HEADS * 3,
             N_BRANCHES * D_MODEL)


def _rms_norm_kernel(x_ref, g_ref, o_ref):
    x = x_ref[...]
    ms = jnp.mean(x * x, axis=-1, keepdims=True)
    o_ref[...] = x * lax.rsqrt(ms + NORM_EPS) * g_ref[...]


def rms_norm_pallas(x, g, rows=512):
    t, d = x.shape
    return pl.pallas_call(
        _rms_norm_kernel,
        grid=(t // rows,),
        in_specs=[pl.BlockSpec((rows, d), lambda i: (i, 0)),
                  pl.BlockSpec((1, d), lambda i: (0, 0))],
        out_specs=pl.BlockSpec((rows, d), lambda i: (i, 0)),
        out_shape=jax.ShapeDtypeStruct((t, d), x.dtype),
        name="rms_norm",
    )(x, g.reshape(1, d))


def _gelu_tanh(x):
    return 0.5 * x * (1.0 + jnp.tanh(0.7978845608028654 * (x + 0.044715 * x * x * x)))


PEER_TOKENS_PER_STEP = 8
LANES = 128
SUBLANES = 8


def _peer_mix_kernel(idx_hbm, x_ref, gt_ref, res_ref, uv_hbm, o_ref,
                     idx_smem, buf, idx_sem, row_sem, *, tb, n_sel, d):
    i = pl.program_id(0)
    n = pl.num_programs(0)
    rows = tb * n_sel
    slot = i % 2

    def idx_copy(step, s):
        return pltpu.make_async_copy(idx_hbm.at[step], idx_smem.at[pl.ds(s * rows, rows)],
                                     idx_sem.at[s])

    def issue_rows(s):
        base = s * rows

        def body(jj, c):
            for k in range(SUBLANES):
                row = idx_smem[base + jj * SUBLANES + k]
                pltpu.make_async_copy(
                    uv_hbm.at[lax.shift_right_logical(row, 3), pl.ds(row & (SUBLANES - 1), 1)],
                    buf.at[s, jj, pl.ds(k, 1)], row_sem.at[s]).start()
            return c
        lax.fori_loop(0, rows // SUBLANES, body, 0)

    @pl.when(i == 0)
    def _():
        idx_copy(0, 0).start()
        idx_copy(0, 0).wait()
        issue_rows(0)

        @pl.when(n > 1)
        def _():
            idx_copy(1, 1).start()

    @pl.when(i + 1 < n)
    def _():
        idx_copy(i + 1, 1 - slot).wait()
        issue_rows(1 - slot)

    pltpu.make_async_copy(uv_hbm.at[pl.ds(0, rows // SUBLANES)], buf.at[slot],
                          row_sem.at[slot]).wait()

    @pl.when(i + 2 < n)
    def _():
        idx_copy(i + 2, slot).start()

    n_chunks = d // LANES
    groups = n_sel // SUBLANES

    def sel_rows(t, lane0):
        tile = buf[slot, pl.ds(t * groups, groups), :, pl.ds(lane0, LANES)]
        return tile.reshape(n_sel, LANES)

    for t in range(tb):
        acc = jnp.zeros((n_sel, LANES), jnp.float32)
        for c in range(n_chunks):
            xc = x_ref[pl.ds(t, 1), pl.ds(c * LANES, LANES)]
            acc = acc + sel_rows(t, c * LANES) * xc
        s = jnp.sum(acc, axis=-1, keepdims=True)
        w = gt_ref[0, :, pl.ds(t, 1)] * _gelu_tanh(s)
        wb = jnp.broadcast_to(w, (n_sel, LANES))
        for c in range(n_chunks):
            vc = sel_rows(t, d + c * LANES)
            oc = jnp.sum(wb * vc, axis=0, keepdims=True)
            o_ref[pl.ds(t, 1), pl.ds(c * LANES, LANES)] = (
                res_ref[pl.ds(t, 1), pl.ds(c * LANES, LANES)] + oc)


def peer_mix(x, res, idx, gt_blocks, uv):
    t, d = x.shape
    n_sel = idx.shape[1]
    tb = PEER_TOKENS_PER_STEP
    steps = t // tb
    idx_steps = idx.reshape(steps, tb * n_sel)
    nb, _, tq = gt_blocks.shape
    gt = jnp.transpose(gt_blocks.reshape(nb, n_sel, tq // tb, tb), (0, 2, 1, 3)).reshape(steps, n_sel, tb)
    kern = functools.partial(_peer_mix_kernel, tb=tb, n_sel=n_sel, d=d)
    return pl.pallas_call(
        kern,
        grid=(steps,),
        in_specs=[pl.BlockSpec(memory_space=pl.ANY),
                  pl.BlockSpec((tb, d), lambda i: (i, 0)),
                  pl.BlockSpec((1, n_sel, tb), lambda i: (i, 0, 0)),
                  pl.BlockSpec((tb, d), lambda i: (i, 0)),
                  pl.BlockSpec(memory_space=pl.ANY)],
        out_specs=pl.BlockSpec((tb, d), lambda i: (i, 0)),
        out_shape=jax.ShapeDtypeStruct((t, d), jnp.float32),
        scratch_shapes=[pltpu.SMEM((2 * tb * n_sel,), jnp.int32),
                        pltpu.VMEM((2, tb * n_sel // SUBLANES, SUBLANES, 2 * d), jnp.float32),
                        pltpu.SemaphoreType.DMA((2,)),
                        pltpu.SemaphoreType.DMA((2,))],
        compiler_params=pltpu.CompilerParams(
            dimension_semantics=("arbitrary",),
            vmem_limit_bytes=2 * tb * n_sel * 2 * d * 4 + (8 << 20)),
        name="peer_mix",
    )(idx_steps, x, gt, res, uv.reshape(uv.shape[0] // SUBLANES, SUBLANES, 2 * d))


def _topk_rows(work, k):
    n = work.shape[0]
    row = lax.broadcasted_iota(jnp.int32, work.shape, 0)
    vals, idxs = [], []
    for _ in range(k):
        mx = jnp.max(work, axis=0, keepdims=True)
        pos = jnp.min(jnp.where(work == mx, row, n), axis=0, keepdims=True)
        work = jnp.where(row == pos, -jnp.inf, work)
        vals.append(mx)
        idxs.append(pos)
    return jnp.concatenate(vals, axis=0), jnp.concatenate(idxs, axis=0)


def _peer_route_kernel(x_ref, wq_ref, keys_ref, idx_ref, gt_ref, q_s, idx_s, g_s, *, tq):
    bf16 = jnp.bfloat16
    k = PEER_TOPK
    half = PEER_D_KEY // 2
    q = jnp.dot(x_ref[...].astype(bf16), wq_ref[...], preferred_element_type=jnp.float32)
    for c in range(2 * PEER_HEADS):
        q_s[c] = q[:, c * half:(c + 1) * half].astype(bf16)

    def head_body(h, carry):
        tops = []
        for p in range(2):
            scores = _nt_dot(keys_ref[p], q_s[2 * h + p])
            tops.append(_topk_rows(scores, k))
        (v0, i0), (v1, i1) = tops
        cand = jnp.concatenate([v0[a:a + 1, :] + v1 for a in range(k)], axis=0)
        cand_idx = jnp.concatenate([i0[a:a + 1, :] * PEER_N_KEYS + i1 for a in range(k)], axis=0)
        flat = lax.broadcasted_iota(jnp.int32, cand.shape, 0)
        top_s, experts = [], []
        for _ in range(k):
            mx = jnp.max(cand, axis=0, keepdims=True)
            pos = jnp.min(jnp.where(cand == mx, flat, k * k), axis=0, keepdims=True)
            pick = flat == pos
            experts.append(jnp.max(jnp.where(pick, cand_idx, -1), axis=0, keepdims=True))
            cand = jnp.where(pick, -jnp.inf, cand)
            top_s.append(mx)
        top_s = jnp.concatenate(top_s, axis=0)
        ex = jnp.exp(top_s - jnp.max(top_s, axis=0, keepdims=True))
        r0 = pl.multiple_of(h * k, k)
        g_s[pl.ds(r0, k), :] = ex / jnp.sum(ex, axis=0, keepdims=True)
        idx_s[pl.ds(r0, k), :] = jnp.concatenate(experts, axis=0)
        return carry
    lax.fori_loop(0, PEER_HEADS, head_body, 0)

    idx_ref[...] = jnp.transpose(idx_s[...])
    gt_ref[0] = g_s[...]


def peer_route(h2, w_q, sub_keys):
    t, d = h2.shape
    tq = PEER_BLOCK
    n_sel = PEER_HEADS * PEER_TOPK
    half = PEER_D_KEY // 2
    return pl.pallas_call(
        functools.partial(_peer_route_kernel, tq=tq),
        grid=(t // tq,),
        in_specs=[pl.BlockSpec((tq, d), lambda i: (i, 0)),
                  pl.BlockSpec((d, PEER_HEADS * PEER_D_KEY), lambda i: (0, 0)),
                  pl.BlockSpec((2, PEER_N_KEYS, half), lambda i: (0, 0, 0))],
        out_specs=[pl.BlockSpec((tq, n_sel), lambda i: (i, 0)),
                   pl.BlockSpec((1, n_sel, tq), lambda i: (i, 0, 0))],
        out_shape=[jax.ShapeDtypeStruct((t, n_sel), jnp.int32),
                   jax.ShapeDtypeStruct((t // tq, n_sel, tq), jnp.float32)],
        scratch_shapes=[pltpu.VMEM((2 * PEER_HEADS, tq, half), jnp.bfloat16),
                        pltpu.VMEM((n_sel, tq), jnp.int32),
                        pltpu.VMEM((n_sel, tq), jnp.float32)],
        compiler_params=pltpu.CompilerParams(dimension_semantics=("arbitrary",),
                                             vmem_limit_bytes=32 << 20),
        name="peer_route",
    )(h2, w_q.astype(jnp.bfloat16), sub_keys.astype(jnp.bfloat16))


NSA_KV_TILE = 256


def _nt_dot(a, b):
    return lax.dot_general(a, b, (((1,), (1,)), ((), ())), preferred_element_type=jnp.float32)


def _nsa_kernel(qn_ref, qr_ref, kc_ref, vc_ref, ov_ref, ks_ref, vs_ref, kw_ref, vw_ref, gl_ref,
                o_ref, m_s, l_s, acc_s, out_s, *, n_heads, qb, n_cmp_pad, n_slc):
    j = pl.program_id(1)
    s0 = j * qb
    hq = n_heads * qb
    bf16 = jnp.bfloat16
    scale = HEAD_DIM ** -0.5
    q_pos = s0 + lax.broadcasted_iota(jnp.int32, (qb, 1), 0)

    gates = jax.nn.sigmoid(gl_ref[0])

    def gate_col(br):
        return jnp.concatenate([gates[:, 3 * h + br:3 * h + br + 1] for h in range(n_heads)], axis=0)

    def masked_update(logits, mask, v_tile):
        kt = logits.shape[-1]
        lg = jnp.where(mask[None], logits.reshape(n_heads, qb, kt), NEG_INF)
        m_old = m_s[...].reshape(n_heads, qb, 1)
        m_new = jnp.maximum(m_old, jnp.max(lg, axis=-1, keepdims=True))
        p = jnp.where(mask[None], jnp.exp(lg - m_new), 0.0)
        alpha = jnp.exp(m_old - m_new)
        l_s[...] = (alpha * l_s[...].reshape(n_heads, qb, 1)
                    + jnp.sum(p, axis=-1, keepdims=True)).reshape(hq, 1)
        pv = jnp.dot(p.reshape(hq, kt).astype(bf16), v_tile, preferred_element_type=jnp.float32)
        acc_s[...] = alpha.reshape(hq, 1) * acc_s[...] + pv
        m_s[...] = m_new.reshape(hq, 1)

    def reset_state():
        m_s[...] = jnp.full((hq, 1), NEG_INF, jnp.float32)
        l_s[...] = jnp.zeros((hq, 1), jnp.float32)
        acc_s[...] = jnp.zeros((hq, HEAD_DIM), jnp.float32)

    def normalized():
        return acc_s[...] / jnp.maximum(l_s[...], 1e-30)

    qn = (qn_ref[0].reshape(hq, HEAD_DIM) * scale).astype(bf16)
    lc = _nt_dot(qn, kc_ref[0]).reshape(n_heads, qb, n_cmp_pad)
    c_end = (lax.broadcasted_iota(jnp.int32, (qb, n_cmp_pad), 1) * NSA_CMP_STRIDE
             + (NSA_CMP_BLOCK - 1))
    c_mask = c_end <= q_pos
    lc = jnp.where(c_mask[None], lc, NEG_INF)
    mc = jnp.max(lc, axis=-1, keepdims=True)
    ec = jnp.where(c_mask[None], jnp.exp(lc - mc), 0.0)
    pc = ec / jnp.maximum(jnp.sum(ec, axis=-1, keepdims=True), 1e-30)
    o_c = jnp.dot(pc.reshape(hq, n_cmp_pad).astype(bf16), vc_ref[0],
                  preferred_element_type=jnp.float32)
    out_s[...] = gate_col(0) * o_c

    p_sum = jnp.sum(pc, axis=0)
    p_hi = p_sum.astype(bf16)
    p_lo = (p_sum - p_hi.astype(jnp.float32)).astype(bf16)
    imp = (jnp.dot(p_hi, ov_ref[...], preferred_element_type=jnp.float32)
           + jnp.dot(p_lo, ov_ref[...], preferred_element_type=jnp.float32))
    blk = lax.broadcasted_iota(jnp.int32, (qb, n_slc), 1)
    cur = q_pos // NSA_SLC_BLOCK
    forced = (blk == 0) | (blk == cur) | (blk == cur - 1)
    imp = jnp.where(forced, NSA_FORCE_SCORE, imp)
    valid = blk <= cur
    work = jnp.where(valid, imp, NEG_INF)
    sel = jnp.zeros((qb, n_slc), jnp.bool_)
    for _ in range(min(NSA_TOP_N, n_slc)):
        mx = jnp.max(work, axis=-1, keepdims=True)
        first = jnp.min(jnp.where(work == mx, blk, n_slc), axis=-1, keepdims=True)
        pick = blk == first
        sel = sel | pick
        work = jnp.where(pick, -jnp.inf, work)
    sel_bf = jnp.where(sel & valid, 1.0, 0.0).astype(bf16)

    qr = (qr_ref[0].reshape(hq, HEAD_DIM) * scale).astype(bf16)
    reset_state()
    n_win_tiles = NSA_WINDOW // qb + 1
    w_first = jnp.maximum(0, n_win_tiles - 1 - j)

    def win_body(w, c):
        k0 = pl.multiple_of(s0 - NSA_WINDOW + w * qb, qb)
        k_pos = k0 + lax.broadcasted_iota(jnp.int32, (qb, qb), 1)
        dist = q_pos - k_pos
        mask = (dist >= 0) & (dist < NSA_WINDOW)
        masked_update(_nt_dot(qr, kw_ref[0, pl.ds(k0, qb), :]), mask, vw_ref[0, pl.ds(k0, qb), :])
        return c
    lax.fori_loop(w_first, n_win_tiles, win_body, 0)
    out_s[...] += gate_col(2) * normalized()

    reset_state()
    kt = NSA_KV_TILE
    blocks_per_tile = kt // NSA_SLC_BLOCK
    n_tiles = (s0 + qb - 1) // kt + 1

    def sel_body(t, c):
        k0 = pl.multiple_of(t * kt, kt)
        k_idx = lax.broadcasted_iota(jnp.int32, (n_slc, kt), 1)
        expand = (lax.broadcasted_iota(jnp.int32, (n_slc, kt), 0)
                  == t * blocks_per_tile + k_idx // NSA_SLC_BLOCK)
        picked = jnp.dot(sel_bf, jnp.where(expand, 1.0, 0.0).astype(bf16),
                         preferred_element_type=jnp.float32)
        k_pos = k0 + lax.broadcasted_iota(jnp.int32, (qb, kt), 1)
        mask = (picked > 0.5) & (k_pos <= q_pos)
        masked_update(_nt_dot(qr, ks_ref[0, pl.ds(k0, kt), :]), mask, vs_ref[0, pl.ds(k0, kt), :])
        return c
    lax.fori_loop(0, n_tiles, sel_body, 0)
    out_s[...] += gate_col(1) * normalized()

    o_ref[0] = out_s[...].reshape(n_heads, qb, HEAD_DIM)


def nsa_attention_pallas(qn, qr, kc, vc, overlap, ks, vs, kw, vw, gl):
    b, h, s, dh = qn.shape
    qb = Q_BLOCK
    n_cmp_pad = kc.shape[1]
    n_slc = overlap.shape[1]
    kern = functools.partial(_nsa_kernel, n_heads=h, qb=qb, n_cmp_pad=n_cmp_pad, n_slc=n_slc)
    q_spec = pl.BlockSpec((1, h, qb, dh), lambda i, j: (i, 0, j, 0))
    cmp_spec = pl.BlockSpec((1, n_cmp_pad, dh), lambda i, j: (i, 0, 0))
    kv_spec = pl.BlockSpec((1, s, dh), lambda i, j: (i, 0, 0))
    return pl.pallas_call(
        kern,
        grid=(b, s // qb),
        in_specs=[q_spec, q_spec, cmp_spec, cmp_spec,
                  pl.BlockSpec((n_cmp_pad, n_slc), lambda i, j: (0, 0)),
                  kv_spec, kv_spec, kv_spec, kv_spec,
                  pl.BlockSpec((1, qb, 3 * h), lambda i, j: (i, j, 0))],
        out_specs=q_spec,
        out_shape=jax.ShapeDtypeStruct((b, h, s, dh), jnp.float32),
        scratch_shapes=[pltpu.VMEM((h * qb, 1), jnp.float32),
                        pltpu.VMEM((h * qb, 1), jnp.float32),
                        pltpu.VMEM((h * qb, dh), jnp.float32),
                        pltpu.VMEM((h * qb, dh), jnp.float32)],
        compiler_params=pltpu.CompilerParams(
            dimension_semantics=("arbitrary", "arbitrary"),
            vmem_limit_bytes=40 << 20),
        name="nsa_attention",
    )(qn, qr, kc, vc, overlap, ks, vs, kw, vw, gl)


def _dilated_kernel(q_ref, k_ref, v_ref, o_ref, m_s, l_s, acc_s, *, qb):
    j = pl.program_id(1)
    s0 = j * qb
    bf16 = jnp.bfloat16
    scale = HEAD_DIM ** -0.5
    q_pos = s0 + lax.broadcasted_iota(jnp.int32, (qb, 1), 0)
    lane = lax.broadcasted_iota(jnp.int32, (qb, LANES), 1)
    first_head = lane < HEAD_DIM
    rows = DIL_HEADS_PER_GROUP * qb

    outs, lses = [], []
    for g, (window, dil) in enumerate(DIL_PAIRS):
        c0 = g * LANES
        q = q_ref[0, :, pl.ds(c0, LANES)] * scale
        q2 = jnp.concatenate([jnp.where(first_head, q, 0.0),
                              jnp.where(first_head, 0.0, q)], axis=0).astype(bf16)
        m_s[...] = jnp.full((rows, 1), NEG_INF, jnp.float32)
        l_s[...] = jnp.zeros((rows, 1), jnp.float32)
        acc_s[...] = jnp.zeros((rows, LANES), jnp.float32)
        n_tiles = window // qb + 1

        def body(w, c, c0=c0, window=window, dil=dil, q2=q2, n_tiles=n_tiles):
            k0 = pl.multiple_of(s0 - (n_tiles - 1 - w) * qb, qb)
            k_pos = k0 + lax.broadcasted_iota(jnp.int32, (qb, qb), 1)
            dist = q_pos - k_pos
            mask = (dist >= 0) & (dist <= window) & ((dist & (dil - 1)) == 0)
            logits = _nt_dot(q2, k_ref[0, pl.ds(k0, qb), pl.ds(c0, LANES)])
            lg = jnp.where(mask[None], logits.reshape(DIL_HEADS_PER_GROUP, qb, qb), NEG_INF)
            m_old = m_s[...].reshape(DIL_HEADS_PER_GROUP, qb, 1)
            m_new = jnp.maximum(m_old, jnp.max(lg, axis=-1, keepdims=True))
            p = jnp.where(mask[None], jnp.exp(lg - m_new), 0.0)
            alpha = jnp.exp(m_old - m_new)
            l_s[...] = (alpha * l_s[...].reshape(DIL_HEADS_PER_GROUP, qb, 1)
                        + jnp.sum(p, axis=-1, keepdims=True)).reshape(rows, 1)
            pv = jnp.dot(p.reshape(rows, qb).astype(bf16), v_ref[0, pl.ds(k0, qb), pl.ds(c0, LANES)],
                         preferred_element_type=jnp.float32)
            acc_s[...] = alpha.reshape(rows, 1) * acc_s[...] + pv
            m_s[...] = m_new.reshape(rows, 1)
            return c
        lax.fori_loop(jnp.maximum(0, n_tiles - 1 - j), n_tiles, body, 0)

        o2 = acc_s[...] / l_s[...]
        lse2 = m_s[...] + jnp.log(l_s[...])
        outs.append(jnp.where(first_head, o2[:qb], o2[qb:]))
        lses.append(jnp.where(first_head, lse2[:qb], lse2[qb:]))

    lse_max = functools.reduce(jnp.maximum, lses)
    ws = [jnp.exp(x - lse_max) for x in lses]
    den = functools.reduce(lambda a, b: a + b, ws)
    num = functools.reduce(lambda a, b: a + b, [w * o for w, o in zip(ws, outs)])
    o_ref[0] = num / den


def dilated_attention_pallas(q, k, v):
    b, s, w = q.shape
    qb = Q_BLOCK
    rows = DIL_HEADS_PER_GROUP * qb
    return pl.pallas_call(
        functools.partial(_dilated_kernel, qb=qb),
        grid=(b, s // qb),
        in_specs=[pl.BlockSpec((1, qb, w), lambda i, j: (i, j, 0)),
                  pl.BlockSpec((1, s, w), lambda i, j: (i, 0, 0)),
                  pl.BlockSpec((1, s, w), lambda i, j: (i, 0, 0))],
        out_specs=pl.BlockSpec((1, qb, DIL_OUT_W), lambda i, j: (i, j, 0)),
        out_shape=jax.ShapeDtypeStruct((b, s, DIL_OUT_W), jnp.float32),
        scratch_shapes=[pltpu.VMEM((rows, 1), jnp.float32),
                        pltpu.VMEM((rows, 1), jnp.float32),
                        pltpu.VMEM((rows, LANES), jnp.float32)],
        compiler_params=pltpu.CompilerParams(
            dimension_semantics=("arbitrary", "arbitrary"),
            vmem_limit_bytes=48 << 20),
        name="dilated_attention",
    )(q, k, v)


def rms_norm(x, g):
    xf = x.astype(jnp.float32)
    y = xf * lax.rsqrt(jnp.mean(xf * xf, axis=-1, keepdims=True) + NORM_EPS)
    return (y * g.astype(jnp.float32)).astype(x.dtype)


def rope(x, pos):
    half = x.shape[-1] // 2
    freqs = ROPE_THETA ** (-jnp.arange(half, dtype=jnp.float32) / half)
    ang = pos.astype(jnp.float32)[:, None] * freqs[None, :]
    cos = jnp.cos(ang)[None, :, None, :]
    sin = jnp.sin(ang)[None, :, None, :]
    xf = x.astype(jnp.float32)
    x1, x2 = xf[..., :half], xf[..., half:]
    return jnp.concatenate([x1 * cos - x2 * sin, x2 * cos + x1 * sin], axis=-1).astype(x.dtype)


def _unblock(y):
    y = jnp.moveaxis(y, 0, 1)
    return y.reshape((y.shape[0], y.shape[1] * y.shape[2]) + y.shape[3:])


def masked_softmax(logits, mask):
    logits = jnp.where(mask, logits, NEG_INF)
    m = jnp.max(logits, axis=-1, keepdims=True)
    e = jnp.where(mask, jnp.exp(logits - m), 0.0)
    return e / jnp.maximum(jnp.sum(e, axis=-1, keepdims=True), 1e-30)


def fox_attention(q, k, v, f_logit, b_f):
    B, S, H, dh = q.shape
    log_f = jax.nn.log_sigmoid(f_logit.astype(jnp.float32) + b_f.astype(jnp.float32))
    cum = jnp.transpose(jnp.cumsum(log_f, axis=1), (0, 2, 1))
    scale = dh ** -0.5
    k_pos = jnp.arange(S)

    def block(i):
        s0 = i * Q_BLOCK
        qb = lax.dynamic_slice_in_dim(q, s0, Q_BLOCK, axis=1)
        cq = lax.dynamic_slice_in_dim(cum, s0, Q_BLOCK, axis=2)
        logits = jnp.einsum('bqhd,bkhd->bhqk', qb, k).astype(jnp.float32) * scale
        logits = logits + cq[..., None] - cum[:, :, None, :]
        q_pos = s0 + jnp.arange(Q_BLOCK)
        logits = jnp.where(k_pos[None, :] <= q_pos[:, None], logits, NEG_INF)
        p = jax.nn.softmax(logits, axis=-1)
        return jnp.einsum('bhqk,bkhd->bqhd', p.astype(v.dtype), v)

    return _unblock(lax.map(block, jnp.arange(S // Q_BLOCK)))


def rg_lru_branch(x_in, gate, conv_w, conv_b, w_a, b_a, w_x, b_x, lam):
    B, S, C = x_in.shape
    xc = lax.conv_general_dilated(x_in, conv_w[:, None, :], window_strides=(1,),
                                  padding=[(LRU_CONV - 1, 0)],
                                  dimension_numbers=('NWC', 'WIO', 'NWC'),
                                  feature_group_count=C) + conv_b
    xr = xc.reshape(B, S, LRU_BLOCKS, C // LRU_BLOCKS)
    r = jax.nn.sigmoid(jnp.einsum('bsnc,ncd->bsnd', xr, w_a).reshape(B, S, C) + b_a)
    i_g = jax.nn.sigmoid(jnp.einsum('bsnc,ncd->bsnd', xr, w_x).reshape(B, S, C) + b_x)
    log_a = -LRU_C * r.astype(jnp.float32) * jax.nn.softplus(-lam.astype(jnp.float32))
    a = jnp.exp(log_a)
    u = jnp.sqrt(-jnp.expm1(2.0 * log_a)) * (i_g * xc).astype(jnp.float32)

    def combine(left, right):
        a_l, b_l = left
        a_r, b_r = right
        return a_l * a_r, a_r * b_l + b_r

    _, h = lax.associative_scan(combine, (a, u), axis=1)
    return h.astype(x_in.dtype) * jax.nn.gelu(gate)


def dilated_attention(q, k, v):
    B, S, H, dh = q.shape
    G, Hg = DIL_GROUPS, DIL_HEADS_PER_GROUP
    qg = q.reshape(B, S, G, Hg, dh)
    kg = k.reshape(B, S, G, Hg, dh)
    vg = v.reshape(B, S, G, Hg, dh)
    dil = jnp.array([d for _, d in DIL_PAIRS], dtype=jnp.int32)
    steps = jnp.arange(DIL_KEYS)
    g_idx = jnp.arange(G)[:, None, None]
    scale = dh ** -0.5

    def block(i):
        s0 = i * Q_BLOCK
        q_pos = s0 + jnp.arange(Q_BLOCK)
        k_pos = q_pos[None, :, None] - dil[:, None, None] * steps[None, None, :]
        valid = k_pos >= 0
        k_idx = jnp.maximum(k_pos, 0)
        kb = kg[:, k_idx, g_idx]
        vb = vg[:, k_idx, g_idx]
        qb = lax.dynamic_slice_in_dim(qg, s0, Q_BLOCK, axis=1)
        logits = jnp.einsum('bqghd,bgqkhd->bgqhk', qb, kb).astype(jnp.float32) * scale
        logits = jnp.where(valid[None, :, :, None, :], logits, NEG_INF)
        m = jnp.max(logits, axis=-1, keepdims=True)
        e = jnp.exp(logits - m)
        l = jnp.sum(e, axis=-1)
        o = jnp.einsum('bgqhk,bgqkhd->bgqhd', e, vb.astype(jnp.float32)) / l[..., None]
        lse = m[..., 0] + jnp.log(l)
        w = jax.nn.softmax(lse, axis=1)
        return jnp.einsum('bgqh,bgqhd->bqhd', w, o).astype(v.dtype)

    return _unblock(lax.map(block, jnp.arange(S // Q_BLOCK)))


def nsa_attention(q, k_cmp, v_cmp, k_slc, v_slc, k_win, v_win, gate_logits,
                  pe_k, pe_v, w1_k, w2_k, w1_v, w2_v):
    B, S, H, dh = q.shape
    Hkv = k_cmp.shape[2]
    G = H // Hkv
    pos = jnp.arange(S)
    q_rot = rope(q, pos)
    k_slc = rope(k_slc, pos)
    k_win = rope(k_win, pos)
    gates = jax.nn.sigmoid(gate_logits).reshape(B, S, H, 3)
    scale = dh ** -0.5

    n_cmp = (S - NSA_CMP_BLOCK) // NSA_CMP_STRIDE + 1
    cmp_idx = jnp.arange(n_cmp)[:, None] * NSA_CMP_STRIDE + jnp.arange(NSA_CMP_BLOCK)[None, :]

    def compress(kv, pe, w1, w2):
        blocks = kv[:, cmp_idx] + pe[None, None, :, None, :]
        flat = jnp.moveaxis(blocks, 3, 2).reshape(B, n_cmp, Hkv, NSA_CMP_BLOCK * dh)
        return jax.nn.gelu(flat @ w1) @ w2

    kc = compress(k_cmp, pe_k, w1_k, w2_k)
    vc = compress(v_cmp, pe_v, w1_v, w2_v)
    cmp_start = jnp.arange(n_cmp) * NSA_CMP_STRIDE
    cmp_end = cmp_start + NSA_CMP_BLOCK - 1

    n_slc = S // NSA_SLC_BLOCK
    top_n = min(NSA_TOP_N, n_slc)
    slc_start = jnp.arange(n_slc) * NSA_SLC_BLOCK
    overlap = ((cmp_start[:, None] < slc_start[None, :] + NSA_SLC_BLOCK)
               & (cmp_start[:, None] + NSA_CMP_BLOCK > slc_start[None, :])).astype(jnp.float32)
    ks_blocks = jnp.transpose(k_slc.reshape(B, n_slc, NSA_SLC_BLOCK, Hkv, dh), (0, 3, 1, 2, 4))
    vs_blocks = jnp.transpose(v_slc.reshape(B, n_slc, NSA_SLC_BLOCK, Hkv, dh), (0, 3, 1, 2, 4))
    pad = ((0, 0), (NSA_WINDOW, 0), (0, 0), (0, 0))
    k_win_pad = jnp.pad(k_win, pad)
    v_win_pad = jnp.pad(v_win, pad)
    b_idx = jnp.arange(B)[:, None, None, None]
    h_idx = jnp.arange(Hkv)[None, :, None, None]
    blk_ids = jnp.arange(n_slc)

    def block(i):
        s0 = i * Q_BLOCK
        q_pos = s0 + jnp.arange(Q_BLOCK)
        qb = lax.dynamic_slice_in_dim(q, s0, Q_BLOCK, axis=1).reshape(B, Q_BLOCK, Hkv, G, dh)
        qrb = lax.dynamic_slice_in_dim(q_rot, s0, Q_BLOCK, axis=1).reshape(B, Q_BLOCK, Hkv, G, dh)
        lc = jnp.einsum('bqkgd,bckd->bkgqc', qb, kc).astype(jnp.float32) * scale
        p_c = masked_softmax(lc, (cmp_end[None, :] <= q_pos[:, None])[None, None, None])
        o_c = jnp.einsum('bkgqc,bckd->bqkgd', p_c, vc.astype(jnp.float32))
        imp = jnp.einsum('bkgqc,cn->bkqn', p_c, overlap)
        cur = q_pos // NSA_SLC_BLOCK
        forced = ((blk_ids[None, :] == 0) | (blk_ids[None, :] == cur[:, None])
                  | (blk_ids[None, :] == cur[:, None] - 1))
        imp = jnp.where(forced, NSA_FORCE_SCORE, imp)
        imp = jnp.where(blk_ids[None, :] <= cur[:, None], imp, NEG_INF)
        top_val, top_idx = lax.top_k(imp, top_n)
        sel_ok = top_val > 0.5 * NEG_INF
        ksel = ks_blocks[b_idx, h_idx, top_idx]
        vsel = vs_blocks[b_idx, h_idx, top_idx]
        sel_pos = top_idx[..., None] * NSA_SLC_BLOCK + jnp.arange(NSA_SLC_BLOCK)
        sel_mask = sel_ok[..., None] & (sel_pos <= q_pos[None, None, :, None, None])
        n_sel_keys = top_n * NSA_SLC_BLOCK
        ls = jnp.einsum('bqkgd,bkqnld->bkgqnl', qrb, ksel).astype(jnp.float32) * scale
        ls = ls.reshape(B, Hkv, G, Q_BLOCK, n_sel_keys)
        p_s = masked_softmax(ls, sel_mask.reshape(B, Hkv, 1, Q_BLOCK, n_sel_keys))
        o_s = jnp.einsum('bkgqm,bkqmd->bqkgd', p_s,
                         vsel.reshape(B, Hkv, Q_BLOCK, n_sel_keys, dh).astype(jnp.float32))
        kw = lax.dynamic_slice_in_dim(k_win_pad, s0, NSA_WINDOW + Q_BLOCK, axis=1)
        vw = lax.dynamic_slice_in_dim(v_win_pad, s0, NSA_WINDOW + Q_BLOCK, axis=1)
        w_pos = s0 - NSA_WINDOW + jnp.arange(NSA_WINDOW + Q_BLOCK)
        dist = q_pos[:, None] - w_pos[None, :]
        w_mask = (dist >= 0) & (dist < NSA_WINDOW) & (w_pos[None, :] >= 0)
        lw = jnp.einsum('bqkgd,bwkd->bkgqw', qrb, kw).astype(jnp.float32) * scale
        p_w = masked_softmax(lw, w_mask)
        o_w = jnp.einsum('bkgqw,bwkd->bqkgd', p_w, vw.astype(jnp.float32))
        gb = lax.dynamic_slice_in_dim(gates, s0, Q_BLOCK, axis=1).reshape(B, Q_BLOCK, Hkv, G, 3)
        gb = gb.astype(jnp.float32)
        out = gb[..., 0:1] * o_c + gb[..., 1:2] * o_s + gb[..., 2:3] * o_w
        return out.reshape(B, Q_BLOCK, H, dh).astype(q.dtype)

    return _unblock(lax.map(block, jnp.arange(S // Q_BLOCK)))


def nsa_branch(nq, nkc, nvc, nks, nvs, nkw, nvw, ng, pe_k, pe_v, w1_k, w2_k, w1_v, w2_v):
    B, S, _ = nq.shape
    dh = HEAD_DIM
    bf16 = jnp.bfloat16
    pos = jnp.arange(S)
    q4 = nq.reshape(B, S, NSA_HEADS, dh)
    qn = jnp.transpose(q4, (0, 2, 1, 3))
    qr = jnp.transpose(rope(q4, pos), (0, 2, 1, 3))
    ks = rope(nks.reshape(B, S, 1, dh), pos).reshape(B, S, dh).astype(bf16)
    kw = rope(nkw.reshape(B, S, 1, dh), pos).reshape(B, S, dh).astype(bf16)

    n_cmp = (S - NSA_CMP_BLOCK) // NSA_CMP_STRIDE + 1
    n_cmp_pad = S // NSA_CMP_STRIDE
    n_slc = S // NSA_SLC_BLOCK

    def compress(kv, pe, w1, w2):
        chunks = kv.reshape(B, n_cmp_pad, NSA_CMP_STRIDE * dh)
        flat = jnp.concatenate([chunks[:, :-1], chunks[:, 1:]], axis=-1) + pe.reshape(-1)
        out = jax.nn.gelu(flat @ w1) @ w2
        return jnp.pad(out, ((0, 0), (0, n_cmp_pad - n_cmp), (0, 0))).astype(bf16)

    kc = compress(nkc, pe_k, w1_k, w2_k)
    vc = compress(nvc, pe_v, w1_v, w2_v)
    cmp_start = jnp.arange(n_cmp_pad) * NSA_CMP_STRIDE
    slc_start = jnp.arange(n_slc) * NSA_SLC_BLOCK
    overlap = ((cmp_start[:, None] < slc_start[None, :] + NSA_SLC_BLOCK)
               & (cmp_start[:, None] + NSA_CMP_BLOCK > slc_start[None, :])
               & (jnp.arange(n_cmp_pad)[:, None] < n_cmp)).astype(bf16)
    out = nsa_attention_pallas(qn, qr, kc, vc, overlap, ks, nvs.astype(bf16), kw,
                               nvw.astype(bf16), ng)
    return jnp.transpose(out, (0, 2, 1, 3)).reshape(B, S, NSA_W)


def peer_ffn(x_res, h2, w_q, sub_keys, u, v):
    expert, gt = peer_route(h2, w_q, sub_keys)
    uv = jnp.concatenate([u, v], axis=1)
    return peer_mix(h2, x_res, expert, gt, uv)


def kernel(x, norm1_g, w_in, fox_b_f, lru_conv_w, lru_conv_b, lru_w_a, lru_b_a, lru_w_x, lru_b_x, lru_lambda, nsa_pe_k, nsa_pe_v, nsa_w1_k, nsa_w2_k, nsa_w1_v, nsa_w2_v, w_br_fox, w_br_lru, w_br_dil, w_br_nsa, w_out, norm2_g, peer_w_q, peer_sub_keys, peer_u, peer_v, final_g):
    B, S, D = x.shape
    pos = jnp.arange(S)
    split_points = np.cumsum(IN_SPLITS)[:-1].tolist()

    def heads(t, n):
        return t.reshape(B, S, n, HEAD_DIM)

    for l in range(DEPTH):
        h = rms_norm(x, norm1_g[l])
        proj = h @ w_in[l]
        (fq, fk, fv, ff, lx, lg, dq, dk, dv, nq, nkc, nvc, nks, nvs, nkw, nvw, ng, mg) = \
            jnp.split(proj, split_points, axis=-1)

        o_fox = fox_attention(heads(fq, FOX_HEADS), heads(fk, FOX_HEADS), heads(fv, FOX_HEADS),
                              ff, fox_b_f[l]).reshape(B, S, FOX_W)
        o_lru = rg_lru_branch(lx, lg, lru_conv_w[l], lru_conv_b[l], lru_w_a[l], lru_b_a[l],
                              lru_w_x[l], lru_b_x[l], lru_lambda[l])
        o_dil = dilated_attention_pallas(
            rope(heads(dq, DIL_HEADS), pos).reshape(B, S, DIL_W),
            rope(heads(dk, DIL_HEADS), pos).reshape(B, S, DIL_W).astype(jnp.bfloat16),
            dv.astype(jnp.bfloat16))
        o_nsa = nsa_branch(nq, nkc, nvc, nks, nvs, nkw, nvw, ng,
                           nsa_pe_k[l], nsa_pe_v[l], nsa_w1_k[l], nsa_w2_k[l],
                           nsa_w1_v[l], nsa_w2_v[l])

        gates = jax.nn.sigmoid(mg).reshape(B, S, N_BRANCHES, D)
        merged = (gates[:, :, 0] * (o_fox @ w_br_fox[l])
                  + gates[:, :, 1] * (o_lru @ w_br_lru[l])
                  + gates[:, :, 2] * (o_dil @ w_br_dil[l])
                  + gates[:, :, 3] * (o_nsa @ w_br_nsa[l]))
        x = x + merged @ w_out[l]

        h2 = rms_norm(x, norm2_g[l])
        x = peer_ffn(x.reshape(B * S, D), h2.reshape(B * S, D), peer_w_q[l], peer_sub_keys[l],
                     peer_u[l], peer_v[l]).reshape(B, S, D)

    return rms_norm_pallas(x.reshape(B * S, D), final_g).reshape(B, S, D)
```

```python
import functools

import jax
import jax.numpy as jnp
import numpy as np
from jax import lax
from jax.experimental import pallas as pl
from jax.experimental.pallas import tpu as pltpu

D_MODEL = 1024
DEPTH = 2
HEAD_DIM = 64
Q_BLOCK = 128
ROPE_THETA = 10000.0
NORM_EPS = 1e-6
NEG_INF = -1e30
N_BRANCHES = 4
FOX_HEADS = 4
FOX_W = FOX_HEADS * HEAD_DIM
LRU_WIDTH = 256
LRU_BLOCKS = 4
LRU_CONV = 4
LRU_C = 8.0
DIL_PAIRS = ((128, 1), (512, 4), (2048, 16))
DIL_GROUPS = len(DIL_PAIRS)
DIL_HEADS_PER_GROUP = 2
DIL_HEADS = DIL_GROUPS * DIL_HEADS_PER_GROUP
DIL_W = DIL_HEADS * HEAD_DIM
DIL_OUT_W = DIL_HEADS_PER_GROUP * HEAD_DIM
DIL_KEYS = DIL_PAIRS[0][0] // DIL_PAIRS[0][1] + 1
NSA_HEADS = 4
NSA_KV_HEADS = 1
NSA_W = NSA_HEADS * HEAD_DIM
NSA_KV_W = NSA_KV_HEADS * HEAD_DIM
NSA_CMP_BLOCK = 32
NSA_CMP_STRIDE = 16
NSA_SLC_BLOCK = 64
NSA_TOP_N = 16
NSA_WINDOW = 512
NSA_FORCE_SCORE = 1e6
PEER_HEADS = 8
PEER_N_KEYS = 128
PEER_TOPK = 16
PEER_D_KEY = 256
PEER_BLOCK = 128

IN_SPLITS = (FOX_W, FOX_W, FOX_W, FOX_HEADS,
             LRU_WIDTH, LRU_WIDTH,
             DIL_W, DIL_W, DIL_W,
             NSA_W, NSA_KV_W, NSA_KV_W, NSA_KV_W, NSA_KV_W, NSA_KV_W, NSA_KV_W, NSA_HEADS * 3,
             N_BRANCHES * D_MODEL)


def _rms_norm_kernel(x_ref, g_ref, o_ref):
    x = x_ref[...]
    ms = jnp.mean(x * x, axis=-1, keepdims=True)
    o_ref[...] = x * lax.rsqrt(ms + NORM_EPS) * g_ref[...]


def rms_norm_pallas(x, g, rows=512):
    t, d = x.shape
    return pl.pallas_call(
        _rms_norm_kernel,
        grid=(t // rows,),
        in_specs=[pl.BlockSpec((rows, d), lambda i: (i, 0)),
                  pl.BlockSpec((1, d), lambda i: (0, 0))],
        out_specs=pl.BlockSpec((rows, d), lambda i: (i, 0)),
        out_shape=jax.ShapeDtypeStruct((t, d), x.dtype),
        name="rms_norm",
    )(x, g.reshape(1, d))


def _gelu_tanh(x):
    return 0.5 * x * (1.0 + jnp.tanh(0.7978845608028654 * (x + 0.044715 * x * x * x)))


PEER_TOKENS_PER_STEP = 8
LANES = 128
SUBLANES = 8


def _peer_mix_kernel(idx_hbm, x_ref, gt_ref, res_ref, uv_hbm, o_ref,
                     idx_smem, buf, idx_sem, row_sem, *, tb, n_sel, d):
    i = pl.program_id(0)
    n = pl.num_programs(0)
    rows = tb * n_sel
    slot = i % 2

    def idx_copy(step, s):
        return pltpu.make_async_copy(idx_hbm.at[step], idx_smem.at[pl.ds(s * rows, rows)],
                                     idx_sem.at[s])

    def issue_rows(s):
        base = s * rows

        def body(jj, c):
            for k in range(SUBLANES):
                row = idx_smem[base + jj * SUBLANES + k]
                pltpu.make_async_copy(
                    uv_hbm.at[lax.shift_right_logical(row, 3), pl.ds(row & (SUBLANES - 1), 1)],
                    buf.at[s, jj, pl.ds(k, 1)], row_sem.at[s]).start()
            return c
        lax.fori_loop(0, rows // SUBLANES, body, 0)

    @pl.when(i == 0)
    def _():
        idx_copy(0, 0).start()
        idx_copy(0, 0).wait()
        issue_rows(0)

        @pl.when(n > 1)
        def _():
            idx_copy(1, 1).start()

    @pl.when(i + 1 < n)
    def _():
        idx_copy(i + 1, 1 - slot).wait()
        issue_rows(1 - slot)

    pltpu.make_async_copy(uv_hbm.at[pl.ds(0, rows // SUBLANES)], buf.at[slot],
                          row_sem.at[slot]).wait()

    @pl.when(i + 2 < n)
    def _():
        idx_copy(i + 2, slot).start()

    n_chunks = d // LANES
    groups = n_sel // SUBLANES

    def sel_rows(t, lane0):
        tile = buf[slot, pl.ds(t * groups, groups), :, pl.ds(lane0, LANES)]
        return tile.reshape(n_sel, LANES)

    for t in range(tb):
        acc = jnp.zeros((n_sel, LANES), jnp.float32)
        for c in range(n_chunks):
            xc = x_ref[pl.ds(t, 1), pl.ds(c * LANES, LANES)]
            acc = acc + sel_rows(t, c * LANES) * xc
        s = jnp.sum(acc, axis=-1, keepdims=True)
        w = gt_ref[0, :, pl.ds(t, 1)] * _gelu_tanh(s)
        wb = jnp.broadcast_to(w, (n_sel, LANES))
        for c in range(n_chunks):
            vc = sel_rows(t, d + c * LANES)
            oc = jnp.sum(wb * vc, axis=0, keepdims=True)
            o_ref[pl.ds(t, 1), pl.ds(c * LANES, LANES)] = (
                res_ref[pl.ds(t, 1), pl.ds(c * LANES, LANES)] + oc)


def peer_mix(x, res, idx, gt_blocks, uv):
    t, d = x.shape
    n_sel = idx.shape[1]
    tb = PEER_TOKENS_PER_STEP
    steps = t // tb
    idx_steps = idx.reshape(steps, tb * n_sel)
    nb, _, tq = gt_blocks.shape
    gt = jnp.transpose(gt_blocks.reshape(nb, n_sel, tq // tb, tb), (0, 2, 1, 3)).reshape(steps, n_sel, tb)
    kern = functools.partial(_peer_mix_kernel, tb=tb, n_sel=n_sel, d=d)
    return pl.pallas_call(
        kern,
        grid=(steps,),
        in_specs=[pl.BlockSpec(memory_space=pl.ANY),
                  pl.BlockSpec((tb, d), lambda i: (i, 0)),
                  pl.BlockSpec((1, n_sel, tb), lambda i: (i, 0, 0)),
                  pl.BlockSpec((tb, d), lambda i: (i, 0)),
                  pl.BlockSpec(memory_space=pl.ANY)],
        out_specs=pl.BlockSpec((tb, d), lambda i: (i, 0)),
        out_shape=jax.ShapeDtypeStruct((t, d), jnp.float32),
        scratch_shapes=[pltpu.SMEM((2 * tb * n_sel,), jnp.int32),
                        pltpu.VMEM((2, tb * n_sel // SUBLANES, SUBLANES, 2 * d), jnp.float32),
                        pltpu.SemaphoreType.DMA((2,)),
                        pltpu.SemaphoreType.DMA((2,))],
        compiler_params=pltpu.CompilerParams(
            dimension_semantics=("arbitrary",),
            vmem_limit_bytes=2 * tb * n_sel * 2 * d * 4 + (8 << 20)),
        name="peer_mix",
    )(idx_steps, x, gt, res, uv.reshape(uv.shape[0] // SUBLANES, SUBLANES, 2 * d))


def _topk_rows(work, k):
    n = work.shape[0]
    row = lax.broadcasted_iota(jnp.int32, work.shape, 0)
    vals, idxs = [], []
    for _ in range(k):
        mx = jnp.max(work, axis=0, keepdims=True)
        pos = jnp.min(jnp.where(work == mx, row, n), axis=0, keepdims=True)
        work = jnp.where(row == pos, -jnp.inf, work)
        vals.append(mx)
        idxs.append(pos)
    return jnp.concatenate(vals, axis=0), jnp.concatenate(idxs, axis=0)


def _peer_route_kernel(x_ref, wq_ref, keys_ref, idx_ref, gt_ref, q_s, idx_s, g_s, *, tq):
    bf16 = jnp.bfloat16
    k = PEER_TOPK
    half = PEER_D_KEY // 2
    q = jnp.dot(x_ref[...].astype(bf16), wq_ref[...], preferred_element_type=jnp.float32)
    for c in range(2 * PEER_HEADS):
        q_s[c] = q[:, c * half:(c + 1) * half].astype(bf16)

    def head_body(h, carry):
        tops = []
        for p in range(2):
            scores = _nt_dot(keys_ref[p], q_s[2 * h + p])
            tops.append(_topk_rows(scores, k))
        (v0, i0), (v1, i1) = tops
        cand = jnp.concatenate([v0[a:a + 1, :] + v1 for a in range(k)], axis=0)
        cand_idx = jnp.concatenate([i0[a:a + 1, :] * PEER_N_KEYS + i1 for a in range(k)], axis=0)
        flat = lax.broadcasted_iota(jnp.int32, cand.shape, 0)
        top_s, experts = [], []
        for _ in range(k):
            mx = jnp.max(cand, axis=0, keepdims=True)
            pos = jnp.min(jnp.where(cand == mx, flat, k * k), axis=0, keepdims=True)
            pick = flat == pos
            experts.append(jnp.max(jnp.where(pick, cand_idx, -1), axis=0, keepdims=True))
            cand = jnp.where(pick, -jnp.inf, cand)
            top_s.append(mx)
        top_s = jnp.concatenate(top_s, axis=0)
        ex = jnp.exp(top_s - jnp.max(top_s, axis=0, keepdims=True))
        r0 = pl.multiple_of(h * k, k)
        g_s[pl.ds(r0, k), :] = ex / jnp.sum(ex, axis=0, keepdims=True)
        idx_s[pl.ds(r0, k), :] = jnp.concatenate(experts, axis=0)
        return carry
    lax.fori_loop(0, PEER_HEADS, head_body, 0)

    idx_ref[...] = jnp.transpose(idx_s[...])
    gt_ref[0] = g_s[...]


def peer_route(h2, w_q, sub_keys):
    t, d = h2.shape
    tq = PEER_BLOCK
    n_sel = PEER_HEADS * PEER_TOPK
    half = PEER_D_KEY // 2
    return pl.pallas_call(
        functools.partial(_peer_route_kernel, tq=tq),
        grid=(t // tq,),
        in_specs=[pl.BlockSpec((tq, d), lambda i: (i, 0)),
                  pl.BlockSpec((d, PEER_HEADS * PEER_D_KEY), lambda i: (0, 0)),
                  pl.BlockSpec((2, PEER_N_KEYS, half), lambda i: (0, 0, 0))],
        out_specs=[pl.BlockSpec((tq, n_sel), lambda i: (i, 0)),
                   pl.BlockSpec((1, n_sel, tq), lambda i: (i, 0, 0))],
        out_shape=[jax.ShapeDtypeStruct((t, n_sel), jnp.int32),
                   jax.ShapeDtypeStruct((t // tq, n_sel, tq), jnp.float32)],
        scratch_shapes=[pltpu.VMEM((2 * PEER_HEADS, tq, half), jnp.bfloat16),
                        pltpu.VMEM((n_sel, tq), jnp.int32),
                        pltpu.VMEM((n_sel, tq), jnp.float32)],
        compiler_params=pltpu.CompilerParams(dimension_semantics=("arbitrary",),
                                             vmem_limit_bytes=32 << 20),
        name="peer_route",
    )(h2, w_q.astype(jnp.bfloat16), sub_keys.astype(jnp.bfloat16))


NSA_KV_TILE = 256


def _nt_dot(a, b):
    return lax.dot_general(a, b, (((1,), (1,)), ((), ())), preferred_element_type=jnp.float32)


def _nsa_kernel(qn_ref, qr_ref, kc_ref, vc_ref, ov_ref, ks_ref, vs_ref, kw_ref, vw_ref, gl_ref,
                o_ref, m_s, l_s, acc_s, out_s, *, n_heads, qb, n_cmp_pad, n_slc):
    j = pl.program_id(1)
    s0 = j * qb
    hq = n_heads * qb
    bf16 = jnp.bfloat16
    scale = HEAD_DIM ** -0.5
    q_pos = s0 + lax.broadcasted_iota(jnp.int32, (qb, 1), 0)

    gates = jax.nn.sigmoid(gl_ref[0])

    def gate_col(br):
        return jnp.concatenate([gates[:, 3 * h + br:3 * h + br + 1] for h in range(n_heads)], axis=0)

    def masked_update(logits, mask, v_tile):
        kt = logits.shape[-1]
        lg = jnp.where(mask[None], logits.reshape(n_heads, qb, kt), NEG_INF)
        m_old = m_s[...].reshape(n_heads, qb, 1)
        m_new = jnp.maximum(m_old, jnp.max(lg, axis=-1, keepdims=True))
        p = jnp.where(mask[None], jnp.exp(lg - m_new), 0.0)
        alpha = jnp.exp(m_old - m_new)
        l_s[...] = (alpha * l_s[...].reshape(n_heads, qb, 1)
                    + jnp.sum(p, axis=-1, keepdims=True)).reshape(hq, 1)
        pv = jnp.dot(p.reshape(hq, kt).astype(bf16), v_tile, preferred_element_type=jnp.float32)
        acc_s[...] = alpha.reshape(hq, 1) * acc_s[...] + pv
        m_s[...] = m_new.reshape(hq, 1)

    def reset_state():
        m_s[...] = jnp.full((hq, 1), NEG_INF, jnp.float32)
        l_s[...] = jnp.zeros((hq, 1), jnp.float32)
        acc_s[...] = jnp.zeros((hq, HEAD_DIM), jnp.float32)

    def normalized():
        return acc_s[...] / jnp.maximum(l_s[...], 1e-30)

    qn = (qn_ref[0].reshape(hq, HEAD_DIM) * scale).astype(bf16)
    lc = _nt_dot(qn, kc_ref[0]).reshape(n_heads, qb, n_cmp_pad)
    c_end = (lax.broadcasted_iota(jnp.int32, (qb, n_cmp_pad), 1) * NSA_CMP_STRIDE
             + (NSA_CMP_BLOCK - 1))
    c_mask = c_end <= q_pos
    lc = jnp.where(c_mask[None], lc, NEG_INF)
    mc = jnp.max(lc, axis=-1, keepdims=True)
    ec = jnp.where(c_mask[None], jnp.exp(lc - mc), 0.0)
    pc = ec / jnp.maximum(jnp.sum(ec, axis=-1, keepdims=True), 1e-30)
    o_c = jnp.dot(pc.reshape(hq, n_cmp_pad).astype(bf16), vc_ref[0],
                  preferred_element_type=jnp.float32)
    out_s[...] = gate_col(0) * o_c

    p_sum = jnp.sum(pc, axis=0)
    p_hi = p_sum.astype(bf16)
    p_lo = (p_sum - p_hi.astype(jnp.float32)).astype(bf16)
    imp = (jnp.dot(p_hi, ov_ref[...], preferred_element_type=jnp.float32)
           + jnp.dot(p_lo, ov_ref[...], preferred_element_type=jnp.float32))
    blk = lax.broadcasted_iota(jnp.int32, (qb, n_slc), 1)
    cur = q_pos // NSA_SLC_BLOCK
    forced = (blk == 0) | (blk == cur) | (blk == cur - 1)
    imp = jnp.where(forced, NSA_FORCE_SCORE, imp)
    valid = blk <= cur
    work = jnp.where(valid, imp, NEG_INF)
    sel = jnp.zeros((qb, n_slc), jnp.bool_)
    for _ in range(min(NSA_TOP_N, n_slc)):
        mx = jnp.max(work, axis=-1, keepdims=True)
        first = jnp.min(jnp.where(work == mx, blk, n_slc), axis=-1, keepdims=True)
        pick = blk == first
        sel = sel | pick
        work = jnp.where(pick, -jnp.inf, work)
    sel_bf = jnp.where(sel & valid, 1.0, 0.0).astype(bf16)

    qr = (qr_ref[0].reshape(hq, HEAD_DIM) * scale).astype(bf16)
    reset_state()
    n_win_tiles = NSA_WINDOW // qb + 1
    w_first = jnp.maximum(0, n_win_tiles - 1 - j)

    def win_body(w, c):
        k0 = pl.multiple_of(s0 - NSA_WINDOW + w * qb, qb)
        k_pos = k0 + lax.broadcasted_iota(jnp.int32, (qb, qb), 1)
        dist = q_pos - k_pos
        mask = (dist >= 0) & (dist < NSA_WINDOW)
        masked_update(_nt_dot(qr, kw_ref[0, pl.ds(k0, qb), :]), mask, vw_ref[0, pl.ds(k0, qb), :])
        return c
    lax.fori_loop(w_first, n_win_tiles, win_body, 0)
    out_s[...] += gate_col(2) * normalized()

    reset_state()
    kt = NSA_KV_TILE
    blocks_per_tile = kt // NSA_SLC_BLOCK
    n_tiles = (s0 + qb - 1) // kt + 1

    def sel_body(t, c):
        k0 = pl.multiple_of(t * kt, kt)
        k_idx = lax.broadcasted_iota(jnp.int32, (n_slc, kt), 1)
        expand = (lax.broadcasted_iota(jnp.int32, (n_slc, kt), 0)
                  == t * blocks_per_tile + k_idx // NSA_SLC_BLOCK)
        picked = jnp.dot(sel_bf, jnp.where(expand, 1.0, 0.0).astype(bf16),
                         preferred_element_type=jnp.float32)
        k_pos = k0 + lax.broadcasted_iota(jnp.int32, (qb, kt), 1)
        mask = (picked > 0.5) & (k_pos <= q_pos)
        masked_update(_nt_dot(qr, ks_ref[0, pl.ds(k0, kt), :]), mask, vs_ref[0, pl.ds(k0, kt), :])
        return c
    lax.fori_loop(0, n_tiles, sel_body, 0)
    out_s[...] += gate_col(1) * normalized()

    o_ref[0] = out_s[...].reshape(n_heads, qb, HEAD_DIM)


def nsa_attention_pallas(qn, qr, kc, vc, overlap, ks, vs, kw, vw, gl):
    b, h, s, dh = qn.shape
    qb = Q_BLOCK
    n_cmp_pad = kc.shape[1]
    n_slc = overlap.shape[1]
    kern = functools.partial(_nsa_kernel, n_heads=h, qb=qb, n_cmp_pad=n_cmp_pad, n_slc=n_slc)
    q_spec = pl.BlockSpec((1, h, qb, dh), lambda i, j: (i, 0, j, 0))
    cmp_spec = pl.BlockSpec((1, n_cmp_pad, dh), lambda i, j: (i, 0, 0))
    kv_spec = pl.BlockSpec((1, s, dh), lambda i, j: (i, 0, 0))
    return pl.pallas_call(
        kern,
        grid=(b, s // qb),
        in_specs=[q_spec, q_spec, cmp_spec, cmp_spec,
                  pl.BlockSpec((n_cmp_pad, n_slc), lambda i, j: (0, 0)),
                  kv_spec, kv_spec, kv_spec, kv_spec,
                  pl.BlockSpec((1, qb, 3 * h), lambda i, j: (i, j, 0))],
        out_specs=q_spec,
        out_shape=jax.ShapeDtypeStruct((b, h, s, dh), jnp.float32),
        scratch_shapes=[pltpu.VMEM((h * qb, 1), jnp.float32),
                        pltpu.VMEM((h * qb, 1), jnp.float32),
                        pltpu.VMEM((h * qb, dh), jnp.float32),
                        pltpu.VMEM((h * qb, dh), jnp.float32)],
        compiler_params=pltpu.CompilerParams(
            dimension_semantics=("arbitrary", "arbitrary"),
            vmem_limit_bytes=40 << 20),
        name="nsa_attention",
    )(qn, qr, kc, vc, overlap, ks, vs, kw, vw, gl)


def _dilated_kernel(q_ref, k_ref, v_ref, o_ref, m_s, l_s, acc_s, *, qb):
    j = pl.program_id(1)
    s0 = j * qb
    bf16 = jnp.bfloat16
    scale = HEAD_DIM ** -0.5
    q_pos = s0 + lax.broadcasted_iota(jnp.int32, (qb, 1), 0)
    lane = lax.broadcasted_iota(jnp.int32, (qb, LANES), 1)
    first_head = lane < HEAD_DIM
    rows = DIL_HEADS_PER_GROUP * qb

    outs, lses = [], []
    for g, (window, dil) in enumerate(DIL_PAIRS):
        c0 = g * LANES
        q = q_ref[0, :, pl.ds(c0, LANES)] * scale
        q2 = jnp.concatenate([jnp.where(first_head, q, 0.0),
                              jnp.where(first_head, 0.0, q)], axis=0).astype(bf16)
        m_s[...] = jnp.full((rows, 1), NEG_INF, jnp.float32)
        l_s[...] = jnp.zeros((rows, 1), jnp.float32)
        acc_s[...] = jnp.zeros((rows, LANES), jnp.float32)
        n_tiles = window // qb + 1

        def body(w, c, c0=c0, window=window, dil=dil, q2=q2, n_tiles=n_tiles):
            k0 = pl.multiple_of(s0 - (n_tiles - 1 - w) * qb, qb)
            k_pos = k0 + lax.broadcasted_iota(jnp.int32, (qb, qb), 1)
            dist = q_pos - k_pos
            mask = (dist >= 0) & (dist <= window) & ((dist & (dil - 1)) == 0)
            logits = _nt_dot(q2, k_ref[0, pl.ds(k0, qb), pl.ds(c0, LANES)])
            lg = jnp.where(mask[None], logits.reshape(DIL_HEADS_PER_GROUP, qb, qb), NEG_INF)
            m_old = m_s[...].reshape(DIL_HEADS_PER_GROUP, qb, 1)
            m_new = jnp.maximum(m_old, jnp.max(lg, axis=-1, keepdims=True))
            p = jnp.where(mask[None], jnp.exp(lg - m_new), 0.0)
            alpha = jnp.exp(m_old - m_new)
            l_s[...] = (alpha * l_s[...].reshape(DIL_HEADS_PER_GROUP, qb, 1)
                        + jnp.sum(p, axis=-1, keepdims=True)).reshape(rows, 1)
            pv = jnp.dot(p.reshape(rows, qb).astype(bf16), v_ref[0, pl.ds(k0, qb), pl.ds(c0, LANES)],
                         preferred_element_type=jnp.float32)
            acc_s[...] = alpha.reshape(rows, 1) * acc_s[...] + pv
            m_s[...] = m_new.reshape(rows, 1)
            return c
        lax.fori_loop(jnp.maximum(0, n_tiles - 1 - j), n_tiles, body, 0)

        o2 = acc_s[...] / l_s[...]
        lse2 = m_s[...] + jnp.log(l_s[...])
        outs.append(jnp.where(first_head, o2[:qb], o2[qb:]))
        lses.append(jnp.where(first_head, lse2[:qb], lse2[qb:]))

    lse_max = functools.reduce(jnp.maximum, lses)
    ws = [jnp.exp(x - lse_max) for x in lses]
    den = functools.reduce(lambda a, b: a + b, ws)
    num = functools.reduce(lambda a, b: a + b, [w * o for w, o in zip(ws, outs)])
    o_ref[0] = num / den


def dilated_attention_pallas(q, k, v):
    b, s, w = q.shape
    qb = Q_BLOCK
    rows = DIL_HEADS_PER_GROUP * qb
    return pl.pallas_call(
        functools.partial(_dilated_kernel, qb=qb),
        grid=(b, s // qb),
        in_specs=[pl.BlockSpec((1, qb, w), lambda i, j: (i, j, 0)),
                  pl.BlockSpec((1, s, w), lambda i, j: (i, 0, 0)),
                  pl.BlockSpec((1, s, w), lambda i, j: (i, 0, 0))],
        out_specs=pl.BlockSpec((1, qb, DIL_OUT_W), lambda i, j: (i, j, 0)),
        out_shape=jax.ShapeDtypeStruct((b, s, DIL_OUT_W), jnp.float32),
        scratch_shapes=[pltpu.VMEM((rows, 1), jnp.float32),
                        pltpu.VMEM((rows, 1), jnp.float32),
                        pltpu.VMEM((rows, LANES), jnp.float32)],
        compiler_params=pltpu.CompilerParams(
            dimension_semantics=("arbitrary", "arbitrary"),
            vmem_limit_bytes=48 << 20),
        name="dilated_attention",
    )(q, k, v)


FOX_KV_TILE = 512


def _fox_kernel(q_ref, k_ref, v_ref, cq_ref, ck_ref, o_ref, m_s, l_s, acc_s, *, qb):
    j = pl.program_id(1)
    s0 = j * qb
    kt = FOX_KV_TILE
    bf16 = jnp.bfloat16
    scale = HEAD_DIM ** -0.5
    q_pos = s0 + lax.broadcasted_iota(jnp.int32, (qb, 1), 0)
    lane = lax.broadcasted_iota(jnp.int32, (qb, LANES), 1)
    first_head = lane < HEAD_DIM
    n_pairs = FOX_HEADS // 2
    rows = FOX_HEADS * qb
    n_tiles = (s0 + qb - 1) // kt + 1

    q2 = []
    for pair in range(n_pairs):
        q = q_ref[0, :, pl.ds(pair * LANES, LANES)] * scale
        q2.append(jnp.concatenate([jnp.where(first_head, q, 0.0),
                                   jnp.where(first_head, 0.0, q)], axis=0).astype(bf16))
    cq = [cq_ref[0, :, h:h + 1] for h in range(FOX_HEADS)]
    m_s[...] = jnp.full((rows, 1), NEG_INF, jnp.float32)
    l_s[...] = jnp.zeros((rows, 1), jnp.float32)
    acc_s[...] = jnp.zeros((rows, LANES), jnp.float32)

    def body(t, c):
        k0 = pl.multiple_of(t * kt, kt)
        k_pos = k0 + lax.broadcasted_iota(jnp.int32, (qb, kt), 1)
        mask = k_pos <= q_pos
        logits = jnp.concatenate(
            [_nt_dot(q2[pair], k_ref[0, pl.ds(k0, kt), pl.ds(pair * LANES, LANES)])
             for pair in range(n_pairs)], axis=0)
        bias = jnp.concatenate([cq[h] - ck_ref[0, h:h + 1, pl.ds(k0, kt)] for h in range(FOX_HEADS)],
                               axis=0)
        lg = jnp.where(mask[None], (logits + bias).reshape(FOX_HEADS, qb, kt), NEG_INF)
        m_old = m_s[...].reshape(FOX_HEADS, qb, 1)
        m_new = jnp.maximum(m_old, jnp.max(lg, axis=-1, keepdims=True))
        p = jnp.exp(lg - m_new)
        alpha = jnp.exp(m_old - m_new)
        l_s[...] = (alpha * l_s[...].reshape(FOX_HEADS, qb, 1)
                    + jnp.sum(p, axis=-1, keepdims=True)).reshape(rows, 1)
        pb = p.reshape(rows, kt).astype(bf16)
        pv = jnp.concatenate(
            [jnp.dot(pb[2 * pair * qb:2 * (pair + 1) * qb],
                     v_ref[0, pl.ds(k0, kt), pl.ds(pair * LANES, LANES)],
                     preferred_element_type=jnp.float32) for pair in range(n_pairs)], axis=0)
        acc_s[...] = alpha.reshape(rows, 1) * acc_s[...] + pv
        m_s[...] = m_new.reshape(rows, 1)
        return c
    lax.fori_loop(0, n_tiles, body, 0)

    o2 = acc_s[...] / l_s[...]
    for pair in range(n_pairs):
        r0 = 2 * pair * qb
        o_ref[0, :, pl.ds(pair * LANES, LANES)] = jnp.where(first_head, o2[r0:r0 + qb],
                                                             o2[r0 + qb:r0 + 2 * qb])


def fox_attention_pallas(q, k, v, cum):
    b, s, w = q.shape
    qb = Q_BLOCK
    return pl.pallas_call(
        functools.partial(_fox_kernel, qb=qb),
        grid=(b, s // qb),
        in_specs=[pl.BlockSpec((1, qb, w), lambda i, j: (i, j, 0)),
                  pl.BlockSpec((1, s, w), lambda i, j: (i, 0, 0)),
                  pl.BlockSpec((1, s, w), lambda i, j: (i, 0, 0)),
                  pl.BlockSpec((1, qb, FOX_HEADS), lambda i, j: (i, j, 0)),
                  pl.BlockSpec((1, FOX_HEADS, s), lambda i, j: (i, 0, 0))],
        out_specs=pl.BlockSpec((1, qb, w), lambda i, j: (i, j, 0)),
        out_shape=jax.ShapeDtypeStruct((b, s, w), jnp.float32),
        scratch_shapes=[pltpu.VMEM((FOX_HEADS * qb, 1), jnp.float32),
                        pltpu.VMEM((FOX_HEADS * qb, 1), jnp.float32),
                        pltpu.VMEM((FOX_HEADS * qb, LANES), jnp.float32)],
        compiler_params=pltpu.CompilerParams(
            dimension_semantics=("arbitrary", "arbitrary"),
            vmem_limit_bytes=40 << 20),
        name="fox_attention",
    )(q, k, v, cum, jnp.transpose(cum, (0, 2, 1)))


LRU_TILE = 512


def _lru_kernel(x_ref, gate_ref, cw_ref, cb_ref, wa_ref, ba_ref, wx_ref, bx_ref, lam_ref, o_ref,
                xbuf, a_s, u_s, h_s, hlast, *, tt):
    j = pl.program_id(1)
    c = x_ref.shape[-1]
    bf16 = jnp.bfloat16

    @pl.when(j == 0)
    def _():
        xbuf[pl.ds(0, SUBLANES), :] = jnp.zeros((SUBLANES, c), jnp.float32)
        hlast[...] = jnp.zeros((1, c), jnp.float32)

    x = x_ref[0]
    xbuf[pl.ds(SUBLANES, tt), :] = x
    xc = cb_ref[...] + cw_ref[LRU_CONV - 1:LRU_CONV, :] * x
    for back in range(1, LRU_CONV):
        xc = xc + cw_ref[LRU_CONV - 1 - back:LRU_CONV - back, :] * xbuf[pl.ds(SUBLANES - back, tt), :]
    xbuf[pl.ds(0, SUBLANES), :] = x[tt - SUBLANES:, :]

    xb = xc.astype(bf16)
    r = jax.nn.sigmoid(jnp.dot(xb, wa_ref[...], preferred_element_type=jnp.float32) + ba_ref[...])
    i_g = jax.nn.sigmoid(jnp.dot(xb, wx_ref[...], preferred_element_type=jnp.float32) + bx_ref[...])
    lam = lam_ref[...]
    softplus_neg = jnp.maximum(-lam, 0.0) + jnp.log1p(jnp.exp(-jnp.abs(lam)))
    log_a = -LRU_C * r * softplus_neg
    a_s[...] = jnp.exp(log_a)
    u_s[...] = jnp.sqrt(1.0 - jnp.exp(2.0 * log_a)) * (i_g * xc)

    def group(gi, h):
        r0 = pl.multiple_of(gi * SUBLANES, SUBLANES)
        a8 = a_s[pl.ds(r0, SUBLANES), :]
        u8 = u_s[pl.ds(r0, SUBLANES), :]
        rows = []
        for s in range(SUBLANES):
            h = a8[s:s + 1, :] * h + u8[s:s + 1, :]
            rows.append(h)
        h_s[pl.ds(r0, SUBLANES), :] = jnp.concatenate(rows, axis=0)
        return h
    hlast[...] = lax.fori_loop(0, tt // SUBLANES, group, hlast[...])
    o_ref[0] = h_s[...] * jax.nn.gelu(gate_ref[0])


def _block_diag(w):
    n, c, d = w.shape
    eye = jnp.eye(n, dtype=w.dtype)
    return (eye[:, None, :, None] * w[:, :, None, :]).reshape(n * c, n * d)


def rg_lru_pallas(x_in, gate, conv_w, conv_b, w_a, b_a, w_x, b_x, lam):
    b, s, c = x_in.shape
    tt = LRU_TILE
    row = lambda v: v.reshape(1, c)
    tok_spec = pl.BlockSpec((1, tt, c), lambda i, j: (i, j, 0))
    full = lambda shape: pl.BlockSpec(shape, lambda i, j: (0,) * len(shape))
    return pl.pallas_call(
        functools.partial(_lru_kernel, tt=tt),
        grid=(b, s // tt),
        in_specs=[tok_spec, tok_spec, full((LRU_CONV, c)), full((1, c)), full((c, c)), full((1, c)),
                  full((c, c)), full((1, c)), full((1, c))],
        out_specs=tok_spec,
        out_shape=jax.ShapeDtypeStruct((b, s, c), jnp.float32),
        scratch_shapes=[pltpu.VMEM((tt + SUBLANES, c), jnp.float32),
                        pltpu.VMEM((tt, c), jnp.float32),
                        pltpu.VMEM((tt, c), jnp.float32),
                        pltpu.VMEM((tt, c), jnp.float32),
                        pltpu.VMEM((1, c), jnp.float32)],
        compiler_params=pltpu.CompilerParams(dimension_semantics=("arbitrary", "arbitrary")),
        name="rg_lru",
    )(x_in, gate, conv_w, row(conv_b), _block_diag(w_a).astype(jnp.bfloat16), row(b_a),
      _block_diag(w_x).astype(jnp.bfloat16), row(b_x), row(lam))


def rms_norm(x, g):
    xf = x.astype(jnp.float32)
    y = xf * lax.rsqrt(jnp.mean(xf * xf, axis=-1, keepdims=True) + NORM_EPS)
    return (y * g.astype(jnp.float32)).astype(x.dtype)


def rope(x, pos):
    half = x.shape[-1] // 2
    freqs = ROPE_THETA ** (-jnp.arange(half, dtype=jnp.float32) / half)
    ang = pos.astype(jnp.float32)[:, None] * freqs[None, :]
    cos = jnp.cos(ang)[None, :, None, :]
    sin = jnp.sin(ang)[None, :, None, :]
    xf = x.astype(jnp.float32)
    x1, x2 = xf[..., :half], xf[..., half:]
    return jnp.concatenate([x1 * cos - x2 * sin, x2 * cos + x1 * sin], axis=-1).astype(x.dtype)


def _unblock(y):
    y = jnp.moveaxis(y, 0, 1)
    return y.reshape((y.shape[0], y.shape[1] * y.shape[2]) + y.shape[3:])


def masked_softmax(logits, mask):
    logits = jnp.where(mask, logits, NEG_INF)
    m = jnp.max(logits, axis=-1, keepdims=True)
    e = jnp.where(mask, jnp.exp(logits - m), 0.0)
    return e / jnp.maximum(jnp.sum(e, axis=-1, keepdims=True), 1e-30)


def fox_attention(q, k, v, f_logit, b_f):
    B, S, H, dh = q.shape
    log_f = jax.nn.log_sigmoid(f_logit.astype(jnp.float32) + b_f.astype(jnp.float32))
    cum = jnp.transpose(jnp.cumsum(log_f, axis=1), (0, 2, 1))
    scale = dh ** -0.5
    k_pos = jnp.arange(S)

    def block(i):
        s0 = i * Q_BLOCK
        qb = lax.dynamic_slice_in_dim(q, s0, Q_BLOCK, axis=1)
        cq = lax.dynamic_slice_in_dim(cum, s0, Q_BLOCK, axis=2)
        logits = jnp.einsum('bqhd,bkhd->bhqk', qb, k).astype(jnp.float32) * scale
        logits = logits + cq[..., None] - cum[:, :, None, :]
        q_pos = s0 + jnp.arange(Q_BLOCK)
        logits = jnp.where(k_pos[None, :] <= q_pos[:, None], logits, NEG_INF)
        p = jax.nn.softmax(logits, axis=-1)
        return jnp.einsum('bhqk,bkhd->bqhd', p.astype(v.dtype), v)

    return _unblock(lax.map(block, jnp.arange(S // Q_BLOCK)))


def rg_lru_branch(x_in, gate, conv_w, conv_b, w_a, b_a, w_x, b_x, lam):
    B, S, C = x_in.shape
    xc = lax.conv_general_dilated(x_in, conv_w[:, None, :], window_strides=(1,),
                                  padding=[(LRU_CONV - 1, 0)],
                                  dimension_numbers=('NWC', 'WIO', 'NWC'),
                                  feature_group_count=C) + conv_b
    xr = xc.reshape(B, S, LRU_BLOCKS, C // LRU_BLOCKS)
    r = jax.nn.sigmoid(jnp.einsum('bsnc,ncd->bsnd', xr, w_a).reshape(B, S, C) + b_a)
    i_g = jax.nn.sigmoid(jnp.einsum('bsnc,ncd->bsnd', xr, w_x).reshape(B, S, C) + b_x)
    log_a = -LRU_C * r.astype(jnp.float32) * jax.nn.softplus(-lam.astype(jnp.float32))
    a = jnp.exp(log_a)
    u = jnp.sqrt(-jnp.expm1(2.0 * log_a)) * (i_g * xc).astype(jnp.float32)

    def combine(left, right):
        a_l, b_l = left
        a_r, b_r = right
        return a_l * a_r, a_r * b_l + b_r

    _, h = lax.associative_scan(combine, (a, u), axis=1)
    return h.astype(x_in.dtype) * jax.nn.gelu(gate)


def dilated_attention(q, k, v):
    B, S, H, dh = q.shape
    G, Hg = DIL_GROUPS, DIL_HEADS_PER_GROUP
    qg = q.reshape(B, S, G, Hg, dh)
    kg = k.reshape(B, S, G, Hg, dh)
    vg = v.reshape(B, S, G, Hg, dh)
    dil = jnp.array([d for _, d in DIL_PAIRS], dtype=jnp.int32)
    steps = jnp.arange(DIL_KEYS)
    g_idx = jnp.arange(G)[:, None, None]
    scale = dh ** -0.5

    def block(i):
        s0 = i * Q_BLOCK
        q_pos = s0 + jnp.arange(Q_BLOCK)
        k_pos = q_pos[None, :, None] - dil[:, None, None] * steps[None, None, :]
        valid = k_pos >= 0
        k_idx = jnp.maximum(k_pos, 0)
        kb = kg[:, k_idx, g_idx]
        vb = vg[:, k_idx, g_idx]
        qb = lax.dynamic_slice_in_dim(qg, s0, Q_BLOCK, axis=1)
        logits = jnp.einsum('bqghd,bgqkhd->bgqhk', qb, kb).astype(jnp.float32) * scale
        logits = jnp.where(valid[None, :, :, None, :], logits, NEG_INF)
        m = jnp.max(logits, axis=-1, keepdims=True)
        e = jnp.exp(logits - m)
        l = jnp.sum(e, axis=-1)
        o = jnp.einsum('bgqhk,bgqkhd->bgqhd', e, vb.astype(jnp.float32)) / l[..., None]
        lse = m[..., 0] + jnp.log(l)
        w = jax.nn.softmax(lse, axis=1)
        return jnp.einsum('bgqh,bgqhd->bqhd', w, o).astype(v.dtype)

    return _unblock(lax.map(block, jnp.arange(S // Q_BLOCK)))


def nsa_attention(q, k_cmp, v_cmp, k_slc, v_slc, k_win, v_win, gate_logits,
                  pe_k, pe_v, w1_k, w2_k, w1_v, w2_v):
    B, S, H, dh = q.shape
    Hkv = k_cmp.shape[2]
    G = H // Hkv
    pos = jnp.arange(S)
    q_rot = rope(q, pos)
    k_slc = rope(k_slc, pos)
    k_win = rope(k_win, pos)
    gates = jax.nn.sigmoid(gate_logits).reshape(B, S, H, 3)
    scale = dh ** -0.5

    n_cmp = (S - NSA_CMP_BLOCK) // NSA_CMP_STRIDE + 1
    cmp_idx = jnp.arange(n_cmp)[:, None] * NSA_CMP_STRIDE + jnp.arange(NSA_CMP_BLOCK)[None, :]

    def compress(kv, pe, w1, w2):
        blocks = kv[:, cmp_idx] + pe[None, None, :, None, :]
        flat = jnp.moveaxis(blocks, 3, 2).reshape(B, n_cmp, Hkv, NSA_CMP_BLOCK * dh)
        return jax.nn.gelu(flat @ w1) @ w2

    kc = compress(k_cmp, pe_k, w1_k, w2_k)
    vc = compress(v_cmp, pe_v, w1_v, w2_v)
    cmp_start = jnp.arange(n_cmp) * NSA_CMP_STRIDE
    cmp_end = cmp_start + NSA_CMP_BLOCK - 1

    n_slc = S // NSA_SLC_BLOCK
    top_n = min(NSA_TOP_N, n_slc)
    slc_start = jnp.arange(n_slc) * NSA_SLC_BLOCK
    overlap = ((cmp_start[:, None] < slc_start[None, :] + NSA_SLC_BLOCK)
               & (cmp_start[:, None] + NSA_CMP_BLOCK > slc_start[None, :])).astype(jnp.float32)
    ks_blocks = jnp.transpose(k_slc.reshape(B, n_slc, NSA_SLC_BLOCK, Hkv, dh), (0, 3, 1, 2, 4))
    vs_blocks = jnp.transpose(v_slc.reshape(B, n_slc, NSA_SLC_BLOCK, Hkv, dh), (0, 3, 1, 2, 4))
    pad = ((0, 0), (NSA_WINDOW, 0), (0, 0), (0, 0))
    k_win_pad = jnp.pad(k_win, pad)
    v_win_pad = jnp.pad(v_win, pad)
    b_idx = jnp.arange(B)[:, None, None, None]
    h_idx = jnp.arange(Hkv)[None, :, None, None]
    blk_ids = jnp.arange(n_slc)

    def block(i):
        s0 = i * Q_BLOCK
        q_pos = s0 + jnp.arange(Q_BLOCK)
        qb = lax.dynamic_slice_in_dim(q, s0, Q_BLOCK, axis=1).reshape(B, Q_BLOCK, Hkv, G, dh)
        qrb = lax.dynamic_slice_in_dim(q_rot, s0, Q_BLOCK, axis=1).reshape(B, Q_BLOCK, Hkv, G, dh)
        lc = jnp.einsum('bqkgd,bckd->bkgqc', qb, kc).astype(jnp.float32) * scale
        p_c = masked_softmax(lc, (cmp_end[None, :] <= q_pos[:, None])[None, None, None])
        o_c = jnp.einsum('bkgqc,bckd->bqkgd', p_c, vc.astype(jnp.float32))
        imp = jnp.einsum('bkgqc,cn->bkqn', p_c, overlap)
        cur = q_pos // NSA_SLC_BLOCK
        forced = ((blk_ids[None, :] == 0) | (blk_ids[None, :] == cur[:, None])
                  | (blk_ids[None, :] == cur[:, None] - 1))
        imp = jnp.where(forced, NSA_FORCE_SCORE, imp)
        imp = jnp.where(blk_ids[None, :] <= cur[:, None], imp, NEG_INF)
        top_val, top_idx = lax.top_k(imp, top_n)
        sel_ok = top_val > 0.5 * NEG_INF
        ksel = ks_blocks[b_idx, h_idx, top_idx]
        vsel = vs_blocks[b_idx, h_idx, top_idx]
        sel_pos = top_idx[..., None] * NSA_SLC_BLOCK + jnp.arange(NSA_SLC_BLOCK)
        sel_mask = sel_ok[..., None] & (sel_pos <= q_pos[None, None, :, None, None])
        n_sel_keys = top_n * NSA_SLC_BLOCK
        ls = jnp.einsum('bqkgd,bkqnld->bkgqnl', qrb, ksel).astype(jnp.float32) * scale
        ls = ls.reshape(B, Hkv, G, Q_BLOCK, n_sel_keys)
        p_s = masked_softmax(ls, sel_mask.reshape(B, Hkv, 1, Q_BLOCK, n_sel_keys))
        o_s = jnp.einsum('bkgqm,bkqmd->bqkgd', p_s,
                         vsel.reshape(B, Hkv, Q_BLOCK, n_sel_keys, dh).astype(jnp.float32))
        kw = lax.dynamic_slice_in_dim(k_win_pad, s0, NSA_WINDOW + Q_BLOCK, axis=1)
        vw = lax.dynamic_slice_in_dim(v_win_pad, s0, NSA_WINDOW + Q_BLOCK, axis=1)
        w_pos = s0 - NSA_WINDOW + jnp.arange(NSA_WINDOW + Q_BLOCK)
        dist = q_pos[:, None] - w_pos[None, :]
        w_mask = (dist >= 0) & (dist < NSA_WINDOW) & (w_pos[None, :] >= 0)
        lw = jnp.einsum('bqkgd,bwkd->bkgqw', qrb, kw).astype(jnp.float32) * scale
        p_w = masked_softmax(lw, w_mask)
        o_w = jnp.einsum('bkgqw,bwkd->bqkgd', p_w, vw.astype(jnp.float32))
        gb = lax.dynamic_slice_in_dim(gates, s0, Q_BLOCK, axis=1).reshape(B, Q_BLOCK, Hkv, G, 3)
        gb = gb.astype(jnp.float32)
        out = gb[..., 0:1] * o_c + gb[..., 1:2] * o_s + gb[..., 2:3] * o_w
        return out.reshape(B, Q_BLOCK, H, dh).astype(q.dtype)

    return _unblock(lax.map(block, jnp.arange(S // Q_BLOCK)))


def nsa_branch(nq, nkc, nvc, nks, nvs, nkw, nvw, ng, pe_k, pe_v, w1_k, w2_k, w1_v, w2_v):
    B, S, _ = nq.shape
    dh = HEAD_DIM
    bf16 = jnp.bfloat16
    pos = jnp.arange(S)
    q4 = nq.reshape(B, S, NSA_HEADS, dh)
    qn = jnp.transpose(q4, (0, 2, 1, 3))
    qr = jnp.transpose(rope(q4, pos), (0, 2, 1, 3))
    ks = rope(nks.reshape(B, S, 1, dh), pos).reshape(B, S, dh).astype(bf16)
    kw = rope(nkw.reshape(B, S, 1, dh), pos).reshape(B, S, dh).astype(bf16)

    n_cmp = (S - NSA_CMP_BLOCK) // NSA_CMP_STRIDE + 1
    n_cmp_pad = S // NSA_CMP_STRIDE
    n_slc = S // NSA_SLC_BLOCK

    def compress(kv, pe, w1, w2):
        chunks = kv.reshape(B, n_cmp_pad, NSA_CMP_STRIDE * dh)
        flat = jnp.concatenate([chunks[:, :-1], chunks[:, 1:]], axis=-1) + pe.reshape(-1)
        out = jax.nn.gelu(flat @ w1) @ w2
        return jnp.pad(out, ((0, 0), (0, n_cmp_pad - n_cmp), (0, 0))).astype(bf16)

    kc = compress(nkc, pe_k, w1_k, w2_k)
    vc = compress(nvc, pe_v, w1_v, w2_v)
    cmp_start = jnp.arange(n_cmp_pad) * NSA_CMP_STRIDE
    slc_start = jnp.arange(n_slc) * NSA_SLC_BLOCK
    overlap = ((cmp_start[:, None] < slc_start[None, :] + NSA_SLC_BLOCK)
               & (cmp_start[:, None] + NSA_CMP_BLOCK > slc_start[None, :])
               & (jnp.arange(n_cmp_pad)[:, None] < n_cmp)).astype(bf16)
    out = nsa_attention_pallas(qn, qr, kc, vc, overlap, ks, nvs.astype(bf16), kw,
                               nvw.astype(bf16), ng)
    return jnp.transpose(out, (0, 2, 1, 3)).reshape(B, S, NSA_W)


def peer_ffn(x_res, h2, w_q, sub_keys, u, v):
    expert, gt = peer_route(h2, w_q, sub_keys)
    uv = jnp.concatenate([u, v], axis=1)
    return peer_mix(h2, x_res, expert, gt, uv)


def kernel(x, norm1_g, w_in, fox_b_f, lru_conv_w, lru_conv_b, lru_w_a, lru_b_a, lru_w_x, lru_b_x, lru_lambda, nsa_pe_k, nsa_pe_v, nsa_w1_k, nsa_w2_k, nsa_w1_v, nsa_w2_v, w_br_fox, w_br_lru, w_br_dil, w_br_nsa, w_out, norm2_g, peer_w_q, peer_sub_keys, peer_u, peer_v, final_g):
    B, S, D = x.shape
    pos = jnp.arange(S)
    split_points = np.cumsum(IN_SPLITS)[:-1].tolist()

    def heads(t, n):
        return t.reshape(B, S, n, HEAD_DIM)

    for l in range(DEPTH):
        h = rms_norm(x, norm1_g[l])
        proj = h @ w_in[l]
        (fq, fk, fv, ff, lx, lg, dq, dk, dv, nq, nkc, nvc, nks, nvs, nkw, nvw, ng, mg) = \
            jnp.split(proj, split_points, axis=-1)

        cum = jnp.cumsum(jax.nn.log_sigmoid(ff + fox_b_f[l]), axis=1)
        o_fox = fox_attention_pallas(fq, fk.astype(jnp.bfloat16), fv.astype(jnp.bfloat16), cum)
        o_lru = rg_lru_pallas(lx, lg, lru_conv_w[l], lru_conv_b[l], lru_w_a[l], lru_b_a[l],
                              lru_w_x[l], lru_b_x[l], lru_lambda[l])
        o_dil = dilated_attention_pallas(
            rope(heads(dq, DIL_HEADS), pos).reshape(B, S, DIL_W),
            rope(heads(dk, DIL_HEADS), pos).reshape(B, S, DIL_W).astype(jnp.bfloat16),
            dv.astype(jnp.bfloat16))
        o_nsa = nsa_branch(nq, nkc, nvc, nks, nvs, nkw, nvw, ng,
                           nsa_pe_k[l], nsa_pe_v[l], nsa_w1_k[l], nsa_w2_k[l],
                           nsa_w1_v[l], nsa_w2_v[l])

        gates = jax.nn.sigmoid(mg).reshape(B, S, N_BRANCHES, D)
        merged = (gates[:, :, 0] * (o_fox @ w_br_fox[l])
                  + gates[:, :, 1] * (o_lru @ w_br_lru[l])
                  + gates[:, :, 2] * (o_dil @ w_br_dil[l])
                  + gates[:, :, 3] * (o_nsa @ w_br_nsa[l]))
        x = x + merged @ w_out[l]

        h2 = rms_norm(x, norm2_g[l])
        x = peer_ffn(x.reshape(B * S, D), h2.reshape(B * S, D), peer_w_q[l], peer_sub_keys[l],
                     peer_u[l], peer_v[l]).reshape(B, S, D)

    return rms_norm_pallas(x.reshape(B * S, D), final_g).reshape(B, S, D)
```

```python
import functools

import jax
import jax.numpy as jnp
import numpy as np
from jax import lax
from jax.experimental import pallas as pl
from jax.experimental.pallas import tpu as pltpu

D_MODEL = 1024
DEPTH = 2
HEAD_DIM = 64
Q_BLOCK = 128
ROPE_THETA = 10000.0
NORM_EPS = 1e-6
NEG_INF = -1e30
N_BRANCHES = 4
FOX_HEADS = 4
FOX_W = FOX_HEADS * HEAD_DIM
LRU_WIDTH = 256
LRU_BLOCKS = 4
LRU_CONV = 4
LRU_C = 8.0
DIL_PAIRS = ((128, 1), (512, 4), (2048, 16))
DIL_GROUPS = len(DIL_PAIRS)
DIL_HEADS_PER_GROUP = 2
DIL_HEADS = DIL_GROUPS * DIL_HEADS_PER_GROUP
DIL_W = DIL_HEADS * HEAD_DIM
DIL_OUT_W = DIL_HEADS_PER_GROUP * HEAD_DIM
DIL_KEYS = DIL_PAIRS[0][0] // DIL_PAIRS[0][1] + 1
NSA_HEADS = 4
NSA_KV_HEADS = 1
NSA_W = NSA_HEADS * HEAD_DIM
NSA_KV_W = NSA_KV_HEADS * HEAD_DIM
NSA_CMP_BLOCK = 32
NSA_CMP_STRIDE = 16
NSA_SLC_BLOCK = 64
NSA_TOP_N = 16
NSA_WINDOW = 512
NSA_FORCE_SCORE = 1e6
PEER_HEADS = 8
PEER_N_KEYS = 128
PEER_TOPK = 16
PEER_D_KEY = 256
PEER_BLOCK = 128

IN_SPLITS = (FOX_W, FOX_W, FOX_W, FOX_HEADS,
             LRU_WIDTH, LRU_WIDTH,
             DIL_W, DIL_W, DIL_W,
             NSA_W, NSA_KV_W, NSA_KV_W, NSA_KV_W, NSA_KV_W, NSA_KV_W, NSA_KV_W, NSA_HEADS * 3,
             N_BRANCHES * D_MODEL)


def _rms_norm_kernel(x_ref, g_ref, o_ref):
    x = x_ref[...]
    ms = jnp.mean(x * x, axis=-1, keepdims=True)
    o_ref[...] = x * lax.rsqrt(ms + NORM_EPS) * g_ref[...]


def rms_norm_pallas(x, g, rows=512):
    t, d = x.shape
    return pl.pallas_call(
        _rms_norm_kernel,
        grid=(t // rows,),
        in_specs=[pl.BlockSpec((rows, d), lambda i: (i, 0)),
                  pl.BlockSpec((1, d), lambda i: (0, 0))],
        out_specs=pl.BlockSpec((rows, d), lambda i: (i, 0)),
        out_shape=jax.ShapeDtypeStruct((t, d), x.dtype),
        name="rms_norm",
    )(x, g.reshape(1, d))


def _gelu_tanh(x):
    return 0.5 * x * (1.0 + jnp.tanh(0.7978845608028654 * (x + 0.044715 * x * x * x)))


PEER_TOKENS_PER_STEP = 8
LANES = 128
SUBLANES = 8


def _peer_mix_kernel(idx_hbm, x_ref, gt_ref, res_ref, uv_hbm, o_ref,
                     idx_smem, buf, idx_sem, row_sem, *, tb, n_sel, d):
    i = pl.program_id(0)
    n = pl.num_programs(0)
    rows = tb * n_sel
    slot = i % 2

    def idx_copy(step, s):
        return pltpu.make_async_copy(idx_hbm.at[step], idx_smem.at[pl.ds(s * rows, rows)],
                                     idx_sem.at[s])

    def issue_rows(s):
        base = s * rows

        def body(jj, c):
            for k in range(SUBLANES):
                row = idx_smem[base + jj * SUBLANES + k]
                pltpu.make_async_copy(uv_hbm.at[row], buf.at[s, jj, pl.ds(k, 1)],
                                      row_sem.at[s]).start()
            return c
        lax.fori_loop(0, rows // SUBLANES, body, 0)

    @pl.when(i == 0)
    def _():
        idx_copy(0, 0).start()
        idx_copy(0, 0).wait()
        issue_rows(0)

        @pl.when(n > 1)
        def _():
            idx_copy(1, 1).start()

    @pl.when(i + 1 < n)
    def _():
        idx_copy(i + 1, 1 - slot).wait()
        issue_rows(1 - slot)

    pltpu.make_async_copy(buf.at[slot], buf.at[slot], row_sem.at[slot]).wait()

    @pl.when(i + 2 < n)
    def _():
        idx_copy(i + 2, slot).start()

    n_chunks = d // LANES
    groups = n_sel // SUBLANES

    def sel_rows(t, lane0):
        tile = buf[slot, pl.ds(t * groups, groups), :, pl.ds(lane0, LANES)]
        return tile.reshape(n_sel, LANES)

    for t in range(tb):
        acc = jnp.zeros((n_sel, LANES), jnp.float32)
        for c in range(n_chunks):
            xc = x_ref[pl.ds(t, 1), pl.ds(c * LANES, LANES)]
            acc = acc + sel_rows(t, c * LANES) * xc
        s = jnp.sum(acc, axis=-1, keepdims=True)
        w = gt_ref[0, :, pl.ds(t, 1)] * _gelu_tanh(s)
        wb = jnp.broadcast_to(w, (n_sel, LANES))
        for c in range(n_chunks):
            vc = sel_rows(t, d + c * LANES)
            oc = jnp.sum(wb * vc, axis=0, keepdims=True)
            o_ref[pl.ds(t, 1), pl.ds(c * LANES, LANES)] = (
                res_ref[pl.ds(t, 1), pl.ds(c * LANES, LANES)] + oc)


def peer_mix(x, res, idx, gt_blocks, uv):
    t, d = x.shape
    n_sel = idx.shape[1]
    tb = PEER_TOKENS_PER_STEP
    steps = t // tb
    idx_steps = idx.reshape(steps, tb * n_sel)
    nb, _, tq = gt_blocks.shape
    gt = jnp.transpose(gt_blocks.reshape(nb, n_sel, tq // tb, tb), (0, 2, 1, 3)).reshape(steps, n_sel, tb)
    kern = functools.partial(_peer_mix_kernel, tb=tb, n_sel=n_sel, d=d)
    return pl.pallas_call(
        kern,
        grid=(steps,),
        in_specs=[pl.BlockSpec(memory_space=pl.ANY),
                  pl.BlockSpec((tb, d), lambda i: (i, 0)),
                  pl.BlockSpec((1, n_sel, tb), lambda i: (i, 0, 0)),
                  pl.BlockSpec((tb, d), lambda i: (i, 0)),
                  pl.BlockSpec(memory_space=pl.ANY)],
        out_specs=pl.BlockSpec((tb, d), lambda i: (i, 0)),
        out_shape=jax.ShapeDtypeStruct((t, d), jnp.float32),
        scratch_shapes=[pltpu.SMEM((2 * tb * n_sel,), jnp.int32),
                        pltpu.VMEM((2, tb * n_sel // SUBLANES, SUBLANES, 2 * d), jnp.float32),
                        pltpu.SemaphoreType.DMA((2,)),
                        pltpu.SemaphoreType.DMA((2,))],
        compiler_params=pltpu.CompilerParams(
            dimension_semantics=("arbitrary",),
            vmem_limit_bytes=2 * tb * n_sel * 2 * d * 4 + (8 << 20)),
        name="peer_mix",
    )(idx_steps, x, gt, res, uv.reshape(uv.shape[0], 1, 2 * d))


def _topk_rows(work, k):
    n = work.shape[0]
    row = lax.broadcasted_iota(jnp.int32, work.shape, 0)
    vals, idxs = [], []
    for _ in range(k):
        mx = jnp.max(work, axis=0, keepdims=True)
        pos = jnp.min(jnp.where(work == mx, row, n), axis=0, keepdims=True)
        work = jnp.where(row == pos, -jnp.inf, work)
        vals.append(mx)
        idxs.append(pos)
    return jnp.concatenate(vals, axis=0), jnp.concatenate(idxs, axis=0)


def _peer_route_kernel(x_ref, wq_ref, keys_ref, idx_ref, gt_ref, q_s, idx_s, g_s, *, tq):
    bf16 = jnp.bfloat16
    k = PEER_TOPK
    half = PEER_D_KEY // 2
    q = jnp.dot(x_ref[...].astype(bf16), wq_ref[...], preferred_element_type=jnp.float32)
    for c in range(2 * PEER_HEADS):
        q_s[c] = q[:, c * half:(c + 1) * half].astype(bf16)

    def head_body(h, carry):
        tops = []
        for p in range(2):
            scores = _nt_dot(keys_ref[p], q_s[2 * h + p])
            tops.append(_topk_rows(scores, k))
        (v0, i0), (v1, i1) = tops
        cand = jnp.concatenate([v0[a:a + 1, :] + v1 for a in range(k)], axis=0)
        cand_idx = jnp.concatenate([i0[a:a + 1, :] * PEER_N_KEYS + i1 for a in range(k)], axis=0)
        flat = lax.broadcasted_iota(jnp.int32, cand.shape, 0)
        top_s, experts = [], []
        for _ in range(k):
            mx = jnp.max(cand, axis=0, keepdims=True)
            pos = jnp.min(jnp.where(cand == mx, flat, k * k), axis=0, keepdims=True)
            pick = flat == pos
            experts.append(jnp.max(jnp.where(pick, cand_idx, -1), axis=0, keepdims=True))
            cand = jnp.where(pick, -jnp.inf, cand)
            top_s.append(mx)
        top_s = jnp.concatenate(top_s, axis=0)
        ex = jnp.exp(top_s - jnp.max(top_s, axis=0, keepdims=True))
        r0 = pl.multiple_of(h * k, k)
        g_s[pl.ds(r0, k), :] = ex / jnp.sum(ex, axis=0, keepdims=True)
        idx_s[pl.ds(r0, k), :] = jnp.concatenate(experts, axis=0)
        return carry
    lax.fori_loop(0, PEER_HEADS, head_body, 0)

    idx_ref[...] = jnp.transpose(idx_s[...])
    gt_ref[0] = g_s[...]


def peer_route(h2, w_q, sub_keys):
    t, d = h2.shape
    tq = PEER_BLOCK
    n_sel = PEER_HEADS * PEER_TOPK
    half = PEER_D_KEY // 2
    return pl.pallas_call(
        functools.partial(_peer_route_kernel, tq=tq),
        grid=(t // tq,),
        in_specs=[pl.BlockSpec((tq, d), lambda i: (i, 0)),
                  pl.BlockSpec((d, PEER_HEADS * PEER_D_KEY), lambda i: (0, 0)),
                  pl.BlockSpec((2, PEER_N_KEYS, half), lambda i: (0, 0, 0))],
        out_specs=[pl.BlockSpec((tq, n_sel), lambda i: (i, 0)),
                   pl.BlockSpec((1, n_sel, tq), lambda i: (i, 0, 0))],
        out_shape=[jax.ShapeDtypeStruct((t, n_sel), jnp.int32),
                   jax.ShapeDtypeStruct((t // tq, n_sel, tq), jnp.float32)],
        scratch_shapes=[pltpu.VMEM((2 * PEER_HEADS, tq, half), jnp.bfloat16),
                        pltpu.VMEM((n_sel, tq), jnp.int32),
                        pltpu.VMEM((n_sel, tq), jnp.float32)],
        compiler_params=pltpu.CompilerParams(dimension_semantics=("arbitrary",),
                                             vmem_limit_bytes=32 << 20),
        name="peer_route",
    )(h2, w_q.astype(jnp.bfloat16), sub_keys.astype(jnp.bfloat16))


NSA_KV_TILE = 512


def _nt_dot(a, b):
    return lax.dot_general(a, b, (((1,), (1,)), ((), ())), preferred_element_type=jnp.float32)


def _nsa_kernel(qn_ref, qr_ref, kc_ref, vc_ref, ov_ref, ks_ref, vs_ref, kw_ref, vw_ref, gl_ref,
                o_ref, m_s, l_s, acc_s, out_s, *, n_heads, qb, n_cmp_pad, n_slc):
    j = pl.program_id(1)
    s0 = j * qb
    hq = n_heads * qb
    bf16 = jnp.bfloat16
    scale = HEAD_DIM ** -0.5
    q_pos = s0 + lax.broadcasted_iota(jnp.int32, (qb, 1), 0)

    gates = jax.nn.sigmoid(gl_ref[0])

    def gate_col(br):
        return jnp.concatenate([gates[:, 3 * h + br:3 * h + br + 1] for h in range(n_heads)], axis=0)

    def masked_update(logits, mask, v_tile):
        kt = logits.shape[-1]
        lg = jnp.where(mask[None], logits.reshape(n_heads, qb, kt), NEG_INF)
        m_old = m_s[...].reshape(n_heads, qb, 1)
        m_new = jnp.maximum(m_old, jnp.max(lg, axis=-1, keepdims=True))
        p = jnp.where(mask[None], jnp.exp(lg - m_new), 0.0)
        alpha = jnp.exp(m_old - m_new)
        l_s[...] = (alpha * l_s[...].reshape(n_heads, qb, 1)
                    + jnp.sum(p, axis=-1, keepdims=True)).reshape(hq, 1)
        pv = jnp.dot(p.reshape(hq, kt).astype(bf16), v_tile, preferred_element_type=jnp.float32)
        acc_s[...] = alpha.reshape(hq, 1) * acc_s[...] + pv
        m_s[...] = m_new.reshape(hq, 1)

    def reset_state():
        m_s[...] = jnp.full((hq, 1), NEG_INF, jnp.float32)
        l_s[...] = jnp.zeros((hq, 1), jnp.float32)
        acc_s[...] = jnp.zeros((hq, HEAD_DIM), jnp.float32)

    def normalized():
        return acc_s[...] / jnp.maximum(l_s[...], 1e-30)

    qn = (qn_ref[0].reshape(hq, HEAD_DIM) * scale).astype(bf16)
    lc = _nt_dot(qn, kc_ref[0]).reshape(n_heads, qb, n_cmp_pad)
    c_end = (lax.broadcasted_iota(jnp.int32, (qb, n_cmp_pad), 1) * NSA_CMP_STRIDE
             + (NSA_CMP_BLOCK - 1))
    c_mask = c_end <= q_pos
    lc = jnp.where(c_mask[None], lc, NEG_INF)
    mc = jnp.max(lc, axis=-1, keepdims=True)
    ec = jnp.where(c_mask[None], jnp.exp(lc - mc), 0.0)
    pc = ec / jnp.maximum(jnp.sum(ec, axis=-1, keepdims=True), 1e-30)
    o_c = jnp.dot(pc.reshape(hq, n_cmp_pad).astype(bf16), vc_ref[0],
                  preferred_element_type=jnp.float32)
    out_s[...] = gate_col(0) * o_c

    p_sum = jnp.sum(pc, axis=0)
    p_hi = p_sum.astype(bf16)
    p_lo = (p_sum - p_hi.astype(jnp.float32)).astype(bf16)
    imp = (jnp.dot(p_hi, ov_ref[...], preferred_element_type=jnp.float32)
           + jnp.dot(p_lo, ov_ref[...], preferred_element_type=jnp.float32))
    blk = lax.broadcasted_iota(jnp.int32, (qb, n_slc), 1)
    cur = q_pos // NSA_SLC_BLOCK
    forced = (blk == 0) | (blk == cur) | (blk == cur - 1)
    imp = jnp.where(forced, NSA_FORCE_SCORE, imp)
    valid = blk <= cur
    work = jnp.where(valid, imp, NEG_INF)
    sel = jnp.zeros((qb, n_slc), jnp.bool_)
    for _ in range(min(NSA_TOP_N, n_slc)):
        mx = jnp.max(work, axis=-1, keepdims=True)
        first = jnp.min(jnp.where(work == mx, blk, n_slc), axis=-1, keepdims=True)
        pick = blk == first
        sel = sel | pick
        work = jnp.where(pick, -jnp.inf, work)
    sel_bf = jnp.where(sel & valid, 1.0, 0.0).astype(bf16)

    qr = (qr_ref[0].reshape(hq, HEAD_DIM) * scale).astype(bf16)
    reset_state()
    n_win_tiles = NSA_WINDOW // qb + 1
    w_first = jnp.maximum(0, n_win_tiles - 1 - j)

    def win_body(w, c):
        k0 = pl.multiple_of(s0 - NSA_WINDOW + w * qb, qb)
        k_pos = k0 + lax.broadcasted_iota(jnp.int32, (qb, qb), 1)
        dist = q_pos - k_pos
        mask = (dist >= 0) & (dist < NSA_WINDOW)
        masked_update(_nt_dot(qr, kw_ref[0, pl.ds(k0, qb), :]), mask, vw_ref[0, pl.ds(k0, qb), :])
        return c
    lax.fori_loop(w_first, n_win_tiles, win_body, 0)
    out_s[...] += gate_col(2) * normalized()

    reset_state()
    kt = NSA_KV_TILE
    blocks_per_tile = kt // NSA_SLC_BLOCK
    n_tiles = (s0 + qb - 1) // kt + 1

    def sel_body(t, c):
        k0 = pl.multiple_of(t * kt, kt)
        k_idx = lax.broadcasted_iota(jnp.int32, (n_slc, kt), 1)
        expand = (lax.broadcasted_iota(jnp.int32, (n_slc, kt), 0)
                  == t * blocks_per_tile + k_idx // NSA_SLC_BLOCK)
        picked = jnp.dot(sel_bf, jnp.where(expand, 1.0, 0.0).astype(bf16),
                         preferred_element_type=jnp.float32)
        k_pos = k0 + lax.broadcasted_iota(jnp.int32, (qb, kt), 1)
        mask = (picked > 0.5) & (k_pos <= q_pos)
        masked_update(_nt_dot(qr, ks_ref[0, pl.ds(k0, kt), :]), mask, vs_ref[0, pl.ds(k0, kt), :])
        return c
    lax.fori_loop(0, n_tiles, sel_body, 0)
    out_s[...] += gate_col(1) * normalized()

    o_ref[0] = out_s[...].reshape(n_heads, qb, HEAD_DIM)


def nsa_attention_pallas(qn, qr, kc, vc, overlap, ks, vs, kw, vw, gl):
    b, h, s, dh = qn.shape
    qb = Q_BLOCK
    n_cmp_pad = kc.shape[1]
    n_slc = overlap.shape[1]
    kern = functools.partial(_nsa_kernel, n_heads=h, qb=qb, n_cmp_pad=n_cmp_pad, n_slc=n_slc)
    q_spec = pl.BlockSpec((1, h, qb, dh), lambda i, j: (i, 0, j, 0))
    cmp_spec = pl.BlockSpec((1, n_cmp_pad, dh), lambda i, j: (i, 0, 0))
    kv_spec = pl.BlockSpec((1, s, dh), lambda i, j: (i, 0, 0))
    return pl.pallas_call(
        kern,
        grid=(b, s // qb),
        in_specs=[q_spec, q_spec, cmp_spec, cmp_spec,
                  pl.BlockSpec((n_cmp_pad, n_slc), lambda i, j: (0, 0)),
                  kv_spec, kv_spec, kv_spec, kv_spec,
                  pl.BlockSpec((1, qb, 3 * h), lambda i, j: (i, j, 0))],
        out_specs=q_spec,
        out_shape=jax.ShapeDtypeStruct((b, h, s, dh), jnp.float32),
        scratch_shapes=[pltpu.VMEM((h * qb, 1), jnp.float32),
                        pltpu.VMEM((h * qb, 1), jnp.float32),
                        pltpu.VMEM((h * qb, dh), jnp.float32),
                        pltpu.VMEM((h * qb, dh), jnp.float32)],
        compiler_params=pltpu.CompilerParams(
            dimension_semantics=("arbitrary", "arbitrary"),
            vmem_limit_bytes=40 << 20),
        name="nsa_attention",
    )(qn, qr, kc, vc, overlap, ks, vs, kw, vw, gl)


def _dilated_kernel(q_ref, k_ref, v_ref, o_ref, m_s, l_s, acc_s, *, qb):
    j = pl.program_id(1)
    s0 = j * qb
    bf16 = jnp.bfloat16
    scale = HEAD_DIM ** -0.5
    q_pos = s0 + lax.broadcasted_iota(jnp.int32, (qb, 1), 0)
    lane = lax.broadcasted_iota(jnp.int32, (qb, LANES), 1)
    first_head = lane < HEAD_DIM
    rows = DIL_HEADS_PER_GROUP * qb

    outs, lses = [], []
    for g, (window, dil) in enumerate(DIL_PAIRS):
        c0 = g * LANES
        q = q_ref[0, :, pl.ds(c0, LANES)] * scale
        q2 = jnp.concatenate([jnp.where(first_head, q, 0.0),
                              jnp.where(first_head, 0.0, q)], axis=0).astype(bf16)
        m_s[...] = jnp.full((rows, 1), NEG_INF, jnp.float32)
        l_s[...] = jnp.zeros((rows, 1), jnp.float32)
        acc_s[...] = jnp.zeros((rows, LANES), jnp.float32)
        n_tiles = window // qb + 1

        def body(w, c, c0=c0, window=window, dil=dil, q2=q2, n_tiles=n_tiles):
            k0 = pl.multiple_of(s0 - (n_tiles - 1 - w) * qb, qb)
            k_pos = k0 + lax.broadcasted_iota(jnp.int32, (qb, qb), 1)
            dist = q_pos - k_pos
            mask = (dist >= 0) & (dist <= window) & ((dist & (dil - 1)) == 0)
            logits = _nt_dot(q2, k_ref[0, pl.ds(k0, qb), pl.ds(c0, LANES)])
            lg = jnp.where(mask[None], logits.reshape(DIL_HEADS_PER_GROUP, qb, qb), NEG_INF)
            m_old = m_s[...].reshape(DIL_HEADS_PER_GROUP, qb, 1)
            m_new = jnp.maximum(m_old, jnp.max(lg, axis=-1, keepdims=True))
            p = jnp.where(mask[None], jnp.exp(lg - m_new), 0.0)
            alpha = jnp.exp(m_old - m_new)
            l_s[...] = (alpha * l_s[...].reshape(DIL_HEADS_PER_GROUP, qb, 1)
                        + jnp.sum(p, axis=-1, keepdims=True)).reshape(rows, 1)
            pv = jnp.dot(p.reshape(rows, qb).astype(bf16), v_ref[0, pl.ds(k0, qb), pl.ds(c0, LANES)],
                         preferred_element_type=jnp.float32)
            acc_s[...] = alpha.reshape(rows, 1) * acc_s[...] + pv
            m_s[...] = m_new.reshape(rows, 1)
            return c
        lax.fori_loop(jnp.maximum(0, n_tiles - 1 - j), n_tiles, body, 0)

        o2 = acc_s[...] / l_s[...]
        lse2 = m_s[...] + jnp.log(l_s[...])
        outs.append(jnp.where(first_head, o2[:qb], o2[qb:]))
        lses.append(jnp.where(first_head, lse2[:qb], lse2[qb:]))

    lse_max = functools.reduce(jnp.maximum, lses)
    ws = [jnp.exp(x - lse_max) for x in lses]
    den = functools.reduce(lambda a, b: a + b, ws)
    num = functools.reduce(lambda a, b: a + b, [w * o for w, o in zip(ws, outs)])
    o_ref[0] = num / den


def dilated_attention_pallas(q, k, v):
    b, s, w = q.shape
    qb = Q_BLOCK
    rows = DIL_HEADS_PER_GROUP * qb
    return pl.pallas_call(
        functools.partial(_dilated_kernel, qb=qb),
        grid=(b, s // qb),
        in_specs=[pl.BlockSpec((1, qb, w), lambda i, j: (i, j, 0)),
                  pl.BlockSpec((1, s, w), lambda i, j: (i, 0, 0)),
                  pl.BlockSpec((1, s, w), lambda i, j: (i, 0, 0))],
        out_specs=pl.BlockSpec((1, qb, DIL_OUT_W), lambda i, j: (i, j, 0)),
        out_shape=jax.ShapeDtypeStruct((b, s, DIL_OUT_W), jnp.float32),
        scratch_shapes=[pltpu.VMEM((rows, 1), jnp.float32),
                        pltpu.VMEM((rows, 1), jnp.float32),
                        pltpu.VMEM((rows, LANES), jnp.float32)],
        compiler_params=pltpu.CompilerParams(
            dimension_semantics=("arbitrary", "arbitrary"),
            vmem_limit_bytes=48 << 20),
        name="dilated_attention",
    )(q, k, v)


FOX_KV_TILE = 512


def _fox_kernel(q_ref, k_ref, v_ref, cq_ref, ck_ref, o_ref, m_s, l_s, acc_s, *, qb):
    j = pl.program_id(1)
    s0 = j * qb
    kt = FOX_KV_TILE
    bf16 = jnp.bfloat16
    scale = HEAD_DIM ** -0.5
    q_pos = s0 + lax.broadcasted_iota(jnp.int32, (qb, 1), 0)
    lane = lax.broadcasted_iota(jnp.int32, (qb, LANES), 1)
    first_head = lane < HEAD_DIM
    n_pairs = FOX_HEADS // 2
    rows = FOX_HEADS * qb
    n_tiles = (s0 + qb - 1) // kt + 1

    q2 = []
    for pair in range(n_pairs):
        q = q_ref[0, :, pl.ds(pair * LANES, LANES)] * scale
        q2.append(jnp.concatenate([jnp.where(first_head, q, 0.0),
                                   jnp.where(first_head, 0.0, q)], axis=0).astype(bf16))
    cq = [cq_ref[0, :, h:h + 1] for h in range(FOX_HEADS)]
    m_s[...] = jnp.full((rows, 1), NEG_INF, jnp.float32)
    l_s[...] = jnp.zeros((rows, 1), jnp.float32)
    acc_s[...] = jnp.zeros((rows, LANES), jnp.float32)

    def body(t, c):
        k0 = pl.multiple_of(t * kt, kt)
        k_pos = k0 + lax.broadcasted_iota(jnp.int32, (qb, kt), 1)
        mask = k_pos <= q_pos
        logits = jnp.concatenate(
            [_nt_dot(q2[pair], k_ref[0, pl.ds(k0, kt), pl.ds(pair * LANES, LANES)])
             for pair in range(n_pairs)], axis=0)
        bias = jnp.concatenate([cq[h] - ck_ref[0, h:h + 1, pl.ds(k0, kt)] for h in range(FOX_HEADS)],
                               axis=0)
        lg = jnp.where(mask[None], (logits + bias).reshape(FOX_HEADS, qb, kt), NEG_INF)
        m_old = m_s[...].reshape(FOX_HEADS, qb, 1)
        m_new = jnp.maximum(m_old, jnp.max(lg, axis=-1, keepdims=True))
        p = jnp.exp(lg - m_new)
        alpha = jnp.exp(m_old - m_new)
        l_s[...] = (alpha * l_s[...].reshape(FOX_HEADS, qb, 1)
                    + jnp.sum(p, axis=-1, keepdims=True)).reshape(rows, 1)
        pb = p.reshape(rows, kt).astype(bf16)
        pv = jnp.concatenate(
            [jnp.dot(pb[2 * pair * qb:2 * (pair + 1) * qb],
                     v_ref[0, pl.ds(k0, kt), pl.ds(pair * LANES, LANES)],
                     preferred_element_type=jnp.float32) for pair in range(n_pairs)], axis=0)
        acc_s[...] = alpha.reshape(rows, 1) * acc_s[...] + pv
        m_s[...] = m_new.reshape(rows, 1)
        return c
    lax.fori_loop(0, n_tiles, body, 0)

    o2 = acc_s[...] / l_s[...]
    for pair in range(n_pairs):
        r0 = 2 * pair * qb
        o_ref[0, :, pl.ds(pair * LANES, LANES)] = jnp.where(first_head, o2[r0:r0 + qb],
                                                             o2[r0 + qb:r0 + 2 * qb])


def fox_attention_pallas(q, k, v, cum):
    b, s, w = q.shape
    qb = Q_BLOCK
    return pl.pallas_call(
        functools.partial(_fox_kernel, qb=qb),
        grid=(b, s // qb),
        in_specs=[pl.BlockSpec((1, qb, w), lambda i, j: (i, j, 0)),
                  pl.BlockSpec((1, s, w), lambda i, j: (i, 0, 0)),
                  pl.BlockSpec((1, s, w), lambda i, j: (i, 0, 0)),
                  pl.BlockSpec((1, qb, FOX_HEADS), lambda i, j: (i, j, 0)),
                  pl.BlockSpec((1, FOX_HEADS, s), lambda i, j: (i, 0, 0))],
        out_specs=pl.BlockSpec((1, qb, w), lambda i, j: (i, j, 0)),
        out_shape=jax.ShapeDtypeStruct((b, s, w), jnp.float32),
        scratch_shapes=[pltpu.VMEM((FOX_HEADS * qb, 1), jnp.float32),
                        pltpu.VMEM((FOX_HEADS * qb, 1), jnp.float32),
                        pltpu.VMEM((FOX_HEADS * qb, LANES), jnp.float32)],
        compiler_params=pltpu.CompilerParams(
            dimension_semantics=("arbitrary", "arbitrary"),
            vmem_limit_bytes=40 << 20),
        name="fox_attention",
    )(q, k, v, cum, jnp.transpose(cum, (0, 2, 1)))


LRU_TILE = 512


def _lru_kernel(x_ref, gate_ref, cw_ref, cb_ref, wa_ref, ba_ref, wx_ref, bx_ref, lam_ref, o_ref,
                xbuf, a_s, u_s, h_s, hlast, *, tt):
    j = pl.program_id(1)
    c = x_ref.shape[-1]
    bf16 = jnp.bfloat16

    @pl.when(j == 0)
    def _():
        xbuf[pl.ds(0, SUBLANES), :] = jnp.zeros((SUBLANES, c), jnp.float32)
        hlast[...] = jnp.zeros((1, c), jnp.float32)

    x = x_ref[0]
    xbuf[pl.ds(SUBLANES, tt), :] = x
    xc = cb_ref[...] + cw_ref[LRU_CONV - 1:LRU_CONV, :] * x
    for back in range(1, LRU_CONV):
        xc = xc + cw_ref[LRU_CONV - 1 - back:LRU_CONV - back, :] * xbuf[pl.ds(SUBLANES - back, tt), :]
    xbuf[pl.ds(0, SUBLANES), :] = x[tt - SUBLANES:, :]

    xb = xc.astype(bf16)
    r = jax.nn.sigmoid(jnp.dot(xb, wa_ref[...], preferred_element_type=jnp.float32) + ba_ref[...])
    i_g = jax.nn.sigmoid(jnp.dot(xb, wx_ref[...], preferred_element_type=jnp.float32) + bx_ref[...])
    lam = lam_ref[...]
    softplus_neg = jnp.maximum(-lam, 0.0) + jnp.log1p(jnp.exp(-jnp.abs(lam)))
    log_a = -LRU_C * r * softplus_neg
    a_s[...] = jnp.exp(log_a)
    u_s[...] = jnp.sqrt(1.0 - jnp.exp(2.0 * log_a)) * (i_g * xc)

    def group(gi, h):
        r0 = pl.multiple_of(gi * SUBLANES, SUBLANES)
        a8 = a_s[pl.ds(r0, SUBLANES), :]
        u8 = u_s[pl.ds(r0, SUBLANES), :]
        rows = []
        for s in range(SUBLANES):
            h = a8[s:s + 1, :] * h + u8[s:s + 1, :]
            rows.append(h)
        h_s[pl.ds(r0, SUBLANES), :] = jnp.concatenate(rows, axis=0)
        return h
    hlast[...] = lax.fori_loop(0, tt // SUBLANES, group, hlast[...])
    o_ref[0] = h_s[...] * jax.nn.gelu(gate_ref[0])


def _block_diag(w):
    n, c, d = w.shape
    eye = jnp.eye(n, dtype=w.dtype)
    return (eye[:, None, :, None] * w[:, :, None, :]).reshape(n * c, n * d)


def rg_lru_pallas(x_in, gate, conv_w, conv_b, w_a, b_a, w_x, b_x, lam):
    b, s, c = x_in.shape
    tt = LRU_TILE
    row = lambda v: v.reshape(1, c)
    tok_spec = pl.BlockSpec((1, tt, c), lambda i, j: (i, j, 0))
    full = lambda shape: pl.BlockSpec(shape, lambda i, j: (0,) * len(shape))
    return pl.pallas_call(
        functools.partial(_lru_kernel, tt=tt),
        grid=(b, s // tt),
        in_specs=[tok_spec, tok_spec, full((LRU_CONV, c)), full((1, c)), full((c, c)), full((1, c)),
                  full((c, c)), full((1, c)), full((1, c))],
        out_specs=tok_spec,
        out_shape=jax.ShapeDtypeStruct((b, s, c), jnp.float32),
        scratch_shapes=[pltpu.VMEM((tt + SUBLANES, c), jnp.float32),
                        pltpu.VMEM((tt, c), jnp.float32),
                        pltpu.VMEM((tt, c), jnp.float32),
                        pltpu.VMEM((tt, c), jnp.float32),
                        pltpu.VMEM((1, c), jnp.float32)],
        compiler_params=pltpu.CompilerParams(dimension_semantics=("arbitrary", "arbitrary")),
        name="rg_lru",
    )(x_in, gate, conv_w, row(conv_b), _block_diag(w_a).astype(jnp.bfloat16), row(b_a),
      _block_diag(w_x).astype(jnp.bfloat16), row(b_x), row(lam))


def rms_norm(x, g):
    xf = x.astype(jnp.float32)
    y = xf * lax.rsqrt(jnp.mean(xf * xf, axis=-1, keepdims=True) + NORM_EPS)
    return (y * g.astype(jnp.float32)).astype(x.dtype)


def rope(x, pos):
    half = x.shape[-1] // 2
    freqs = ROPE_THETA ** (-jnp.arange(half, dtype=jnp.float32) / half)
    ang = pos.astype(jnp.float32)[:, None] * freqs[None, :]
    cos = jnp.cos(ang)[None, :, None, :]
    sin = jnp.sin(ang)[None, :, None, :]
    xf = x.astype(jnp.float32)
    x1, x2 = xf[..., :half], xf[..., half:]
    return jnp.concatenate([x1 * cos - x2 * sin, x2 * cos + x1 * sin], axis=-1).astype(x.dtype)


def _unblock(y):
    y = jnp.moveaxis(y, 0, 1)
    return y.reshape((y.shape[0], y.shape[1] * y.shape[2]) + y.shape[3:])


def masked_softmax(logits, mask):
    logits = jnp.where(mask, logits, NEG_INF)
    m = jnp.max(logits, axis=-1, keepdims=True)
    e = jnp.where(mask, jnp.exp(logits - m), 0.0)
    return e / jnp.maximum(jnp.sum(e, axis=-1, keepdims=True), 1e-30)


def fox_attention(q, k, v, f_logit, b_f):
    B, S, H, dh = q.shape
    log_f = jax.nn.log_sigmoid(f_logit.astype(jnp.float32) + b_f.astype(jnp.float32))
    cum = jnp.transpose(jnp.cumsum(log_f, axis=1), (0, 2, 1))
    scale = dh ** -0.5
    k_pos = jnp.arange(S)

    def block(i):
        s0 = i * Q_BLOCK
        qb = lax.dynamic_slice_in_dim(q, s0, Q_BLOCK, axis=1)
        cq = lax.dynamic_slice_in_dim(cum, s0, Q_BLOCK, axis=2)
        logits = jnp.einsum('bqhd,bkhd->bhqk', qb, k).astype(jnp.float32) * scale
        logits = logits + cq[..., None] - cum[:, :, None, :]
        q_pos = s0 + jnp.arange(Q_BLOCK)
        logits = jnp.where(k_pos[None, :] <= q_pos[:, None], logits, NEG_INF)
        p = jax.nn.softmax(logits, axis=-1)
        return jnp.einsum('bhqk,bkhd->bqhd', p.astype(v.dtype), v)

    return _unblock(lax.map(block, jnp.arange(S // Q_BLOCK)))


def rg_lru_branch(x_in, gate, conv_w, conv_b, w_a, b_a, w_x, b_x, lam):
    B, S, C = x_in.shape
    xc = lax.conv_general_dilated(x_in, conv_w[:, None, :], window_strides=(1,),
                                  padding=[(LRU_CONV - 1, 0)],
                                  dimension_numbers=('NWC', 'WIO', 'NWC'),
                                  feature_group_count=C) + conv_b
    xr = xc.reshape(B, S, LRU_BLOCKS, C // LRU_BLOCKS)
    r = jax.nn.sigmoid(jnp.einsum('bsnc,ncd->bsnd', xr, w_a).reshape(B, S, C) + b_a)
    i_g = jax.nn.sigmoid(jnp.einsum('bsnc,ncd->bsnd', xr, w_x).reshape(B, S, C) + b_x)
    log_a = -LRU_C * r.astype(jnp.float32) * jax.nn.softplus(-lam.astype(jnp.float32))
    a = jnp.exp(log_a)
    u = jnp.sqrt(-jnp.expm1(2.0 * log_a)) * (i_g * xc).astype(jnp.float32)

    def combine(left, right):
        a_l, b_l = left
        a_r, b_r = right
        return a_l * a_r, a_r * b_l + b_r

    _, h = lax.associative_scan(combine, (a, u), axis=1)
    return h.astype(x_in.dtype) * jax.nn.gelu(gate)


def dilated_attention(q, k, v):
    B, S, H, dh = q.shape
    G, Hg = DIL_GROUPS, DIL_HEADS_PER_GROUP
    qg = q.reshape(B, S, G, Hg, dh)
    kg = k.reshape(B, S, G, Hg, dh)
    vg = v.reshape(B, S, G, Hg, dh)
    dil = jnp.array([d for _, d in DIL_PAIRS], dtype=jnp.int32)
    steps = jnp.arange(DIL_KEYS)
    g_idx = jnp.arange(G)[:, None, None]
    scale = dh ** -0.5

    def block(i):
        s0 = i * Q_BLOCK
        q_pos = s0 + jnp.arange(Q_BLOCK)
        k_pos = q_pos[None, :, None] - dil[:, None, None] * steps[None, None, :]
        valid = k_pos >= 0
        k_idx = jnp.maximum(k_pos, 0)
        kb = kg[:, k_idx, g_idx]
        vb = vg[:, k_idx, g_idx]
        qb = lax.dynamic_slice_in_dim(qg, s0, Q_BLOCK, axis=1)
        logits = jnp.einsum('bqghd,bgqkhd->bgqhk', qb, kb).astype(jnp.float32) * scale
        logits = jnp.where(valid[None, :, :, None, :], logits, NEG_INF)
        m = jnp.max(logits, axis=-1, keepdims=True)
        e = jnp.exp(logits - m)
        l = jnp.sum(e, axis=-1)
        o = jnp.einsum('bgqhk,bgqkhd->bgqhd', e, vb.astype(jnp.float32)) / l[..., None]
        lse = m[..., 0] + jnp.log(l)
        w = jax.nn.softmax(lse, axis=1)
        return jnp.einsum('bgqh,bgqhd->bqhd', w, o).astype(v.dtype)

    return _unblock(lax.map(block, jnp.arange(S // Q_BLOCK)))


def nsa_attention(q, k_cmp, v_cmp, k_slc, v_slc, k_win, v_win, gate_logits,
                  pe_k, pe_v, w1_k, w2_k, w1_v, w2_v):
    B, S, H, dh = q.shape
    Hkv = k_cmp.shape[2]
    G = H // Hkv
    pos = jnp.arange(S)
    q_rot = rope(q, pos)
    k_slc = rope(k_slc, pos)
    k_win = rope(k_win, pos)
    gates = jax.nn.sigmoid(gate_logits).reshape(B, S, H, 3)
    scale = dh ** -0.5

    n_cmp = (S - NSA_CMP_BLOCK) // NSA_CMP_STRIDE + 1
    cmp_idx = jnp.arange(n_cmp)[:, None] * NSA_CMP_STRIDE + jnp.arange(NSA_CMP_BLOCK)[None, :]

    def compress(kv, pe, w1, w2):
        blocks = kv[:, cmp_idx] + pe[None, None, :, None, :]
        flat = jnp.moveaxis(blocks, 3, 2).reshape(B, n_cmp, Hkv, NSA_CMP_BLOCK * dh)
        return jax.nn.gelu(flat @ w1) @ w2

    kc = compress(k_cmp, pe_k, w1_k, w2_k)
    vc = compress(v_cmp, pe_v, w1_v, w2_v)
    cmp_start = jnp.arange(n_cmp) * NSA_CMP_STRIDE
    cmp_end = cmp_start + NSA_CMP_BLOCK - 1

    n_slc = S // NSA_SLC_BLOCK
    top_n = min(NSA_TOP_N, n_slc)
    slc_start = jnp.arange(n_slc) * NSA_SLC_BLOCK
    overlap = ((cmp_start[:, None] < slc_start[None, :] + NSA_SLC_BLOCK)
               & (cmp_start[:, None] + NSA_CMP_BLOCK > slc_start[None, :])).astype(jnp.float32)
    ks_blocks = jnp.transpose(k_slc.reshape(B, n_slc, NSA_SLC_BLOCK, Hkv, dh), (0, 3, 1, 2, 4))
    vs_blocks = jnp.transpose(v_slc.reshape(B, n_slc, NSA_SLC_BLOCK, Hkv, dh), (0, 3, 1, 2, 4))
    pad = ((0, 0), (NSA_WINDOW, 0), (0, 0), (0, 0))
    k_win_pad = jnp.pad(k_win, pad)
    v_win_pad = jnp.pad(v_win, pad)
    b_idx = jnp.arange(B)[:, None, None, None]
    h_idx = jnp.arange(Hkv)[None, :, None, None]
    blk_ids = jnp.arange(n_slc)

    def block(i):
        s0 = i * Q_BLOCK
        q_pos = s0 + jnp.arange(Q_BLOCK)
        qb = lax.dynamic_slice_in_dim(q, s0, Q_BLOCK, axis=1).reshape(B, Q_BLOCK, Hkv, G, dh)
        qrb = lax.dynamic_slice_in_dim(q_rot, s0, Q_BLOCK, axis=1).reshape(B, Q_BLOCK, Hkv, G, dh)
        lc = jnp.einsum('bqkgd,bckd->bkgqc', qb, kc).astype(jnp.float32) * scale
        p_c = masked_softmax(lc, (cmp_end[None, :] <= q_pos[:, None])[None, None, None])
        o_c = jnp.einsum('bkgqc,bckd->bqkgd', p_c, vc.astype(jnp.float32))
        imp = jnp.einsum('bkgqc,cn->bkqn', p_c, overlap)
        cur = q_pos // NSA_SLC_BLOCK
        forced = ((blk_ids[None, :] == 0) | (blk_ids[None, :] == cur[:, None])
                  | (blk_ids[None, :] == cur[:, None] - 1))
        imp = jnp.where(forced, NSA_FORCE_SCORE, imp)
        imp = jnp.where(blk_ids[None, :] <= cur[:, None], imp, NEG_INF)
        top_val, top_idx = lax.top_k(imp, top_n)
        sel_ok = top_val > 0.5 * NEG_INF
        ksel = ks_blocks[b_idx, h_idx, top_idx]
        vsel = vs_blocks[b_idx, h_idx, top_idx]
        sel_pos = top_idx[..., None] * NSA_SLC_BLOCK + jnp.arange(NSA_SLC_BLOCK)
        sel_mask = sel_ok[..., None] & (sel_pos <= q_pos[None, None, :, None, None])
        n_sel_keys = top_n * NSA_SLC_BLOCK
        ls = jnp.einsum('bqkgd,bkqnld->bkgqnl', qrb, ksel).astype(jnp.float32) * scale
        ls = ls.reshape(B, Hkv, G, Q_BLOCK, n_sel_keys)
        p_s = masked_softmax(ls, sel_mask.reshape(B, Hkv, 1, Q_BLOCK, n_sel_keys))
        o_s = jnp.einsum('bkgqm,bkqmd->bqkgd', p_s,
                         vsel.reshape(B, Hkv, Q_BLOCK, n_sel_keys, dh).astype(jnp.float32))
        kw = lax.dynamic_slice_in_dim(k_win_pad, s0, NSA_WINDOW + Q_BLOCK, axis=1)
        vw = lax.dynamic_slice_in_dim(v_win_pad, s0, NSA_WINDOW + Q_BLOCK, axis=1)
        w_pos = s0 - NSA_WINDOW + jnp.arange(NSA_WINDOW + Q_BLOCK)
        dist = q_pos[:, None] - w_pos[None, :]
        w_mask = (dist >= 0) & (dist < NSA_WINDOW) & (w_pos[None, :] >= 0)
        lw = jnp.einsum('bqkgd,bwkd->bkgqw', qrb, kw).astype(jnp.float32) * scale
        p_w = masked_softmax(lw, w_mask)
        o_w = jnp.einsum('bkgqw,bwkd->bqkgd', p_w, vw.astype(jnp.float32))
        gb = lax.dynamic_slice_in_dim(gates, s0, Q_BLOCK, axis=1).reshape(B, Q_BLOCK, Hkv, G, 3)
        gb = gb.astype(jnp.float32)
        out = gb[..., 0:1] * o_c + gb[..., 1:2] * o_s + gb[..., 2:3] * o_w
        return out.reshape(B, Q_BLOCK, H, dh).astype(q.dtype)

    return _unblock(lax.map(block, jnp.arange(S // Q_BLOCK)))


def nsa_branch(nq, nkc, nvc, nks, nvs, nkw, nvw, ng, pe_k, pe_v, w1_k, w2_k, w1_v, w2_v):
    B, S, _ = nq.shape
    dh = HEAD_DIM
    bf16 = jnp.bfloat16
    pos = jnp.arange(S)
    q4 = nq.reshape(B, S, NSA_HEADS, dh)
    qn = jnp.transpose(q4, (0, 2, 1, 3))
    qr = jnp.transpose(rope(q4, pos), (0, 2, 1, 3))
    ks = rope(nks.reshape(B, S, 1, dh), pos).reshape(B, S, dh).astype(bf16)
    kw = rope(nkw.reshape(B, S, 1, dh), pos).reshape(B, S, dh).astype(bf16)

    n_cmp = (S - NSA_CMP_BLOCK) // NSA_CMP_STRIDE + 1
    n_cmp_pad = S // NSA_CMP_STRIDE
    n_slc = S // NSA_SLC_BLOCK

    def compress(kv, pe, w1, w2):
        chunks = kv.reshape(B, n_cmp_pad, NSA_CMP_STRIDE * dh)
        flat = jnp.concatenate([chunks[:, :-1], chunks[:, 1:]], axis=-1) + pe.reshape(-1)
        out = jax.nn.gelu(flat @ w1) @ w2
        return jnp.pad(out, ((0, 0), (0, n_cmp_pad - n_cmp), (0, 0))).astype(bf16)

    kc = compress(nkc, pe_k, w1_k, w2_k)
    vc = compress(nvc, pe_v, w1_v, w2_v)
    cmp_start = jnp.arange(n_cmp_pad) * NSA_CMP_STRIDE
    slc_start = jnp.arange(n_slc) * NSA_SLC_BLOCK
    overlap = ((cmp_start[:, None] < slc_start[None, :] + NSA_SLC_BLOCK)
               & (cmp_start[:, None] + NSA_CMP_BLOCK > slc_start[None, :])
               & (jnp.arange(n_cmp_pad)[:, None] < n_cmp)).astype(bf16)
    out = nsa_attention_pallas(qn, qr, kc, vc, overlap, ks, nvs.astype(bf16), kw,
                               nvw.astype(bf16), ng)
    return jnp.transpose(out, (0, 2, 1, 3)).reshape(B, S, NSA_W)


def peer_ffn(x_res, h2, w_q, sub_keys, u, v):
    expert, gt = peer_route(h2, w_q, sub_keys)
    uv = jnp.concatenate([u, v], axis=1)
    return peer_mix(h2, x_res, expert, gt, uv)


def kernel(x, norm1_g, w_in, fox_b_f, lru_conv_w, lru_conv_b, lru_w_a, lru_b_a, lru_w_x, lru_b_x, lru_lambda, nsa_pe_k, nsa_pe_v, nsa_w1_k, nsa_w2_k, nsa_w1_v, nsa_w2_v, w_br_fox, w_br_lru, w_br_dil, w_br_nsa, w_out, norm2_g, peer_w_q, peer_sub_keys, peer_u, peer_v, final_g):
    B, S, D = x.shape
    pos = jnp.arange(S)
    split_points = np.cumsum(IN_SPLITS)[:-1].tolist()

    def heads(t, n):
        return t.reshape(B, S, n, HEAD_DIM)

    for l in range(DEPTH):
        h = rms_norm(x, norm1_g[l])
        proj = h @ w_in[l]
        (fq, fk, fv, ff, lx, lg, dq, dk, dv, nq, nkc, nvc, nks, nvs, nkw, nvw, ng, mg) = \
            jnp.split(proj, split_points, axis=-1)

        cum = jnp.cumsum(jax.nn.log_sigmoid(ff + fox_b_f[l]), axis=1)
        o_fox = fox_attention_pallas(fq, fk.astype(jnp.bfloat16), fv.astype(jnp.bfloat16), cum)
        o_lru = rg_lru_pallas(lx, lg, lru_conv_w[l], lru_conv_b[l], lru_w_a[l], lru_b_a[l],
                              lru_w_x[l], lru_b_x[l], lru_lambda[l])
        o_dil = dilated_attention_pallas(
            rope(heads(dq, DIL_HEADS), pos).reshape(B, S, DIL_W),
            rope(heads(dk, DIL_HEADS), pos).reshape(B, S, DIL_W).astype(jnp.bfloat16),
            dv.astype(jnp.bfloat16))
        o_nsa = nsa_branch(nq, nkc, nvc, nks, nvs, nkw, nvw, ng,
                           nsa_pe_k[l], nsa_pe_v[l], nsa_w1_k[l], nsa_w2_k[l],
                           nsa_w1_v[l], nsa_w2_v[l])

        gates = jax.nn.sigmoid(mg).reshape(B, S, N_BRANCHES, D)
        merged = (gates[:, :, 0] * (o_fox @ w_br_fox[l])
                  + gates[:, :, 1] * (o_lru @ w_br_lru[l])
                  + gates[:, :, 2] * (o_dil @ w_br_dil[l])
                  + gates[:, :, 3] * (o_nsa @ w_br_nsa[l]))
        x = x + merged @ w_out[l]

        h2 = rms_norm(x, norm2_g[l])
        x = peer_ffn(x.reshape(B * S, D), h2.reshape(B * S, D), peer_w_q[l], peer_sub_keys[l],
                     peer_u[l], peer_v[l]).reshape(B, S, D)

    return rms_norm_pallas(x.reshape(B * S, D), final_g).reshape(B, S, D)
```

```python
import functools

import jax
import jax.numpy as jnp
import numpy as np
from jax import lax
from jax.experimental import pallas as pl
from jax.experimental.pallas import tpu as pltpu

D_MODEL = 1024
DEPTH = 2
HEAD_DIM = 64
Q_BLOCK = 128
ROPE_THETA = 10000.0
NORM_EPS = 1e-6
NEG_INF = -1e30
N_BRANCHES = 4
FOX_HEADS = 4
FOX_W = FOX_HEADS * HEAD_DIM
LRU_WIDTH = 256
LRU_BLOCKS = 4
LRU_CONV = 4
LRU_C = 8.0
DIL_PAIRS = ((128, 1), (512, 4), (2048, 16))
DIL_GROUPS = len(DIL_PAIRS)
DIL_HEADS_PER_GROUP = 2
DIL_HEADS = DIL_GROUPS * DIL_HEADS_PER_GROUP
DIL_W = DIL_HEADS * HEAD_DIM
DIL_OUT_W = DIL_HEADS_PER_GROUP * HEAD_DIM
DIL_KEYS = DIL_PAIRS[0][0] // DIL_PAIRS[0][1] + 1
NSA_HEADS = 4
NSA_KV_HEADS = 1
NSA_W = NSA_HEADS * HEAD_DIM
NSA_KV_W = NSA_KV_HEADS * HEAD_DIM
NSA_CMP_BLOCK = 32
NSA_CMP_STRIDE = 16
NSA_SLC_BLOCK = 64
NSA_TOP_N = 16
NSA_WINDOW = 512
NSA_FORCE_SCORE = 1e6
PEER_HEADS = 8
PEER_N_KEYS = 128
PEER_TOPK = 16
PEER_D_KEY = 256
PEER_BLOCK = 128

IN_SPLITS = (FOX_W, FOX_W, FOX_W, FOX_HEADS,
             LRU_WIDTH, LRU_WIDTH,
             DIL_W, DIL_W, DIL_W,
             NSA_W, NSA_KV_W, NSA_KV_W, NSA_KV_W, NSA_KV_W, NSA_KV_W, NSA_KV_W, NSA_HEADS * 3,
             N_BRANCHES * D_MODEL)


def _rms_norm_kernel(x_ref, g_ref, o_ref):
    x = x_ref[...]
    ms = jnp.mean(x * x, axis=-1, keepdims=True)
    o_ref[...] = x * lax.rsqrt(ms + NORM_EPS) * g_ref[...]


def rms_norm_pallas(x, g, rows=512):
    t, d = x.shape
    return pl.pallas_call(
        _rms_norm_kernel,
        grid=(t // rows,),
        in_specs=[pl.BlockSpec((rows, d), lambda i: (i, 0)),
                  pl.BlockSpec((1, d), lambda i: (0, 0))],
        out_specs=pl.BlockSpec((rows, d), lambda i: (i, 0)),
        out_shape=jax.ShapeDtypeStruct((t, d), x.dtype),
        name="rms_norm",
    )(x, g.reshape(1, d))


def _gelu_tanh(x):
    return 0.5 * x * (1.0 + jnp.tanh(0.7978845608028654 * (x + 0.044715 * x * x * x)))


PEER_TOKENS_PER_STEP = 8
LANES = 128
SUBLANES = 8


def _peer_mix_kernel(idx_hbm, x_ref, gt_ref, res_ref, uv_hbm, o_ref,
                     idx_smem, buf, idx_sem, row_sem, *, tb, n_sel, d):
    i = pl.program_id(0)
    n = pl.num_programs(0)
    rows = tb * n_sel
    n_chunks = d // LANES
    groups = n_sel // SUBLANES

    def idx_copy(step, s):
        return pltpu.make_async_copy(idx_hbm.at[step], idx_smem.at[pl.ds(s * rows, rows)],
                                     idx_sem.at[s])

    def row_copy(s, j, row):
        return pltpu.make_async_copy(uv_hbm.at[row], buf.at[s, j // SUBLANES, pl.ds(j % SUBLANES, 1)],
                                     row_sem.at[s])

    def slot_wait(s):
        pltpu.make_async_copy(buf.at[s], buf.at[s], row_sem.at[s]).wait()

    @pl.when(i == 0)
    def _():
        idx_copy(0, 0).start()
        idx_copy(0, 0).wait()

        def body(jj, c):
            for k in range(SUBLANES):
                row = idx_smem[jj * SUBLANES + k]
                pltpu.make_async_copy(uv_hbm.at[row], buf.at[0, jj, pl.ds(k, 1)],
                                      row_sem.at[0]).start(priority=k % 2)
            return c
        lax.fori_loop(0, rows // SUBLANES, body, 0)
        idx_copy(1, 1).start()

    def step(cur):
        nxt = 1 - cur
        idx_copy(i + 1, nxt).wait()

        @pl.when(i + 2 <= n)
        def _():
            idx_copy(i + 2, cur).start()

        slot_wait(cur)

        def issue(t, part):
            lo = part * groups // (2 * n_chunks)
            hi = (part + 1) * groups // (2 * n_chunks)
            for j in range(t * n_sel + lo * SUBLANES, t * n_sel + hi * SUBLANES):
                row_copy(nxt, j, idx_smem[nxt * rows + j]).start(priority=j % 2)

        def sel_rows(t, lane0):
            tile = buf[cur, pl.ds(t * groups, groups), :, pl.ds(lane0, LANES)]
            return tile.reshape(n_sel, LANES)

        for t in range(tb):
            acc = jnp.zeros((n_sel, LANES), jnp.float32)
            for c in range(n_chunks):
                issue(t, c)
                xc = x_ref[pl.ds(t, 1), pl.ds(c * LANES, LANES)]
                acc = acc + sel_rows(t, c * LANES) * xc
            s = jnp.sum(acc, axis=-1, keepdims=True)
            w = gt_ref[0, :, pl.ds(t, 1)] * _gelu_tanh(s)
            wb = jnp.broadcast_to(w, (n_sel, LANES))
            for c in range(n_chunks):
                issue(t, n_chunks + c)
                vc = sel_rows(t, d + c * LANES)
                oc = jnp.sum(wb * vc, axis=0, keepdims=True)
                o_ref[pl.ds(t, 1), pl.ds(c * LANES, LANES)] = (
                    res_ref[pl.ds(t, 1), pl.ds(c * LANES, LANES)] + oc)

        @pl.when(i == n - 1)
        def _():
            slot_wait(nxt)

    @pl.when(i % 2 == 0)
    def _():
        step(0)

    @pl.when(i % 2 == 1)
    def _():
        step(1)


def peer_mix(x, res, idx, gt_blocks, uv):
    t, d = x.shape
    n_sel = idx.shape[1]
    tb = PEER_TOKENS_PER_STEP
    steps = t // tb
    spare = (jnp.arange(tb * n_sel, dtype=jnp.int32) % uv.shape[0]).reshape(1, tb * n_sel)
    idx_steps = jnp.concatenate([idx.reshape(steps, tb * n_sel), spare], axis=0)
    nb, _, tq = gt_blocks.shape
    gt = jnp.transpose(gt_blocks.reshape(nb, n_sel, tq // tb, tb), (0, 2, 1, 3)).reshape(steps, n_sel, tb)
    kern = functools.partial(_peer_mix_kernel, tb=tb, n_sel=n_sel, d=d)
    return pl.pallas_call(
        kern,
        grid=(steps,),
        in_specs=[pl.BlockSpec(memory_space=pl.ANY),
                  pl.BlockSpec((tb, d), lambda i: (i, 0)),
                  pl.BlockSpec((1, n_sel, tb), lambda i: (i, 0, 0)),
                  pl.BlockSpec((tb, d), lambda i: (i, 0)),
                  pl.BlockSpec(memory_space=pl.ANY)],
        out_specs=pl.BlockSpec((tb, d), lambda i: (i, 0)),
        out_shape=jax.ShapeDtypeStruct((t, d), jnp.float32),
        scratch_shapes=[pltpu.SMEM((2 * tb * n_sel,), jnp.int32),
                        pltpu.VMEM((2, tb * n_sel // SUBLANES, SUBLANES, 2 * d), jnp.float32),
                        pltpu.SemaphoreType.DMA((2,)),
                        pltpu.SemaphoreType.DMA((2,))],
        compiler_params=pltpu.CompilerParams(
            dimension_semantics=("arbitrary",),
            vmem_limit_bytes=2 * tb * n_sel * 2 * d * 4 + (8 << 20)),
        name="peer_mix",
    )(idx_steps, x, gt, res, uv.reshape(uv.shape[0], 1, 2 * d))


def _topk_rows(work, k):
    n = work.shape[0]
    row = lax.broadcasted_iota(jnp.int32, work.shape, 0)
    vals, idxs = [], []
    for _ in range(k):
        mx = jnp.max(work, axis=0, keepdims=True)
        pos = jnp.min(jnp.where(work == mx, row, n), axis=0, keepdims=True)
        work = jnp.where(row == pos, -jnp.inf, work)
        vals.append(mx)
        idxs.append(pos)
    return jnp.concatenate(vals, axis=0), jnp.concatenate(idxs, axis=0)


def _peer_route_kernel(x_ref, wq_ref, keys_ref, idx_ref, gt_ref, q_s, idx_s, g_s, *, tq):
    bf16 = jnp.bfloat16
    k = PEER_TOPK
    half = PEER_D_KEY // 2
    q = jnp.dot(x_ref[...].astype(bf16), wq_ref[...], preferred_element_type=jnp.float32)
    for c in range(2 * PEER_HEADS):
        q_s[c] = q[:, c * half:(c + 1) * half].astype(bf16)

    def head_body(h, carry):
        tops = []
        for p in range(2):
            scores = _nt_dot(keys_ref[p], q_s[2 * h + p])
            tops.append(_topk_rows(scores, k))
        (v0, i0), (v1, i1) = tops
        keep = [k // (a + 1) for a in range(k)]
        n_cand = sum(keep)
        n_pad = -n_cand % SUBLANES
        cand = jnp.concatenate([v0[a:a + 1, :] + v1[:keep[a]] for a in range(k)]
                               + [jnp.full((n_pad, tq), -jnp.inf, jnp.float32)], axis=0)
        cand_idx = jnp.concatenate([i0[a:a + 1, :] * PEER_N_KEYS + i1[:keep[a]] for a in range(k)]
                                   + [jnp.full((n_pad, tq), -1, jnp.int32)], axis=0)
        flat = lax.broadcasted_iota(jnp.int32, cand.shape, 0)
        top_s, experts = [], []
        for _ in range(k):
            mx = jnp.max(cand, axis=0, keepdims=True)
            pos = jnp.min(jnp.where(cand == mx, flat, n_cand + n_pad), axis=0, keepdims=True)
            pick = flat == pos
            experts.append(jnp.max(jnp.where(pick, cand_idx, -1), axis=0, keepdims=True))
            cand = jnp.where(pick, -jnp.inf, cand)
            top_s.append(mx)
        top_s = jnp.concatenate(top_s, axis=0)
        ex = jnp.exp(top_s - jnp.max(top_s, axis=0, keepdims=True))
        r0 = pl.multiple_of(h * k, k)
        g_s[pl.ds(r0, k), :] = ex / jnp.sum(ex, axis=0, keepdims=True)
        idx_s[pl.ds(r0, k), :] = jnp.concatenate(experts, axis=0)
        return carry
    lax.fori_loop(0, PEER_HEADS, head_body, 0)

    idx_ref[...] = jnp.transpose(idx_s[...])
    gt_ref[0] = g_s[...]


def peer_route(h2, w_q, sub_keys):
    t, d = h2.shape
    tq = PEER_BLOCK
    n_sel = PEER_HEADS * PEER_TOPK
    half = PEER_D_KEY // 2
    return pl.pallas_call(
        functools.partial(_peer_route_kernel, tq=tq),
        grid=(t // tq,),
        in_specs=[pl.BlockSpec((tq, d), lambda i: (i, 0)),
                  pl.BlockSpec((d, PEER_HEADS * PEER_D_KEY), lambda i: (0, 0)),
                  pl.BlockSpec((2, PEER_N_KEYS, half), lambda i: (0, 0, 0))],
        out_specs=[pl.BlockSpec((tq, n_sel), lambda i: (i, 0)),
                   pl.BlockSpec((1, n_sel, tq), lambda i: (i, 0, 0))],
        out_shape=[jax.ShapeDtypeStruct((t, n_sel), jnp.int32),
                   jax.ShapeDtypeStruct((t // tq, n_sel, tq), jnp.float32)],
        scratch_shapes=[pltpu.VMEM((2 * PEER_HEADS, tq, half), jnp.bfloat16),
                        pltpu.VMEM((n_sel, tq), jnp.int32),
                        pltpu.VMEM((n_sel, tq), jnp.float32)],
        compiler_params=pltpu.CompilerParams(dimension_semantics=("arbitrary",),
                                             vmem_limit_bytes=32 << 20),
        name="peer_route",
    )(h2, w_q.astype(jnp.bfloat16), sub_keys.astype(jnp.bfloat16))


NSA_KV_TILE = 512


def _nt_dot(a, b):
    return lax.dot_general(a, b, (((1,), (1,)), ((), ())), preferred_element_type=jnp.float32)


def _nsa_kernel(qn_ref, qr_ref, kc_ref, vc_ref, ov_ref, ks_ref, vs_ref, kw_ref, vw_ref, gl_ref,
                o_ref, m_s, l_s, acc_s, out_s, *, n_heads, qb, n_cmp_pad, n_slc):
    j = pl.program_id(1)
    s0 = j * qb
    hq = n_heads * qb
    bf16 = jnp.bfloat16
    scale = HEAD_DIM ** -0.5
    q_pos = s0 + lax.broadcasted_iota(jnp.int32, (qb, 1), 0)

    gates = jax.nn.sigmoid(gl_ref[0])

    def gate_col(br):
        return jnp.concatenate([gates[:, 3 * h + br:3 * h + br + 1] for h in range(n_heads)], axis=0)

    def masked_update(logits, mask, v_tile):
        kt = logits.shape[-1]
        lg = jnp.where(mask[None], logits.reshape(n_heads, qb, kt), NEG_INF)
        m_old = m_s[...].reshape(n_heads, qb, 1)
        m_new = jnp.maximum(m_old, jnp.max(lg, axis=-1, keepdims=True))
        p = jnp.where(mask[None], jnp.exp(lg - m_new), 0.0)
        alpha = jnp.exp(m_old - m_new)
        l_s[...] = (alpha * l_s[...].reshape(n_heads, qb, 1)
                    + jnp.sum(p, axis=-1, keepdims=True)).reshape(hq, 1)
        pv = jnp.dot(p.reshape(hq, kt).astype(bf16), v_tile, preferred_element_type=jnp.float32)
        acc_s[...] = alpha.reshape(hq, 1) * acc_s[...] + pv
        m_s[...] = m_new.reshape(hq, 1)

    def reset_state():
        m_s[...] = jnp.full((hq, 1), NEG_INF, jnp.float32)
        l_s[...] = jnp.zeros((hq, 1), jnp.float32)
        acc_s[...] = jnp.zeros((hq, HEAD_DIM), jnp.float32)

    def normalized():
        return acc_s[...] / jnp.maximum(l_s[...], 1e-30)

    qn = (qn_ref[0].reshape(hq, HEAD_DIM) * scale).astype(bf16)
    lc = _nt_dot(qn, kc_ref[0]).reshape(n_heads, qb, n_cmp_pad)
    c_end = (lax.broadcasted_iota(jnp.int32, (qb, n_cmp_pad), 1) * NSA_CMP_STRIDE
             + (NSA_CMP_BLOCK - 1))
    c_mask = c_end <= q_pos
    lc = jnp.where(c_mask[None], lc, NEG_INF)
    mc = jnp.max(lc, axis=-1, keepdims=True)
    ec = jnp.where(c_mask[None], jnp.exp(lc - mc), 0.0)
    pc = ec / jnp.maximum(jnp.sum(ec, axis=-1, keepdims=True), 1e-30)
    o_c = jnp.dot(pc.reshape(hq, n_cmp_pad).astype(bf16), vc_ref[0],
                  preferred_element_type=jnp.float32)
    out_s[...] = gate_col(0) * o_c

    p_sum = jnp.sum(pc, axis=0)
    p_hi = p_sum.astype(bf16)
    p_lo = (p_sum - p_hi.astype(jnp.float32)).astype(bf16)
    imp = (jnp.dot(p_hi, ov_ref[...], preferred_element_type=jnp.float32)
           + jnp.dot(p_lo, ov_ref[...], preferred_element_type=jnp.float32))
    blk = lax.broadcasted_iota(jnp.int32, (qb, n_slc), 1)
    cur = q_pos // NSA_SLC_BLOCK
    forced = (blk == 0) | (blk == cur) | (blk == cur - 1)
    imp = jnp.where(forced, NSA_FORCE_SCORE, imp)
    valid = blk <= cur
    work = jnp.transpose(jnp.where(valid, imp, NEG_INF))
    blk_t = lax.broadcasted_iota(jnp.int32, (n_slc, qb), 0)
    sel_t = jnp.zeros((n_slc, qb), jnp.float32)
    for _ in range(min(NSA_TOP_N, n_slc)):
        mx = jnp.max(work, axis=0, keepdims=True)
        first = jnp.min(jnp.where(work == mx, blk_t, n_slc), axis=0, keepdims=True)
        pick = blk_t == first
        sel_t = jnp.where(pick, 1.0, sel_t)
        work = jnp.where(pick, -jnp.inf, work)
    sel_bf = jnp.where(valid, jnp.transpose(sel_t), 0.0).astype(bf16)

    qr = (qr_ref[0].reshape(hq, HEAD_DIM) * scale).astype(bf16)
    reset_state()
    w_len = NSA_WINDOW + qb
    w0 = pl.multiple_of(jnp.maximum(s0 - NSA_WINDOW, 0), qb)
    dist = q_pos - (w0 + lax.broadcasted_iota(jnp.int32, (qb, w_len), 1))
    masked_update(_nt_dot(qr, kw_ref[0, pl.ds(w0, w_len), :]), (dist >= 0) & (dist < NSA_WINDOW),
                  vw_ref[0, pl.ds(w0, w_len), :])
    out_s[...] += gate_col(2) * normalized()

    reset_state()
    kt = NSA_KV_TILE
    blocks_per_tile = kt // NSA_SLC_BLOCK
    n_tiles = (s0 + qb - 1) // kt + 1

    def sel_body(t, c):
        k0 = pl.multiple_of(t * kt, kt)
        k_idx = lax.broadcasted_iota(jnp.int32, (n_slc, kt), 1)
        expand = (lax.broadcasted_iota(jnp.int32, (n_slc, kt), 0)
                  == t * blocks_per_tile + k_idx // NSA_SLC_BLOCK)
        picked = jnp.dot(sel_bf, jnp.where(expand, 1.0, 0.0).astype(bf16),
                         preferred_element_type=jnp.float32)
        k_pos = k0 + lax.broadcasted_iota(jnp.int32, (qb, kt), 1)
        mask = (picked > 0.5) & (k_pos <= q_pos)
        masked_update(_nt_dot(qr, ks_ref[0, pl.ds(k0, kt), :]), mask, vs_ref[0, pl.ds(k0, kt), :])
        return c
    lax.fori_loop(0, n_tiles, sel_body, 0)
    out_s[...] += gate_col(1) * normalized()

    o_ref[0] = out_s[...].reshape(n_heads, qb, HEAD_DIM)


def nsa_attention_pallas(qn, qr, kc, vc, overlap, ks, vs, kw, vw, gl):
    b, h, s, dh = qn.shape
    qb = Q_BLOCK
    n_cmp_pad = kc.shape[1]
    n_slc = overlap.shape[1]
    kern = functools.partial(_nsa_kernel, n_heads=h, qb=qb, n_cmp_pad=n_cmp_pad, n_slc=n_slc)
    q_spec = pl.BlockSpec((1, h, qb, dh), lambda i, j: (i, 0, j, 0))
    cmp_spec = pl.BlockSpec((1, n_cmp_pad, dh), lambda i, j: (i, 0, 0))
    kv_spec = pl.BlockSpec((1, s, dh), lambda i, j: (i, 0, 0))
    return pl.pallas_call(
        kern,
        grid=(b, s // qb),
        in_specs=[q_spec, q_spec, cmp_spec, cmp_spec,
                  pl.BlockSpec((n_cmp_pad, n_slc), lambda i, j: (0, 0)),
                  kv_spec, kv_spec, kv_spec, kv_spec,
                  pl.BlockSpec((1, qb, 3 * h), lambda i, j: (i, j, 0))],
        out_specs=q_spec,
        out_shape=jax.ShapeDtypeStruct((b, h, s, dh), jnp.float32),
        scratch_shapes=[pltpu.VMEM((h * qb, 1), jnp.float32),
                        pltpu.VMEM((h * qb, 1), jnp.float32),
                        pltpu.VMEM((h * qb, dh), jnp.float32),
                        pltpu.VMEM((h * qb, dh), jnp.float32)],
        compiler_params=pltpu.CompilerParams(
            dimension_semantics=("arbitrary", "arbitrary"),
            vmem_limit_bytes=40 << 20),
        name="nsa_attention",
    )(qn, qr, kc, vc, overlap, ks, vs, kw, vw, gl)


def _dilated_kernel(q_ref, k_ref, v_ref, o_ref, m_s, l_s, acc_s, *, qb):
    j = pl.program_id(1)
    s0 = j * qb
    bf16 = jnp.bfloat16
    scale = HEAD_DIM ** -0.5
    q_pos = s0 + lax.broadcasted_iota(jnp.int32, (qb, 1), 0)
    lane = lax.broadcasted_iota(jnp.int32, (qb, LANES), 1)
    first_head = lane < HEAD_DIM
    rows = DIL_HEADS_PER_GROUP * qb

    outs, lses = [], []
    for g, (window, dil) in enumerate(DIL_PAIRS):
        c0 = g * LANES
        q = q_ref[0, :, pl.ds(c0, LANES)] * scale
        q2 = jnp.concatenate([jnp.where(first_head, q, 0.0),
                              jnp.where(first_head, 0.0, q)], axis=0).astype(bf16)
        m_s[...] = jnp.full((rows, 1), NEG_INF, jnp.float32)
        l_s[...] = jnp.zeros((rows, 1), jnp.float32)
        acc_s[...] = jnp.zeros((rows, LANES), jnp.float32)
        n_tiles = window // qb + 1

        def body(w, c, c0=c0, window=window, dil=dil, q2=q2, n_tiles=n_tiles):
            k0 = pl.multiple_of(s0 - (n_tiles - 1 - w) * qb, qb)
            k_pos = k0 + lax.broadcasted_iota(jnp.int32, (qb, qb), 1)
            dist = q_pos - k_pos
            mask = (dist >= 0) & (dist <= window) & ((dist & (dil - 1)) == 0)
            logits = _nt_dot(q2, k_ref[0, pl.ds(k0, qb), pl.ds(c0, LANES)])
            lg = jnp.where(mask[None], logits.reshape(DIL_HEADS_PER_GROUP, qb, qb), NEG_INF)
            m_old = m_s[...].reshape(DIL_HEADS_PER_GROUP, qb, 1)
            m_new = jnp.maximum(m_old, jnp.max(lg, axis=-1, keepdims=True))
            p = jnp.where(mask[None], jnp.exp(lg - m_new), 0.0)
            alpha = jnp.exp(m_old - m_new)
            l_s[...] = (alpha * l_s[...].reshape(DIL_HEADS_PER_GROUP, qb, 1)
                        + jnp.sum(p, axis=-1, keepdims=True)).reshape(rows, 1)
            pv = jnp.dot(p.reshape(rows, qb).astype(bf16), v_ref[0, pl.ds(k0, qb), pl.ds(c0, LANES)],
                         preferred_element_type=jnp.float32)
            acc_s[...] = alpha.reshape(rows, 1) * acc_s[...] + pv
            m_s[...] = m_new.reshape(rows, 1)
            return c
        lax.fori_loop(jnp.maximum(0, n_tiles - 1 - j), n_tiles, body, 0)

        o2 = acc_s[...] / l_s[...]
        lse2 = m_s[...] + jnp.log(l_s[...])
        outs.append(jnp.where(first_head, o2[:qb], o2[qb:]))
        lses.append(jnp.where(first_head, lse2[:qb], lse2[qb:]))

    lse_max = functools.reduce(jnp.maximum, lses)
    ws = [jnp.exp(x - lse_max) for x in lses]
    den = functools.reduce(lambda a, b: a + b, ws)
    num = functools.reduce(lambda a, b: a + b, [w * o for w, o in zip(ws, outs)])
    o_ref[0] = num / den


def dilated_attention_pallas(q, k, v):
    b, s, w = q.shape
    qb = Q_BLOCK
    rows = DIL_HEADS_PER_GROUP * qb
    return pl.pallas_call(
        functools.partial(_dilated_kernel, qb=qb),
        grid=(b, s // qb),
        in_specs=[pl.BlockSpec((1, qb, w), lambda i, j: (i, j, 0)),
                  pl.BlockSpec((1, s, w), lambda i, j: (i, 0, 0)),
                  pl.BlockSpec((1, s, w), lambda i, j: (i, 0, 0))],
        out_specs=pl.BlockSpec((1, qb, DIL_OUT_W), lambda i, j: (i, j, 0)),
        out_shape=jax.ShapeDtypeStruct((b, s, DIL_OUT_W), jnp.float32),
        scratch_shapes=[pltpu.VMEM((rows, 1), jnp.float32),
                        pltpu.VMEM((rows, 1), jnp.float32),
                        pltpu.VMEM((rows, LANES), jnp.float32)],
        compiler_params=pltpu.CompilerParams(
            dimension_semantics=("arbitrary", "arbitrary"),
            vmem_limit_bytes=48 << 20),
        name="dilated_attention",
    )(q, k, v)


FOX_KV_TILE = 512


def _fox_kernel(q_ref, k_ref, v_ref, cq_ref, ck_ref, o_ref, m_s, l_s, acc_s, *, qb):
    j = pl.program_id(1)
    s0 = j * qb
    kt = FOX_KV_TILE
    bf16 = jnp.bfloat16
    scale = HEAD_DIM ** -0.5
    q_pos = s0 + lax.broadcasted_iota(jnp.int32, (qb, 1), 0)
    lane = lax.broadcasted_iota(jnp.int32, (qb, LANES), 1)
    first_head = lane < HEAD_DIM
    n_pairs = FOX_HEADS // 2
    rows = FOX_HEADS * qb
    n_tiles = (s0 + qb - 1) // kt + 1

    q2 = []
    for pair in range(n_pairs):
        q = q_ref[0, :, pl.ds(pair * LANES, LANES)] * scale
        q2.append(jnp.concatenate([jnp.where(first_head, q, 0.0),
                                   jnp.where(first_head, 0.0, q)], axis=0).astype(bf16))
    cq = [cq_ref[0, :, h:h + 1] for h in range(FOX_HEADS)]
    m_s[...] = jnp.full((rows, 1), NEG_INF, jnp.float32)
    l_s[...] = jnp.zeros((rows, 1), jnp.float32)
    acc_s[...] = jnp.zeros((rows, LANES), jnp.float32)

    def body(t, c):
        k0 = pl.multiple_of(t * kt, kt)
        k_pos = k0 + lax.broadcasted_iota(jnp.int32, (qb, kt), 1)
        mask = k_pos <= q_pos
        logits = jnp.concatenate(
            [_nt_dot(q2[pair], k_ref[0, pl.ds(k0, kt), pl.ds(pair * LANES, LANES)])
             for pair in range(n_pairs)], axis=0)
        bias = jnp.concatenate([cq[h] - ck_ref[0, h:h + 1, pl.ds(k0, kt)] for h in range(FOX_HEADS)],
                               axis=0)
        lg = jnp.where(mask[None], (logits + bias).reshape(FOX_HEADS, qb, kt), NEG_INF)
        m_old = m_s[...].reshape(FOX_HEADS, qb, 1)
        m_new = jnp.maximum(m_old, jnp.max(lg, axis=-1, keepdims=True))
        p = jnp.exp(lg - m_new)
        alpha = jnp.exp(m_old - m_new)
        l_s[...] = (alpha * l_s[...].reshape(FOX_HEADS, qb, 1)
                    + jnp.sum(p, axis=-1, keepdims=True)).reshape(rows, 1)
        pb = p.reshape(rows, kt).astype(bf16)
        pv = jnp.concatenate(
            [jnp.dot(pb[2 * pair * qb:2 * (pair + 1) * qb],
                     v_ref[0, pl.ds(k0, kt), pl.ds(pair * LANES, LANES)],
                     preferred_element_type=jnp.float32) for pair in range(n_pairs)], axis=0)
        acc_s[...] = alpha.reshape(rows, 1) * acc_s[...] + pv
        m_s[...] = m_new.reshape(rows, 1)
        return c
    lax.fori_loop(0, n_tiles, body, 0)

    o2 = acc_s[...] / l_s[...]
    for pair in range(n_pairs):
        r0 = 2 * pair * qb
        o_ref[0, :, pl.ds(pair * LANES, LANES)] = jnp.where(first_head, o2[r0:r0 + qb],
                                                             o2[r0 + qb:r0 + 2 * qb])


def fox_attention_pallas(q, k, v, cum):
    b, s, w = q.shape
    qb = Q_BLOCK
    return pl.pallas_call(
        functools.partial(_fox_kernel, qb=qb),
        grid=(b, s // qb),
        in_specs=[pl.BlockSpec((1, qb, w), lambda i, j: (i, j, 0)),
                  pl.BlockSpec((1, s, w), lambda i, j: (i, 0, 0)),
                  pl.BlockSpec((1, s, w), lambda i, j: (i, 0, 0)),
                  pl.BlockSpec((1, qb, FOX_HEADS), lambda i, j: (i, j, 0)),
                  pl.BlockSpec((1, FOX_HEADS, s), lambda i, j: (i, 0, 0))],
        out_specs=pl.BlockSpec((1, qb, w), lambda i, j: (i, j, 0)),
        out_shape=jax.ShapeDtypeStruct((b, s, w), jnp.float32),
        scratch_shapes=[pltpu.VMEM((FOX_HEADS * qb, 1), jnp.float32),
                        pltpu.VMEM((FOX_HEADS * qb, 1), jnp.float32),
                        pltpu.VMEM((FOX_HEADS * qb, LANES), jnp.float32)],
        compiler_params=pltpu.CompilerParams(
            dimension_semantics=("arbitrary", "arbitrary"),
            vmem_limit_bytes=40 << 20),
        name="fox_attention",
    )(q, k, v, cum, jnp.transpose(cum, (0, 2, 1)))


LRU_TILE = 512


def _lru_kernel(x_ref, gate_ref, cw_ref, cb_ref, wa_ref, ba_ref, wx_ref, bx_ref, lam_ref, o_ref,
                xbuf, a_s, u_s, h_s, hlast, *, tt):
    j = pl.program_id(1)
    c = x_ref.shape[-1]
    bf16 = jnp.bfloat16

    @pl.when(j == 0)
    def _():
        xbuf[pl.ds(0, SUBLANES), :] = jnp.zeros((SUBLANES, c), jnp.float32)
        hlast[...] = jnp.zeros((1, c), jnp.float32)

    x = x_ref[0]
    xbuf[pl.ds(SUBLANES, tt), :] = x
    xc = cb_ref[...] + cw_ref[LRU_CONV - 1:LRU_CONV, :] * x
    for back in range(1, LRU_CONV):
        xc = xc + cw_ref[LRU_CONV - 1 - back:LRU_CONV - back, :] * xbuf[pl.ds(SUBLANES - back, tt), :]
    xbuf[pl.ds(0, SUBLANES), :] = x[tt - SUBLANES:, :]

    xb = xc.astype(bf16)
    r = jax.nn.sigmoid(jnp.dot(xb, wa_ref[...], preferred_element_type=jnp.float32) + ba_ref[...])
    i_g = jax.nn.sigmoid(jnp.dot(xb, wx_ref[...], preferred_element_type=jnp.float32) + bx_ref[...])
    lam = lam_ref[...]
    softplus_neg = jnp.maximum(-lam, 0.0) + jnp.log1p(jnp.exp(-jnp.abs(lam)))
    log_a = -LRU_C * r * softplus_neg
    a_s[...] = jnp.exp(log_a)
    u_s[...] = jnp.sqrt(1.0 - jnp.exp(2.0 * log_a)) * (i_g * xc)

    def group(gi, h):
        r0 = pl.multiple_of(gi * SUBLANES, SUBLANES)
        a8 = a_s[pl.ds(r0, SUBLANES), :]
        u8 = u_s[pl.ds(r0, SUBLANES), :]
        rows = []
        for s in range(SUBLANES):
            h = a8[s:s + 1, :] * h + u8[s:s + 1, :]
            rows.append(h)
        h_s[pl.ds(r0, SUBLANES), :] = jnp.concatenate(rows, axis=0)
        return h
    hlast[...] = lax.fori_loop(0, tt // SUBLANES, group, hlast[...])
    o_ref[0] = h_s[...] * jax.nn.gelu(gate_ref[0])


def _block_diag(w):
    n, c, d = w.shape
    eye = jnp.eye(n, dtype=w.dtype)
    return (eye[:, None, :, None] * w[:, :, None, :]).reshape(n * c, n * d)


def rg_lru_pallas(x_in, gate, conv_w, conv_b, w_a, b_a, w_x, b_x, lam):
    b, s, c = x_in.shape
    tt = LRU_TILE
    row = lambda v: v.reshape(1, c)
    tok_spec = pl.BlockSpec((1, tt, c), lambda i, j: (i, j, 0))
    full = lambda shape: pl.BlockSpec(shape, lambda i, j: (0,) * len(shape))
    return pl.pallas_call(
        functools.partial(_lru_kernel, tt=tt),
        grid=(b, s // tt),
        in_specs=[tok_spec, tok_spec, full((LRU_CONV, c)), full((1, c)), full((c, c)), full((1, c)),
                  full((c, c)), full((1, c)), full((1, c))],
        out_specs=tok_spec,
        out_shape=jax.ShapeDtypeStruct((b, s, c), jnp.float32),
        scratch_shapes=[pltpu.VMEM((tt + SUBLANES, c), jnp.float32),
                        pltpu.VMEM((tt, c), jnp.float32),
                        pltpu.VMEM((tt, c), jnp.float32),
                        pltpu.VMEM((tt, c), jnp.float32),
                        pltpu.VMEM((1, c), jnp.float32)],
        compiler_params=pltpu.CompilerParams(dimension_semantics=("arbitrary", "arbitrary")),
        name="rg_lru",
    )(x_in, gate, conv_w, row(conv_b), _block_diag(w_a).astype(jnp.bfloat16), row(b_a),
      _block_diag(w_x).astype(jnp.bfloat16), row(b_x), row(lam))


def rms_norm(x, g):
    xf = x.astype(jnp.float32)
    y = xf * lax.rsqrt(jnp.mean(xf * xf, axis=-1, keepdims=True) + NORM_EPS)
    return (y * g.astype(jnp.float32)).astype(x.dtype)


def rope(x, pos):
    half = x.shape[-1] // 2
    freqs = ROPE_THETA ** (-jnp.arange(half, dtype=jnp.float32) / half)
    ang = pos.astype(jnp.float32)[:, None] * freqs[None, :]
    cos = jnp.cos(ang)[None, :, None, :]
    sin = jnp.sin(ang)[None, :, None, :]
    xf = x.astype(jnp.float32)
    x1, x2 = xf[..., :half], xf[..., half:]
    return jnp.concatenate([x1 * cos - x2 * sin, x2 * cos + x1 * sin], axis=-1).astype(x.dtype)


def _unblock(y):
    y = jnp.moveaxis(y, 0, 1)
    return y.reshape((y.shape[0], y.shape[1] * y.shape[2]) + y.shape[3:])


def masked_softmax(logits, mask):
    logits = jnp.where(mask, logits, NEG_INF)
    m = jnp.max(logits, axis=-1, keepdims=True)
    e = jnp.where(mask, jnp.exp(logits - m), 0.0)
    return e / jnp.maximum(jnp.sum(e, axis=-1, keepdims=True), 1e-30)


def fox_attention(q, k, v, f_logit, b_f):
    B, S, H, dh = q.shape
    log_f = jax.nn.log_sigmoid(f_logit.astype(jnp.float32) + b_f.astype(jnp.float32))
    cum = jnp.transpose(jnp.cumsum(log_f, axis=1), (0, 2, 1))
    scale = dh ** -0.5
    k_pos = jnp.arange(S)

    def block(i):
        s0 = i * Q_BLOCK
        qb = lax.dynamic_slice_in_dim(q, s0, Q_BLOCK, axis=1)
        cq = lax.dynamic_slice_in_dim(cum, s0, Q_BLOCK, axis=2)
        logits = jnp.einsum('bqhd,bkhd->bhqk', qb, k).astype(jnp.float32) * scale
        logits = logits + cq[..., None] - cum[:, :, None, :]
        q_pos = s0 + jnp.arange(Q_BLOCK)
        logits = jnp.where(k_pos[None, :] <= q_pos[:, None], logits, NEG_INF)
        p = jax.nn.softmax(logits, axis=-1)
        return jnp.einsum('bhqk,bkhd->bqhd', p.astype(v.dtype), v)

    return _unblock(lax.map(block, jnp.arange(S // Q_BLOCK)))


def rg_lru_branch(x_in, gate, conv_w, conv_b, w_a, b_a, w_x, b_x, lam):
    B, S, C = x_in.shape
    xc = lax.conv_general_dilated(x_in, conv_w[:, None, :], window_strides=(1,),
                                  padding=[(LRU_CONV - 1, 0)],
                                  dimension_numbers=('NWC', 'WIO', 'NWC'),
                                  feature_group_count=C) + conv_b
    xr = xc.reshape(B, S, LRU_BLOCKS, C // LRU_BLOCKS)
    r = jax.nn.sigmoid(jnp.einsum('bsnc,ncd->bsnd', xr, w_a).reshape(B, S, C) + b_a)
    i_g = jax.nn.sigmoid(jnp.einsum('bsnc,ncd->bsnd', xr, w_x).reshape(B, S, C) + b_x)
    log_a = -LRU_C * r.astype(jnp.float32) * jax.nn.softplus(-lam.astype(jnp.float32))
    a = jnp.exp(log_a)
    u = jnp.sqrt(-jnp.expm1(2.0 * log_a)) * (i_g * xc).astype(jnp.float32)

    def combine(left, right):
        a_l, b_l = left
        a_r, b_r = right
        return a_l * a_r, a_r * b_l + b_r

    _, h = lax.associative_scan(combine, (a, u), axis=1)
    return h.astype(x_in.dtype) * jax.nn.gelu(gate)


def dilated_attention(q, k, v):
    B, S, H, dh = q.shape
    G, Hg = DIL_GROUPS, DIL_HEADS_PER_GROUP
    qg = q.reshape(B, S, G, Hg, dh)
    kg = k.reshape(B, S, G, Hg, dh)
    vg = v.reshape(B, S, G, Hg, dh)
    dil = jnp.array([d for _, d in DIL_PAIRS], dtype=jnp.int32)
    steps = jnp.arange(DIL_KEYS)
    g_idx = jnp.arange(G)[:, None, None]
    scale = dh ** -0.5

    def block(i):
        s0 = i * Q_BLOCK
        q_pos = s0 + jnp.arange(Q_BLOCK)
        k_pos = q_pos[None, :, None] - dil[:, None, None] * steps[None, None, :]
        valid = k_pos >= 0
        k_idx = jnp.maximum(k_pos, 0)
        kb = kg[:, k_idx, g_idx]
        vb = vg[:, k_idx, g_idx]
        qb = lax.dynamic_slice_in_dim(qg, s0, Q_BLOCK, axis=1)
        logits = jnp.einsum('bqghd,bgqkhd->bgqhk', qb, kb).astype(jnp.float32) * scale
        logits = jnp.where(valid[None, :, :, None, :], logits, NEG_INF)
        m = jnp.max(logits, axis=-1, keepdims=True)
        e = jnp.exp(logits - m)
        l = jnp.sum(e, axis=-1)
        o = jnp.einsum('bgqhk,bgqkhd->bgqhd', e, vb.astype(jnp.float32)) / l[..., None]
        lse = m[..., 0] + jnp.log(l)
        w = jax.nn.softmax(lse, axis=1)
        return jnp.einsum('bgqh,bgqhd->bqhd', w, o).astype(v.dtype)

    return _unblock(lax.map(block, jnp.arange(S // Q_BLOCK)))


def nsa_attention(q, k_cmp, v_cmp, k_slc, v_slc, k_win, v_win, gate_logits,
                  pe_k, pe_v, w1_k, w2_k, w1_v, w2_v):
    B, S, H, dh = q.shape
    Hkv = k_cmp.shape[2]
    G = H // Hkv
    pos = jnp.arange(S)
    q_rot = rope(q, pos)
    k_slc = rope(k_slc, pos)
    k_win = rope(k_win, pos)
    gates = jax.nn.sigmoid(gate_logits).reshape(B, S, H, 3)
    scale = dh ** -0.5

    n_cmp = (S - NSA_CMP_BLOCK) // NSA_CMP_STRIDE + 1
    cmp_idx = jnp.arange(n_cmp)[:, None] * NSA_CMP_STRIDE + jnp.arange(NSA_CMP_BLOCK)[None, :]

    def compress(kv, pe, w1, w2):
        blocks = kv[:, cmp_idx] + pe[None, None, :, None, :]
        flat = jnp.moveaxis(blocks, 3, 2).reshape(B, n_cmp, Hkv, NSA_CMP_BLOCK * dh)
        return jax.nn.gelu(flat @ w1) @ w2

    kc = compress(k_cmp, pe_k, w1_k, w2_k)
    vc = compress(v_cmp, pe_v, w1_v, w2_v)
    cmp_start = jnp.arange(n_cmp) * NSA_CMP_STRIDE
    cmp_end = cmp_start + NSA_CMP_BLOCK - 1

    n_slc = S // NSA_SLC_BLOCK
    top_n = min(NSA_TOP_N, n_slc)
    slc_start = jnp.arange(n_slc) * NSA_SLC_BLOCK
    overlap = ((cmp_start[:, None] < slc_start[None, :] + NSA_SLC_BLOCK)
               & (cmp_start[:, None] + NSA_CMP_BLOCK > slc_start[None, :])).astype(jnp.float32)
    ks_blocks = jnp.transpose(k_slc.reshape(B, n_slc, NSA_SLC_BLOCK, Hkv, dh), (0, 3, 1, 2, 4))
    vs_blocks = jnp.transpose(v_slc.reshape(B, n_slc, NSA_SLC_BLOCK, Hkv, dh), (0, 3, 1, 2, 4))
    pad = ((0, 0), (NSA_WINDOW, 0), (0, 0), (0, 0))
    k_win_pad = jnp.pad(k_win, pad)
    v_win_pad = jnp.pad(v_win, pad)
    b_idx = jnp.arange(B)[:, None, None, None]
    h_idx = jnp.arange(Hkv)[None, :, None, None]
    blk_ids = jnp.arange(n_slc)

    def block(i):
        s0 = i * Q_BLOCK
        q_pos = s0 + jnp.arange(Q_BLOCK)
        qb = lax.dynamic_slice_in_dim(q, s0, Q_BLOCK, axis=1).reshape(B, Q_BLOCK, Hkv, G, dh)
        qrb = lax.dynamic_slice_in_dim(q_rot, s0, Q_BLOCK, axis=1).reshape(B, Q_BLOCK, Hkv, G, dh)
        lc = jnp.einsum('bqkgd,bckd->bkgqc', qb, kc).astype(jnp.float32) * scale
        p_c = masked_softmax(lc, (cmp_end[None, :] <= q_pos[:, None])[None, None, None])
        o_c = jnp.einsum('bkgqc,bckd->bqkgd', p_c, vc.astype(jnp.float32))
        imp = jnp.einsum('bkgqc,cn->bkqn', p_c, overlap)
        cur = q_pos // NSA_SLC_BLOCK
        forced = ((blk_ids[None, :] == 0) | (blk_ids[None, :] == cur[:, None])
                  | (blk_ids[None, :] == cur[:, None] - 1))
        imp = jnp.where(forced, NSA_FORCE_SCORE, imp)
        imp = jnp.where(blk_ids[None, :] <= cur[:, None], imp, NEG_INF)
        top_val, top_idx = lax.top_k(imp, top_n)
        sel_ok = top_val > 0.5 * NEG_INF
        ksel = ks_blocks[b_idx, h_idx, top_idx]
        vsel = vs_blocks[b_idx, h_idx, top_idx]
        sel_pos = top_idx[..., None] * NSA_SLC_BLOCK + jnp.arange(NSA_SLC_BLOCK)
        sel_mask = sel_ok[..., None] & (sel_pos <= q_pos[None, None, :, None, None])
        n_sel_keys = top_n * NSA_SLC_BLOCK
        ls = jnp.einsum('bqkgd,bkqnld->bkgqnl', qrb, ksel).astype(jnp.float32) * scale
        ls = ls.reshape(B, Hkv, G, Q_BLOCK, n_sel_keys)
        p_s = masked_softmax(ls, sel_mask.reshape(B, Hkv, 1, Q_BLOCK, n_sel_keys))
        o_s = jnp.einsum('bkgqm,bkqmd->bqkgd', p_s,
                         vsel.reshape(B, Hkv, Q_BLOCK, n_sel_keys, dh).astype(jnp.float32))
        kw = lax.dynamic_slice_in_dim(k_win_pad, s0, NSA_WINDOW + Q_BLOCK, axis=1)
        vw = lax.dynamic_slice_in_dim(v_win_pad, s0, NSA_WINDOW + Q_BLOCK, axis=1)
        w_pos = s0 - NSA_WINDOW + jnp.arange(NSA_WINDOW + Q_BLOCK)
        dist = q_pos[:, None] - w_pos[None, :]
        w_mask = (dist >= 0) & (dist < NSA_WINDOW) & (w_pos[None, :] >= 0)
        lw = jnp.einsum('bqkgd,bwkd->bkgqw', qrb, kw).astype(jnp.float32) * scale
        p_w = masked_softmax(lw, w_mask)
        o_w = jnp.einsum('bkgqw,bwkd->bqkgd', p_w, vw.astype(jnp.float32))
        gb = lax.dynamic_slice_in_dim(gates, s0, Q_BLOCK, axis=1).reshape(B, Q_BLOCK, Hkv, G, 3)
        gb = gb.astype(jnp.float32)
        out = gb[..., 0:1] * o_c + gb[..., 1:2] * o_s + gb[..., 2:3] * o_w
        return out.reshape(B, Q_BLOCK, H, dh).astype(q.dtype)

    return _unblock(lax.map(block, jnp.arange(S // Q_BLOCK)))


def nsa_branch(nq, nkc, nvc, nks, nvs, nkw, nvw, ng, pe_k, pe_v, w1_k, w2_k, w1_v, w2_v):
    B, S, _ = nq.shape
    dh = HEAD_DIM
    bf16 = jnp.bfloat16
    pos = jnp.arange(S)
    q4 = nq.reshape(B, S, NSA_HEADS, dh)
    qn = jnp.transpose(q4, (0, 2, 1, 3))
    qr = jnp.transpose(rope(q4, pos), (0, 2, 1, 3))
    ks = rope(nks.reshape(B, S, 1, dh), pos).reshape(B, S, dh).astype(bf16)
    kw = rope(nkw.reshape(B, S, 1, dh), pos).reshape(B, S, dh).astype(bf16)

    n_cmp = (S - NSA_CMP_BLOCK) // NSA_CMP_STRIDE + 1
    n_cmp_pad = S // NSA_CMP_STRIDE
    n_slc = S // NSA_SLC_BLOCK

    def compress(kv, pe, w1, w2):
        chunks = kv.reshape(B, n_cmp_pad, NSA_CMP_STRIDE * dh)
        flat = jnp.concatenate([chunks[:, :-1], chunks[:, 1:]], axis=-1) + pe.reshape(-1)
        out = jax.nn.gelu(flat @ w1) @ w2
        return jnp.pad(out, ((0, 0), (0, n_cmp_pad - n_cmp), (0, 0))).astype(bf16)

    kc = compress(nkc, pe_k, w1_k, w2_k)
    vc = compress(nvc, pe_v, w1_v, w2_v)
    cmp_start = jnp.arange(n_cmp_pad) * NSA_CMP_STRIDE
    slc_start = jnp.arange(n_slc) * NSA_SLC_BLOCK
    overlap = ((cmp_start[:, None] < slc_start[None, :] + NSA_SLC_BLOCK)
               & (cmp_start[:, None] + NSA_CMP_BLOCK > slc_start[None, :])
               & (jnp.arange(n_cmp_pad)[:, None] < n_cmp)).astype(bf16)
    out = nsa_attention_pallas(qn, qr, kc, vc, overlap, ks, nvs.astype(bf16), kw,
                               nvw.astype(bf16), ng)
    return jnp.transpose(out, (0, 2, 1, 3)).reshape(B, S, NSA_W)


def peer_ffn(x_res, h2, w_q, sub_keys, u, v):
    expert, gt = peer_route(h2, w_q, sub_keys)
    uv = jnp.concatenate([u, v], axis=1)
    return peer_mix(h2, x_res, expert, gt, uv)


def kernel(x, norm1_g, w_in, fox_b_f, lru_conv_w, lru_conv_b, lru_w_a, lru_b_a, lru_w_x, lru_b_x, lru_lambda, nsa_pe_k, nsa_pe_v, nsa_w1_k, nsa_w2_k, nsa_w1_v, nsa_w2_v, w_br_fox, w_br_lru, w_br_dil, w_br_nsa, w_out, norm2_g, peer_w_q, peer_sub_keys, peer_u, peer_v, final_g):
    B, S, D = x.shape
    pos = jnp.arange(S)
    split_points = np.cumsum(IN_SPLITS)[:-1].tolist()

    def heads(t, n):
        return t.reshape(B, S, n, HEAD_DIM)

    for l in range(DEPTH):
        h = rms_norm(x, norm1_g[l])
        proj = h @ w_in[l]
        (fq, fk, fv, ff, lx, lg, dq, dk, dv, nq, nkc, nvc, nks, nvs, nkw, nvw, ng, mg) = \
            jnp.split(proj, split_points, axis=-1)

        cum = jnp.cumsum(jax.nn.log_sigmoid(ff + fox_b_f[l]), axis=1)
        o_fox = fox_attention_pallas(fq, fk.astype(jnp.bfloat16), fv.astype(jnp.bfloat16), cum)
        o_lru = rg_lru_pallas(lx, lg, lru_conv_w[l], lru_conv_b[l], lru_w_a[l], lru_b_a[l],
                              lru_w_x[l], lru_b_x[l], lru_lambda[l])
        o_dil = dilated_attention_pallas(
            rope(heads(dq, DIL_HEADS), pos).reshape(B, S, DIL_W),
            rope(heads(dk, DIL_HEADS), pos).reshape(B, S, DIL_W).astype(jnp.bfloat16),
            dv.astype(jnp.bfloat16))
        o_nsa = nsa_branch(nq, nkc, nvc, nks, nvs, nkw, nvw, ng,
                           nsa_pe_k[l], nsa_pe_v[l], nsa_w1_k[l], nsa_w2_k[l],
                           nsa_w1_v[l], nsa_w2_v[l])

        gates = jax.nn.sigmoid(mg).reshape(B, S, N_BRANCHES, D)
        merged = (gates[:, :, 0] * (o_fox @ w_br_fox[l])
                  + gates[:, :, 1] * (o_lru @ w_br_lru[l])
                  + gates[:, :, 2] * (o_dil @ w_br_dil[l])
                  + gates[:, :, 3] * (o_nsa @ w_br_nsa[l]))
        x = x + merged @ w_out[l]

        h2 = rms_norm(x, norm2_g[l])
        x = peer_ffn(x.reshape(B * S, D), h2.reshape(B * S, D), peer_w_q[l], peer_sub_keys[l],
                     peer_u[l], peer_v[l]).reshape(B, S, D)

    return rms_norm_pallas(x.reshape(B * S, D), final_g).reshape(B, S, D)
```

```python
import functools

import jax
import jax.numpy as jnp
import numpy as np
from jax import lax
from jax.experimental import pallas as pl
from jax.experimental.pallas import tpu as pltpu

D_MODEL = 1024
DEPTH = 2
HEAD_DIM = 64
Q_BLOCK = 128
ROPE_THETA = 10000.0
NORM_EPS = 1e-6
NEG_INF = -1e30
N_BRANCHES = 4
FOX_HEADS = 4
FOX_W = FOX_HEADS * HEAD_DIM
LRU_WIDTH = 256
LRU_BLOCKS = 4
LRU_CONV = 4
LRU_C = 8.0
DIL_PAIRS = ((128, 1), (512, 4), (2048, 16))
DIL_GROUPS = len(DIL_PAIRS)
DIL_HEADS_PER_GROUP = 2
DIL_HEADS = DIL_GROUPS * DIL_HEADS_PER_GROUP
DIL_W = DIL_HEADS * HEAD_DIM
DIL_OUT_W = DIL_HEADS_PER_GROUP * HEAD_DIM
DIL_KEYS = DIL_PAIRS[0][0] // DIL_PAIRS[0][1] + 1
NSA_HEADS = 4
NSA_KV_HEADS = 1
NSA_W = NSA_HEADS * HEAD_DIM
NSA_KV_W = NSA_KV_HEADS * HEAD_DIM
NSA_CMP_BLOCK = 32
NSA_CMP_STRIDE = 16
NSA_SLC_BLOCK = 64
NSA_TOP_N = 16
NSA_WINDOW = 512
NSA_FORCE_SCORE = 1e6
PEER_HEADS = 8
PEER_N_KEYS = 128
PEER_TOPK = 16
PEER_D_KEY = 256
PEER_BLOCK = 128

IN_SPLITS = (FOX_W, FOX_W, FOX_W, FOX_HEADS,
             LRU_WIDTH, LRU_WIDTH,
             DIL_W, DIL_W, DIL_W,
             NSA_W, NSA_KV_W, NSA_KV_W, NSA_KV_W, NSA_KV_W, NSA_KV_W, NSA_KV_W, NSA_HEADS * 3,
             N_BRANCHES * D_MODEL)


def _rms_norm_kernel(x_ref, g_ref, o_ref):
    x = x_ref[...]
    ms = jnp.mean(x * x, axis=-1, keepdims=True)
    o_ref[...] = x * lax.rsqrt(ms + NORM_EPS) * g_ref[...]


def rms_norm_pallas(x, g, rows=512):
    t, d = x.shape
    return pl.pallas_call(
        _rms_norm_kernel,
        grid=(t // rows,),
        in_specs=[pl.BlockSpec((rows, d), lambda i: (i, 0)),
                  pl.BlockSpec((1, d), lambda i: (0, 0))],
        out_specs=pl.BlockSpec((rows, d), lambda i: (i, 0)),
        out_shape=jax.ShapeDtypeStruct((t, d), x.dtype),
        name="rms_norm",
    )(x, g.reshape(1, d))


def _gelu_tanh(x):
    return 0.5 * x * (1.0 + jnp.tanh(0.7978845608028654 * (x + 0.044715 * x * x * x)))


PEER_TOKENS_PER_STEP = 8
LANES = 128
SUBLANES = 8


def _peer_mix_kernel(idx_hbm, x_ref, gt_ref, res_ref, uv_hbm, o_ref,
                     idx_smem, buf, idx_sem, row_sem, *, tb, n_sel, d):
    i = pl.program_id(0)
    n = pl.num_programs(0)
    rows = tb * n_sel
    n_chunks = d // LANES
    groups = n_sel // SUBLANES

    def idx_copy(step, s):
        return pltpu.make_async_copy(idx_hbm.at[step], idx_smem.at[pl.ds(s * rows, rows)],
                                     idx_sem.at[s])

    def row_copy(s, j, row):
        return pltpu.make_async_copy(uv_hbm.at[row], buf.at[s, j // SUBLANES, pl.ds(j % SUBLANES, 1)],
                                     row_sem.at[s])

    def slot_wait(s):
        pltpu.make_async_copy(buf.at[s], buf.at[s], row_sem.at[s]).wait()

    @pl.when(i == 0)
    def _():
        idx_copy(0, 0).start()
        idx_copy(0, 0).wait()

        def body(jj, c):
            for k in range(SUBLANES):
                row = idx_smem[jj * SUBLANES + k]
                pltpu.make_async_copy(uv_hbm.at[row], buf.at[0, jj, pl.ds(k, 1)],
                                      row_sem.at[0]).start(priority=k % 2)
            return c
        lax.fori_loop(0, rows // SUBLANES, body, 0)
        idx_copy(1, 1).start()

    def step(cur):
        nxt = 1 - cur
        idx_copy(i + 1, nxt).wait()

        @pl.when(i + 2 <= n)
        def _():
            idx_copy(i + 2, cur).start()

        slot_wait(cur)

        def issue(t, part):
            lo = part * groups // (2 * n_chunks)
            hi = (part + 1) * groups // (2 * n_chunks)
            for j in range(t * n_sel + lo * SUBLANES, t * n_sel + hi * SUBLANES):
                row_copy(nxt, j, idx_smem[nxt * rows + j]).start(priority=j % 2)

        def sel_rows(t, lane0):
            tile = buf[cur, pl.ds(t * groups, groups), :, pl.ds(lane0, LANES)]
            return tile.reshape(n_sel, LANES)

        for t in range(tb):
            acc = jnp.zeros((n_sel, LANES), jnp.float32)
            for c in range(n_chunks):
                issue(t, c)
                xc = x_ref[pl.ds(t, 1), pl.ds(c * LANES, LANES)]
                acc = acc + sel_rows(t, c * LANES) * xc
            s = jnp.sum(acc, axis=-1, keepdims=True)
            w = gt_ref[0, :, pl.ds(t, 1)] * _gelu_tanh(s)
            wb = jnp.broadcast_to(w, (n_sel, LANES))
            for c in range(n_chunks):
                issue(t, n_chunks + c)
                vc = sel_rows(t, d + c * LANES)
                oc = jnp.sum(wb * vc, axis=0, keepdims=True)
                o_ref[pl.ds(t, 1), pl.ds(c * LANES, LANES)] = (
                    res_ref[pl.ds(t, 1), pl.ds(c * LANES, LANES)] + oc)

        @pl.when(i == n - 1)
        def _():
            slot_wait(nxt)

    @pl.when(i % 2 == 0)
    def _():
        step(0)

    @pl.when(i % 2 == 1)
    def _():
        step(1)


def peer_mix(x, res, idx, gt_blocks, uv):
    t, d = x.shape
    n_sel = idx.shape[1]
    tb = PEER_TOKENS_PER_STEP
    steps = t // tb
    spare = (jnp.arange(tb * n_sel, dtype=jnp.int32) % uv.shape[0]).reshape(1, tb * n_sel)
    idx_steps = jnp.concatenate([idx.reshape(steps, tb * n_sel), spare], axis=0)
    nb, _, tq = gt_blocks.shape
    gt = jnp.transpose(gt_blocks.reshape(nb, n_sel, tq // tb, tb), (0, 2, 1, 3)).reshape(steps, n_sel, tb)
    kern = functools.partial(_peer_mix_kernel, tb=tb, n_sel=n_sel, d=d)
    return pl.pallas_call(
        kern,
        grid=(steps,),
        in_specs=[pl.BlockSpec(memory_space=pl.ANY),
                  pl.BlockSpec((tb, d), lambda i: (i, 0)),
                  pl.BlockSpec((1, n_sel, tb), lambda i: (i, 0, 0)),
                  pl.BlockSpec((tb, d), lambda i: (i, 0)),
                  pl.BlockSpec(memory_space=pl.ANY)],
        out_specs=pl.BlockSpec((tb, d), lambda i: (i, 0)),
        out_shape=jax.ShapeDtypeStruct((t, d), jnp.float32),
        scratch_shapes=[pltpu.SMEM((2 * tb * n_sel,), jnp.int32),
                        pltpu.VMEM((2, tb * n_sel // SUBLANES, SUBLANES, 2 * d), jnp.float32),
                        pltpu.SemaphoreType.DMA((2,)),
                        pltpu.SemaphoreType.DMA((2,))],
        compiler_params=pltpu.CompilerParams(
            dimension_semantics=("arbitrary",),
            vmem_limit_bytes=2 * tb * n_sel * 2 * d * 4 + (8 << 20)),
        name="peer_mix",
    )(idx_steps, x, gt, res, uv.reshape(uv.shape[0], 1, 2 * d))


def _topk_rows(work, k):
    n = work.shape[0]
    row = lax.broadcasted_iota(jnp.int32, work.shape, 0)
    vals, idxs = [], []
    for _ in range(k):
        mx = jnp.max(work, axis=0, keepdims=True)
        pos = jnp.min(jnp.where(work == mx, row, n), axis=0, keepdims=True)
        work = jnp.where(row == pos, -jnp.inf, work)
        vals.append(mx)
        idxs.append(pos)
    return jnp.concatenate(vals, axis=0), jnp.concatenate(idxs, axis=0)


def _peer_route_kernel(x_ref, wq_ref, keys_ref, idx_ref, gt_ref, q_s, idx_s, g_s, *, tq):
    bf16 = jnp.bfloat16
    k = PEER_TOPK
    half = PEER_D_KEY // 2
    q = jnp.dot(x_ref[...].astype(bf16), wq_ref[...], preferred_element_type=jnp.float32)
    for c in range(2 * PEER_HEADS):
        q_s[c] = q[:, c * half:(c + 1) * half].astype(bf16)

    def head_body(h, carry):
        tops = []
        for p in range(2):
            scores = _nt_dot(keys_ref[p], q_s[2 * h + p])
            tops.append(_topk_rows(scores, k))
        (v0, i0), (v1, i1) = tops
        keep = [k // (a + 1) for a in range(k)]
        n_cand = sum(keep)
        n_pad = -n_cand % SUBLANES
        cand = jnp.concatenate([v0[a:a + 1, :] + v1[:keep[a]] for a in range(k)]
                               + [jnp.full((n_pad, tq), -jnp.inf, jnp.float32)], axis=0)
        cand_idx = jnp.concatenate([i0[a:a + 1, :] * PEER_N_KEYS + i1[:keep[a]] for a in range(k)]
                                   + [jnp.full((n_pad, tq), -1, jnp.int32)], axis=0)
        flat = lax.broadcasted_iota(jnp.int32, cand.shape, 0)
        top_s, experts = [], []
        for _ in range(k):
            mx = jnp.max(cand, axis=0, keepdims=True)
            pos = jnp.min(jnp.where(cand == mx, flat, n_cand + n_pad), axis=0, keepdims=True)
            pick = flat == pos
            experts.append(jnp.max(jnp.where(pick, cand_idx, -1), axis=0, keepdims=True))
            cand = jnp.where(pick, -jnp.inf, cand)
            top_s.append(mx)
        top_s = jnp.concatenate(top_s, axis=0)
        ex = jnp.exp(top_s - jnp.max(top_s, axis=0, keepdims=True))
        r0 = pl.multiple_of(h * k, k)
        g_s[pl.ds(r0, k), :] = ex / jnp.sum(ex, axis=0, keepdims=True)
        idx_s[pl.ds(r0, k), :] = jnp.concatenate(experts, axis=0)
        return carry
    lax.fori_loop(0, PEER_HEADS, head_body, 0)

    idx_ref[...] = jnp.transpose(idx_s[...])
    gt_ref[0] = g_s[...]


def peer_route(h2, w_q, sub_keys):
    t, d = h2.shape
    tq = PEER_BLOCK
    n_sel = PEER_HEADS * PEER_TOPK
    half = PEER_D_KEY // 2
    return pl.pallas_call(
        functools.partial(_peer_route_kernel, tq=tq),
        grid=(t // tq,),
        in_specs=[pl.BlockSpec((tq, d), lambda i: (i, 0)),
                  pl.BlockSpec((d, PEER_HEADS * PEER_D_KEY), lambda i: (0, 0)),
                  pl.BlockSpec((2, PEER_N_KEYS, half), lambda i: (0, 0, 0))],
        out_specs=[pl.BlockSpec((tq, n_sel), lambda i: (i, 0)),
                   pl.BlockSpec((1, n_sel, tq), lambda i: (i, 0, 0))],
        out_shape=[jax.ShapeDtypeStruct((t, n_sel), jnp.int32),
                   jax.ShapeDtypeStruct((t // tq, n_sel, tq), jnp.float32)],
        scratch_shapes=[pltpu.VMEM((2 * PEER_HEADS, tq, half), jnp.bfloat16),
                        pltpu.VMEM((n_sel, tq), jnp.int32),
                        pltpu.VMEM((n_sel, tq), jnp.float32)],
        compiler_params=pltpu.CompilerParams(dimension_semantics=("arbitrary",),
                                             vmem_limit_bytes=32 << 20),
        name="peer_route",
    )(h2, w_q.astype(jnp.bfloat16), sub_keys.astype(jnp.bfloat16))


NSA_KV_TILE = 512


def _nt_dot(a, b):
    return lax.dot_general(a, b, (((1,), (1,)), ((), ())), preferred_element_type=jnp.float32)


def _nsa_kernel(qn_ref, qr_ref, kc_ref, vc_ref, ov_ref, ks_ref, vs_ref, kw_ref, vw_ref, gl_ref,
                o_ref, m_s, l_s, acc_s, out_s, *, n_heads, qb, n_cmp_pad, n_slc):
    j = pl.program_id(1)
    s0 = j * qb
    hq = n_heads * qb
    bf16 = jnp.bfloat16
    scale = HEAD_DIM ** -0.5
    q_pos = s0 + lax.broadcasted_iota(jnp.int32, (qb, 1), 0)

    gates = jax.nn.sigmoid(gl_ref[0])

    def gate_col(br):
        return jnp.concatenate([gates[:, 3 * h + br:3 * h + br + 1] for h in range(n_heads)], axis=0)

    def masked_update(logits, mask, v_tile):
        kt = logits.shape[-1]
        lg = jnp.where(mask[None], logits.reshape(n_heads, qb, kt), NEG_INF)
        m_old = m_s[...].reshape(n_heads, qb, 1)
        m_new = jnp.maximum(m_old, jnp.max(lg, axis=-1, keepdims=True))
        p = jnp.where(mask[None], jnp.exp(lg - m_new), 0.0)
        alpha = jnp.exp(m_old - m_new)
        l_s[...] = (alpha * l_s[...].reshape(n_heads, qb, 1)
                    + jnp.sum(p, axis=-1, keepdims=True)).reshape(hq, 1)
        pv = jnp.dot(p.reshape(hq, kt).astype(bf16), v_tile, preferred_element_type=jnp.float32)
        acc_s[...] = alpha.reshape(hq, 1) * acc_s[...] + pv
        m_s[...] = m_new.reshape(hq, 1)

    def reset_state():
        m_s[...] = jnp.full((hq, 1), NEG_INF, jnp.float32)
        l_s[...] = jnp.zeros((hq, 1), jnp.float32)
        acc_s[...] = jnp.zeros((hq, HEAD_DIM), jnp.float32)

    def normalized():
        return acc_s[...] / jnp.maximum(l_s[...], 1e-30)

    qn = (qn_ref[0].reshape(hq, HEAD_DIM) * scale).astype(bf16)
    lc = _nt_dot(qn, kc_ref[0]).reshape(n_heads, qb, n_cmp_pad)
    c_end = (lax.broadcasted_iota(jnp.int32, (qb, n_cmp_pad), 1) * NSA_CMP_STRIDE
             + (NSA_CMP_BLOCK - 1))
    c_mask = c_end <= q_pos
    lc = jnp.where(c_mask[None], lc, NEG_INF)
    mc = jnp.max(lc, axis=-1, keepdims=True)
    ec = jnp.where(c_mask[None], jnp.exp(lc - mc), 0.0)
    pc = ec / jnp.maximum(jnp.sum(ec, axis=-1, keepdims=True), 1e-30)
    o_c = jnp.dot(pc.reshape(hq, n_cmp_pad).astype(bf16), vc_ref[0],
                  preferred_element_type=jnp.float32)
    out_s[...] = gate_col(0) * o_c

    p_sum = jnp.sum(pc, axis=0)
    p_hi = p_sum.astype(bf16)
    p_lo = (p_sum - p_hi.astype(jnp.float32)).astype(bf16)
    imp = (jnp.dot(p_hi, ov_ref[...], preferred_element_type=jnp.float32)
           + jnp.dot(p_lo, ov_ref[...], preferred_element_type=jnp.float32))
    blk = lax.broadcasted_iota(jnp.int32, (qb, n_slc), 1)
    cur = q_pos // NSA_SLC_BLOCK
    forced = (blk == 0) | (blk == cur) | (blk == cur - 1)
    imp = jnp.where(forced, NSA_FORCE_SCORE, imp)
    valid = blk <= cur
    work = jnp.transpose(jnp.where(valid, imp, NEG_INF))
    blk_t = lax.broadcasted_iota(jnp.int32, (n_slc, qb), 0)
    sel_t = jnp.zeros((n_slc, qb), jnp.float32)
    for _ in range(min(NSA_TOP_N, n_slc)):
        mx = jnp.max(work, axis=0, keepdims=True)
        first = jnp.min(jnp.where(work == mx, blk_t, n_slc), axis=0, keepdims=True)
        pick = blk_t == first
        sel_t = jnp.where(pick, 1.0, sel_t)
        work = jnp.where(pick, -jnp.inf, work)
    sel_bf = jnp.where(valid, jnp.transpose(sel_t), 0.0).astype(bf16)

    qr = (qr_ref[0].reshape(hq, HEAD_DIM) * scale).astype(bf16)
    reset_state()
    w_len = NSA_WINDOW + qb
    w0 = pl.multiple_of(jnp.maximum(s0 - NSA_WINDOW, 0), qb)
    dist = q_pos - (w0 + lax.broadcasted_iota(jnp.int32, (qb, w_len), 1))
    masked_update(_nt_dot(qr, kw_ref[0, pl.ds(w0, w_len), :]), (dist >= 0) & (dist < NSA_WINDOW),
                  vw_ref[0, pl.ds(w0, w_len), :])
    out_s[...] += gate_col(2) * normalized()

    reset_state()
    kt = NSA_KV_TILE
    blocks_per_tile = kt // NSA_SLC_BLOCK
    n_tiles = (s0 + qb - 1) // kt + 1

    def sel_body(t, c):
        k0 = pl.multiple_of(t * kt, kt)
        k_idx = lax.broadcasted_iota(jnp.int32, (n_slc, kt), 1)
        expand = (lax.broadcasted_iota(jnp.int32, (n_slc, kt), 0)
                  == t * blocks_per_tile + k_idx // NSA_SLC_BLOCK)
        picked = jnp.dot(sel_bf, jnp.where(expand, 1.0, 0.0).astype(bf16),
                         preferred_element_type=jnp.float32)
        k_pos = k0 + lax.broadcasted_iota(jnp.int32, (qb, kt), 1)
        mask = (picked > 0.5) & (k_pos <= q_pos)
        masked_update(_nt_dot(qr, ks_ref[0, pl.ds(k0, kt), :]), mask, vs_ref[0, pl.ds(k0, kt), :])
        return c
    lax.fori_loop(0, n_tiles, sel_body, 0)
    out_s[...] += gate_col(1) * normalized()

    o_ref[0] = out_s[...].reshape(n_heads, qb, HEAD_DIM)


def nsa_attention_pallas(qn, qr, kc, vc, overlap, ks, vs, kw, vw, gl):
    b, h, s, dh = qn.shape
    qb = Q_BLOCK
    n_cmp_pad = kc.shape[1]
    n_slc = overlap.shape[1]
    kern = functools.partial(_nsa_kernel, n_heads=h, qb=qb, n_cmp_pad=n_cmp_pad, n_slc=n_slc)
    q_spec = pl.BlockSpec((1, h, qb, dh), lambda i, j: (i, 0, j, 0))
    cmp_spec = pl.BlockSpec((1, n_cmp_pad, dh), lambda i, j: (i, 0, 0))
    kv_spec = pl.BlockSpec((1, s, dh), lambda i, j: (i, 0, 0))
    return pl.pallas_call(
        kern,
        grid=(b, s // qb),
        in_specs=[q_spec, q_spec, cmp_spec, cmp_spec,
                  pl.BlockSpec((n_cmp_pad, n_slc), lambda i, j: (0, 0)),
                  kv_spec, kv_spec, kv_spec, kv_spec,
                  pl.BlockSpec((1, qb, 3 * h), lambda i, j: (i, j, 0))],
        out_specs=q_spec,
        out_shape=jax.ShapeDtypeStruct((b, h, s, dh), jnp.float32),
        scratch_shapes=[pltpu.VMEM((h * qb, 1), jnp.float32),
                        pltpu.VMEM((h * qb, 1), jnp.float32),
                        pltpu.VMEM((h * qb, dh), jnp.float32),
                        pltpu.VMEM((h * qb, dh), jnp.float32)],
        compiler_params=pltpu.CompilerParams(
            dimension_semantics=("arbitrary", "arbitrary"),
            vmem_limit_bytes=40 << 20),
        name="nsa_attention",
    )(qn, qr, kc, vc, overlap, ks, vs, kw, vw, gl)


def _dilated_kernel(q_ref, k_ref, v_ref, o_ref, *, qb):
    j = pl.program_id(1)
    s0 = j * qb
    bf16 = jnp.bfloat16
    scale = HEAD_DIM ** -0.5
    q_pos = s0 + lax.broadcasted_iota(jnp.int32, (qb, 1), 0)
    lane = lax.broadcasted_iota(jnp.int32, (qb, LANES), 1)
    first_head = lane < HEAD_DIM
    rows = DIL_HEADS_PER_GROUP * qb

    outs, lses = [], []
    for g, (window, dil) in enumerate(DIL_PAIRS):
        c0 = g * LANES
        q = q_ref[0, :, pl.ds(c0, LANES)] * scale
        q2 = jnp.concatenate([jnp.where(first_head, q, 0.0),
                              jnp.where(first_head, 0.0, q)], axis=0).astype(bf16)
        span = window + qb
        k0 = pl.multiple_of(jnp.maximum(s0 - window, 0), qb)
        dist = q_pos - (k0 + lax.broadcasted_iota(jnp.int32, (qb, span), 1))
        mask = (dist >= 0) & (dist <= window) & ((dist & (dil - 1)) == 0)
        logits = _nt_dot(q2, k_ref[0, pl.ds(k0, span), pl.ds(c0, LANES)])
        lg = jnp.where(mask[None], logits.reshape(DIL_HEADS_PER_GROUP, qb, span), NEG_INF)
        m = jnp.max(lg, axis=-1, keepdims=True)
        e = jnp.exp(lg - m)
        l = jnp.sum(e, axis=-1, keepdims=True).reshape(rows, 1)
        pv = jnp.dot(e.reshape(rows, span).astype(bf16), v_ref[0, pl.ds(k0, span), pl.ds(c0, LANES)],
                     preferred_element_type=jnp.float32)
        o2 = pv / l
        lse2 = m.reshape(rows, 1) + jnp.log(l)
        outs.append(jnp.where(first_head, o2[:qb], o2[qb:]))
        lses.append(jnp.where(first_head, lse2[:qb], lse2[qb:]))

    lse_max = functools.reduce(jnp.maximum, lses)
    ws = [jnp.exp(x - lse_max) for x in lses]
    den = functools.reduce(lambda a, b: a + b, ws)
    num = functools.reduce(lambda a, b: a + b, [w * o for w, o in zip(ws, outs)])
    o_ref[0] = num / den


def dilated_attention_pallas(q, k, v):
    b, s, w = q.shape
    qb = Q_BLOCK
    return pl.pallas_call(
        functools.partial(_dilated_kernel, qb=qb),
        grid=(b, s // qb),
        in_specs=[pl.BlockSpec((1, qb, w), lambda i, j: (i, j, 0)),
                  pl.BlockSpec((1, s, w), lambda i, j: (i, 0, 0)),
                  pl.BlockSpec((1, s, w), lambda i, j: (i, 0, 0))],
        out_specs=pl.BlockSpec((1, qb, DIL_OUT_W), lambda i, j: (i, j, 0)),
        out_shape=jax.ShapeDtypeStruct((b, s, DIL_OUT_W), jnp.float32),
        compiler_params=pltpu.CompilerParams(
            dimension_semantics=("arbitrary", "arbitrary"),
            vmem_limit_bytes=48 << 20),
        name="dilated_attention",
    )(q, k, v)


FOX_KV_TILE = 512


def _fox_kernel(q_ref, k_ref, v_ref, cq_ref, ck_ref, o_ref, m_s, l_s, acc_s, *, qb):
    j = pl.program_id(1)
    s0 = j * qb
    kt = FOX_KV_TILE
    bf16 = jnp.bfloat16
    scale = HEAD_DIM ** -0.5
    q_pos = s0 + lax.broadcasted_iota(jnp.int32, (qb, 1), 0)
    lane = lax.broadcasted_iota(jnp.int32, (qb, LANES), 1)
    first_head = lane < HEAD_DIM
    n_pairs = FOX_HEADS // 2
    rows = FOX_HEADS * qb
    n_tiles = (s0 + qb - 1) // kt + 1

    q2 = []
    for pair in range(n_pairs):
        q = q_ref[0, :, pl.ds(pair * LANES, LANES)] * scale
        q2.append(jnp.concatenate([jnp.where(first_head, q, 0.0),
                                   jnp.where(first_head, 0.0, q)], axis=0).astype(bf16))
    cq = [cq_ref[0, :, h:h + 1] for h in range(FOX_HEADS)]
    m_s[...] = jnp.full((rows, 1), NEG_INF, jnp.float32)
    l_s[...] = jnp.zeros((rows, 1), jnp.float32)
    acc_s[...] = jnp.zeros((rows, LANES), jnp.float32)

    def body(t, c):
        k0 = pl.multiple_of(t * kt, kt)
        k_pos = k0 + lax.broadcasted_iota(jnp.int32, (qb, kt), 1)
        mask = k_pos <= q_pos
        logits = jnp.concatenate(
            [_nt_dot(q2[pair], k_ref[0, pl.ds(k0, kt), pl.ds(pair * LANES, LANES)])
             for pair in range(n_pairs)], axis=0)
        bias = jnp.concatenate([cq[h] - ck_ref[0, h:h + 1, pl.ds(k0, kt)] for h in range(FOX_HEADS)],
                               axis=0)
        lg = jnp.where(mask[None], (logits + bias).reshape(FOX_HEADS, qb, kt), NEG_INF)
        m_old = m_s[...].reshape(FOX_HEADS, qb, 1)
        m_new = jnp.maximum(m_old, jnp.max(lg, axis=-1, keepdims=True))
        p = jnp.exp(lg - m_new)
        alpha = jnp.exp(m_old - m_new)
        l_s[...] = (alpha * l_s[...].reshape(FOX_HEADS, qb, 1)
                    + jnp.sum(p, axis=-1, keepdims=True)).reshape(rows, 1)
        pb = p.reshape(rows, kt).astype(bf16)
        pv = jnp.concatenate(
            [jnp.dot(pb[2 * pair * qb:2 * (pair + 1) * qb],
                     v_ref[0, pl.ds(k0, kt), pl.ds(pair * LANES, LANES)],
                     preferred_element_type=jnp.float32) for pair in range(n_pairs)], axis=0)
        acc_s[...] = alpha.reshape(rows, 1) * acc_s[...] + pv
        m_s[...] = m_new.reshape(rows, 1)
        return c
    lax.fori_loop(0, n_tiles, body, 0)

    o2 = acc_s[...] / l_s[...]
    for pair in range(n_pairs):
        r0 = 2 * pair * qb
        o_ref[0, :, pl.ds(pair * LANES, LANES)] = jnp.where(first_head, o2[r0:r0 + qb],
                                                             o2[r0 + qb:r0 + 2 * qb])


def fox_attention_pallas(q, k, v, cum):
    b, s, w = q.shape
    qb = Q_BLOCK
    return pl.pallas_call(
        functools.partial(_fox_kernel, qb=qb),
        grid=(b, s // qb),
        in_specs=[pl.BlockSpec((1, qb, w), lambda i, j: (i, j, 0)),
                  pl.BlockSpec((1, s, w), lambda i, j: (i, 0, 0)),
                  pl.BlockSpec((1, s, w), lambda i, j: (i, 0, 0)),
                  pl.BlockSpec((1, qb, FOX_HEADS), lambda i, j: (i, j, 0)),
                  pl.BlockSpec((1, FOX_HEADS, s), lambda i, j: (i, 0, 0))],
        out_specs=pl.BlockSpec((1, qb, w), lambda i, j: (i, j, 0)),
        out_shape=jax.ShapeDtypeStruct((b, s, w), jnp.float32),
        scratch_shapes=[pltpu.VMEM((FOX_HEADS * qb, 1), jnp.float32),
                        pltpu.VMEM((FOX_HEADS * qb, 1), jnp.float32),
                        pltpu.VMEM((FOX_HEADS * qb, LANES), jnp.float32)],
        compiler_params=pltpu.CompilerParams(
            dimension_semantics=("arbitrary", "arbitrary"),
            vmem_limit_bytes=40 << 20),
        name="fox_attention",
    )(q, k, v, cum, jnp.transpose(cum, (0, 2, 1)))


LRU_TILE = 512


def _lru_kernel(x_ref, gate_ref, cw_ref, cb_ref, wa_ref, ba_ref, wx_ref, bx_ref, lam_ref, o_ref,
                xbuf, a_s, u_s, h_s, hlast, *, tt):
    j = pl.program_id(1)
    c = x_ref.shape[-1]
    bf16 = jnp.bfloat16

    @pl.when(j == 0)
    def _():
        xbuf[pl.ds(0, SUBLANES), :] = jnp.zeros((SUBLANES, c), jnp.float32)
        hlast[...] = jnp.zeros((1, c), jnp.float32)

    x = x_ref[0]
    xbuf[pl.ds(SUBLANES, tt), :] = x
    xc = cb_ref[...] + cw_ref[LRU_CONV - 1:LRU_CONV, :] * x
    for back in range(1, LRU_CONV):
        xc = xc + cw_ref[LRU_CONV - 1 - back:LRU_CONV - back, :] * xbuf[pl.ds(SUBLANES - back, tt), :]
    xbuf[pl.ds(0, SUBLANES), :] = x[tt - SUBLANES:, :]

    xb = xc.astype(bf16)
    r = jax.nn.sigmoid(jnp.dot(xb, wa_ref[...], preferred_element_type=jnp.float32) + ba_ref[...])
    i_g = jax.nn.sigmoid(jnp.dot(xb, wx_ref[...], preferred_element_type=jnp.float32) + bx_ref[...])
    lam = lam_ref[...]
    softplus_neg = jnp.maximum(-lam, 0.0) + jnp.log1p(jnp.exp(-jnp.abs(lam)))
    log_a = -LRU_C * r * softplus_neg
    a_s[...] = jnp.exp(log_a)
    u_s[...] = jnp.sqrt(1.0 - jnp.exp(2.0 * log_a)) * (i_g * xc)

    def group(gi, h):
        r0 = pl.multiple_of(gi * SUBLANES, SUBLANES)
        a8 = a_s[pl.ds(r0, SUBLANES), :]
        u8 = u_s[pl.ds(r0, SUBLANES), :]
        rows = []
        for s in range(SUBLANES):
            h = a8[s:s + 1, :] * h + u8[s:s + 1, :]
            rows.append(h)
        h_s[pl.ds(r0, SUBLANES), :] = jnp.concatenate(rows, axis=0)
        return h
    hlast[...] = lax.fori_loop(0, tt // SUBLANES, group, hlast[...])
    o_ref[0] = h_s[...] * jax.nn.gelu(gate_ref[0])


def _block_diag(w):
    n, c, d = w.shape
    eye = jnp.eye(n, dtype=w.dtype)
    return (eye[:, None, :, None] * w[:, :, None, :]).reshape(n * c, n * d)


def rg_lru_pallas(x_in, gate, conv_w, conv_b, w_a, b_a, w_x, b_x, lam):
    b, s, c = x_in.shape
    tt = LRU_TILE
    row = lambda v: v.reshape(1, c)
    tok_spec = pl.BlockSpec((1, tt, c), lambda i, j: (i, j, 0))
    full = lambda shape: pl.BlockSpec(shape, lambda i, j: (0,) * len(shape))
    return pl.pallas_call(
        functools.partial(_lru_kernel, tt=tt),
        grid=(b, s // tt),
        in_specs=[tok_spec, tok_spec, full((LRU_CONV, c)), full((1, c)), full((c, c)), full((1, c)),
                  full((c, c)), full((1, c)), full((1, c))],
        out_specs=tok_spec,
        out_shape=jax.ShapeDtypeStruct((b, s, c), jnp.float32),
        scratch_shapes=[pltpu.VMEM((tt + SUBLANES, c), jnp.float32),
                        pltpu.VMEM((tt, c), jnp.float32),
                        pltpu.VMEM((tt, c), jnp.float32),
                        pltpu.VMEM((tt, c), jnp.float32),
                        pltpu.VMEM((1, c), jnp.float32)],
        compiler_params=pltpu.CompilerParams(dimension_semantics=("arbitrary", "arbitrary")),
        name="rg_lru",
    )(x_in, gate, conv_w, row(conv_b), _block_diag(w_a).astype(jnp.bfloat16), row(b_a),
      _block_diag(w_x).astype(jnp.bfloat16), row(b_x), row(lam))


DENSE_ROWS = 512


def _norm_matmul_kernel(x_ref, g_ref, w_ref, o_ref, xn_s):
    @pl.when(pl.program_id(1) == 0)
    def _():
        x = x_ref[...]
        y = x * lax.rsqrt(jnp.mean(x * x, axis=-1, keepdims=True) + NORM_EPS) * g_ref[...]
        xn_s[...] = y.astype(jnp.bfloat16)
    o_ref[...] = jnp.dot(xn_s[...], w_ref[...], preferred_element_type=jnp.float32)


def norm_matmul(x, g, w, tn):
    t, d = x.shape
    n = w.shape[1]
    tm = DENSE_ROWS
    return pl.pallas_call(
        _norm_matmul_kernel,
        grid=(t // tm, n // tn),
        in_specs=[pl.BlockSpec((tm, d), lambda i, j: (i, 0)),
                  pl.BlockSpec((1, d), lambda i, j: (0, 0)),
                  pl.BlockSpec((d, tn), lambda i, j: (0, j))],
        out_specs=pl.BlockSpec((tm, tn), lambda i, j: (i, j)),
        out_shape=jax.ShapeDtypeStruct((t, n), jnp.float32),
        scratch_shapes=[pltpu.VMEM((tm, d), jnp.bfloat16)],
        compiler_params=pltpu.CompilerParams(dimension_semantics=("arbitrary", "arbitrary"),
                                             vmem_limit_bytes=40 << 20),
        name="norm_matmul",
    )(x, g.reshape(1, d), w)


def _merge_out_kernel(x_ref, mg_ref, of_ref, ol_ref, od_ref, on_ref, wf_ref, wl_ref, wd_ref, wn_ref,
                      wo_ref, g2_ref, y_ref, h2_ref, *, d):
    bf16 = jnp.bfloat16
    merged = None
    for b, (o_ref, w_ref) in enumerate(((of_ref, wf_ref), (ol_ref, wl_ref), (od_ref, wd_ref),
                                        (on_ref, wn_ref))):
        y = jnp.dot(o_ref[...].astype(bf16), w_ref[...], preferred_element_type=jnp.float32)
        term = jax.nn.sigmoid(mg_ref[:, pl.ds(b * d, d)]) * y
        merged = term if merged is None else merged + term
    y = x_ref[...] + jnp.dot(merged.astype(bf16), wo_ref[...], preferred_element_type=jnp.float32)
    y_ref[...] = y
    h2_ref[...] = y * lax.rsqrt(jnp.mean(y * y, axis=-1, keepdims=True) + NORM_EPS) * g2_ref[...]


def merge_out(x, mg, o_fox, o_lru, o_dil, o_nsa, w_fox, w_lru, w_dil, w_nsa, w_out, g2):
    t, d = x.shape
    tm = DENSE_ROWS // 2
    bf16 = jnp.bfloat16
    tok = lambda a: pl.BlockSpec((tm, a.shape[1]), lambda i: (i, 0))
    full = lambda a: pl.BlockSpec(a.shape, lambda i: (0, 0))
    ws = [w.astype(bf16) for w in (w_fox, w_lru, w_dil, w_nsa, w_out)]
    g2r = g2.reshape(1, d)
    acts = (x, mg, o_fox, o_lru, o_dil, o_nsa)
    return pl.pallas_call(
        functools.partial(_merge_out_kernel, d=d),
        grid=(t // tm,),
        in_specs=[tok(a) for a in acts] + [full(w) for w in ws] + [full(g2r)],
        out_specs=[pl.BlockSpec((tm, d), lambda i: (i, 0)), pl.BlockSpec((tm, d), lambda i: (i, 0))],
        out_shape=[jax.ShapeDtypeStruct((t, d), jnp.float32), jax.ShapeDtypeStruct((t, d), jnp.float32)],
        compiler_params=pltpu.CompilerParams(dimension_semantics=("arbitrary",),
                                             vmem_limit_bytes=48 << 20),
        name="merge_out",
    )(*acts, *ws, g2r)


def rope(x, pos):
    half = x.shape[-1] // 2
    freqs = ROPE_THETA ** (-jnp.arange(half, dtype=jnp.float32) / half)
    ang = pos.astype(jnp.float32)[:, None] * freqs[None, :]
    cos = jnp.cos(ang)[None, :, None, :]
    sin = jnp.sin(ang)[None, :, None, :]
    xf = x.astype(jnp.float32)
    x1, x2 = xf[..., :half], xf[..., half:]
    return jnp.concatenate([x1 * cos - x2 * sin, x2 * cos + x1 * sin], axis=-1).astype(x.dtype)


def nsa_branch(nq, nkc, nvc, nks, nvs, nkw, nvw, ng, pe_k, pe_v, w1_k, w2_k, w1_v, w2_v):
    B, S, _ = nq.shape
    dh = HEAD_DIM
    bf16 = jnp.bfloat16
    pos = jnp.arange(S)
    q4 = nq.reshape(B, S, NSA_HEADS, dh)
    qn = jnp.transpose(q4, (0, 2, 1, 3))
    qr = jnp.transpose(rope(q4, pos), (0, 2, 1, 3))
    ks = rope(nks.reshape(B, S, 1, dh), pos).reshape(B, S, dh).astype(bf16)
    kw = rope(nkw.reshape(B, S, 1, dh), pos).reshape(B, S, dh).astype(bf16)

    n_cmp = (S - NSA_CMP_BLOCK) // NSA_CMP_STRIDE + 1
    n_cmp_pad = S // NSA_CMP_STRIDE
    n_slc = S // NSA_SLC_BLOCK

    def compress(kv, pe, w1, w2):
        chunks = kv.reshape(B, n_cmp_pad, NSA_CMP_STRIDE * dh)
        flat = jnp.concatenate([chunks[:, :-1], chunks[:, 1:]], axis=-1) + pe.reshape(-1)
        out = jax.nn.gelu(flat @ w1) @ w2
        return jnp.pad(out, ((0, 0), (0, n_cmp_pad - n_cmp), (0, 0))).astype(bf16)

    kc = compress(nkc, pe_k, w1_k, w2_k)
    vc = compress(nvc, pe_v, w1_v, w2_v)
    cmp_start = jnp.arange(n_cmp_pad) * NSA_CMP_STRIDE
    slc_start = jnp.arange(n_slc) * NSA_SLC_BLOCK
    overlap = ((cmp_start[:, None] < slc_start[None, :] + NSA_SLC_BLOCK)
               & (cmp_start[:, None] + NSA_CMP_BLOCK > slc_start[None, :])
               & (jnp.arange(n_cmp_pad)[:, None] < n_cmp)).astype(bf16)
    out = nsa_attention_pallas(qn, qr, kc, vc, overlap, ks, nvs.astype(bf16), kw,
                               nvw.astype(bf16), ng)
    return jnp.transpose(out, (0, 2, 1, 3)).reshape(B, S, NSA_W)


def peer_ffn(x_res, h2, w_q, sub_keys, u, v):
    expert, gt = peer_route(h2, w_q, sub_keys)
    uv = jnp.concatenate([u, v], axis=1)
    return peer_mix(h2, x_res, expert, gt, uv)


def kernel(x, norm1_g, w_in, fox_b_f, lru_conv_w, lru_conv_b, lru_w_a, lru_b_a, lru_w_x, lru_b_x, lru_lambda, nsa_pe_k, nsa_pe_v, nsa_w1_k, nsa_w2_k, nsa_w1_v, nsa_w2_v, w_br_fox, w_br_lru, w_br_dil, w_br_nsa, w_out, norm2_g, peer_w_q, peer_sub_keys, peer_u, peer_v, final_g):
    B, S, D = x.shape
    T = B * S
    pos = jnp.arange(S)
    n_mix = sum(IN_SPLITS[:-1])
    mix_tile = 640
    n_mix_pad = -(-n_mix // mix_tile) * mix_tile
    split_points = np.cumsum(IN_SPLITS[:-1])[:-1].tolist()
    bf16 = jnp.bfloat16

    def heads(t, n):
        return t.reshape(B, S, n, HEAD_DIM)

    x = x.reshape(T, D)
    for l in range(DEPTH):
        w_mix = jnp.pad(w_in[l][:, :n_mix], ((0, 0), (0, n_mix_pad - n_mix))).astype(bf16)
        proj = norm_matmul(x, norm1_g[l], w_mix, mix_tile)[:, :n_mix].reshape(B, S, n_mix)
        mg = norm_matmul(x, norm1_g[l], w_in[l][:, n_mix:].astype(bf16), D)
        (fq, fk, fv, ff, lx, lg, dq, dk, dv, nq, nkc, nvc, nks, nvs, nkw, nvw, ng) = \
            jnp.split(proj, split_points, axis=-1)

        cum = jnp.cumsum(jax.nn.log_sigmoid(ff + fox_b_f[l]), axis=1)
        o_fox = fox_attention_pallas(fq, fk.astype(jnp.bfloat16), fv.astype(jnp.bfloat16), cum)
        o_lru = rg_lru_pallas(lx, lg, lru_conv_w[l], lru_conv_b[l], lru_w_a[l], lru_b_a[l],
                              lru_w_x[l], lru_b_x[l], lru_lambda[l])
        o_dil = dilated_attention_pallas(
            rope(heads(dq, DIL_HEADS), pos).reshape(B, S, DIL_W),
            rope(heads(dk, DIL_HEADS), pos).reshape(B, S, DIL_W).astype(jnp.bfloat16),
            dv.astype(jnp.bfloat16))
        o_nsa = nsa_branch(nq, nkc, nvc, nks, nvs, nkw, nvw, ng,
                           nsa_pe_k[l], nsa_pe_v[l], nsa_w1_k[l], nsa_w2_k[l],
                           nsa_w1_v[l], nsa_w2_v[l])

        x, h2 = merge_out(x, mg, o_fox.reshape(T, FOX_W), o_lru.reshape(T, LRU_WIDTH),
                          o_dil.reshape(T, DIL_OUT_W), o_nsa.reshape(T, NSA_W),
                          w_br_fox[l], w_br_lru[l], w_br_dil[l], w_br_nsa[l], w_out[l], norm2_g[l])
        x = peer_ffn(x, h2, peer_w_q[l], peer_sub_keys[l], peer_u[l], peer_v[l])

    return rms_norm_pallas(x, final_g).reshape(B, S, D)
```

```python
import functools

import jax
import jax.numpy as jnp
import numpy as np
from jax import lax
from jax.experimental import pallas as pl
from jax.experimental.pallas import tpu as pltpu

D_MODEL = 1024
DEPTH = 2
HEAD_DIM = 64
Q_BLOCK = 128
ROPE_THETA = 10000.0
NORM_EPS = 1e-6
NEG_INF = -1e30
N_BRANCHES = 4
FOX_HEADS = 4
FOX_W = FOX_HEADS * HEAD_DIM
LRU_WIDTH = 256
LRU_BLOCKS = 4
LRU_CONV = 4
LRU_C = 8.0
DIL_PAIRS = ((128, 1), (512, 4), (2048, 16))
DIL_GROUPS = len(DIL_PAIRS)
DIL_HEADS_PER_GROUP = 2
DIL_HEADS = DIL_GROUPS * DIL_HEADS_PER_GROUP
DIL_W = DIL_HEADS * HEAD_DIM
DIL_OUT_W = DIL_HEADS_PER_GROUP * HEAD_DIM
DIL_KEYS = DIL_PAIRS[0][0] // DIL_PAIRS[0][1] + 1
NSA_HEADS = 4
NSA_KV_HEADS = 1
NSA_W = NSA_HEADS * HEAD_DIM
NSA_KV_W = NSA_KV_HEADS * HEAD_DIM
NSA_CMP_BLOCK = 32
NSA_CMP_STRIDE = 16
NSA_SLC_BLOCK = 64
NSA_TOP_N = 16
NSA_WINDOW = 512
NSA_FORCE_SCORE = 1e6
PEER_HEADS = 8
PEER_N_KEYS = 128
PEER_TOPK = 16
PEER_D_KEY = 256
PEER_BLOCK = 128

IN_SPLITS = (FOX_W, FOX_W, FOX_W, FOX_HEADS,
             LRU_WIDTH, LRU_WIDTH,
             DIL_W, DIL_W, DIL_W,
             NSA_W, NSA_KV_W, NSA_KV_W, NSA_KV_W, NSA_KV_W, NSA_KV_W, NSA_KV_W, NSA_HEADS * 3,
             N_BRANCHES * D_MODEL)


def _rms_norm_kernel(x_ref, g_ref, o_ref):
    x = x_ref[...]
    ms = jnp.mean(x * x, axis=-1, keepdims=True)
    o_ref[...] = x * lax.rsqrt(ms + NORM_EPS) * g_ref[...]


def rms_norm_pallas(x, g, rows=512):
    t, d = x.shape
    return pl.pallas_call(
        _rms_norm_kernel,
        grid=(t // rows,),
        in_specs=[pl.BlockSpec((rows, d), lambda i: (i, 0)),
                  pl.BlockSpec((1, d), lambda i: (0, 0))],
        out_specs=pl.BlockSpec((rows, d), lambda i: (i, 0)),
        out_shape=jax.ShapeDtypeStruct((t, d), x.dtype),
        name="rms_norm",
    )(x, g.reshape(1, d))


def _gelu_tanh(x):
    return 0.5 * x * (1.0 + jnp.tanh(0.7978845608028654 * (x + 0.044715 * x * x * x)))


PEER_TOKENS_PER_STEP = 8
LANES = 128
SUBLANES = 8


def _peer_mix_kernel(idx_hbm, x_ref, gt_ref, res_ref, uv_hbm, o_ref,
                     idx_smem, buf, idx_sem, row_sem, *, tb, n_sel, d):
    i = pl.program_id(0)
    n = pl.num_programs(0)
    rows = tb * n_sel
    n_chunks = d // LANES
    groups = n_sel // SUBLANES

    def idx_copy(step, s):
        return pltpu.make_async_copy(idx_hbm.at[step], idx_smem.at[pl.ds(s * rows, rows)],
                                     idx_sem.at[s])

    def row_copy(s, j, row):
        return pltpu.make_async_copy(uv_hbm.at[row], buf.at[s, j // SUBLANES, pl.ds(j % SUBLANES, 1)],
                                     row_sem.at[s])

    def slot_wait(s):
        pltpu.make_async_copy(buf.at[s], buf.at[s], row_sem.at[s]).wait()

    @pl.when(i == 0)
    def _():
        idx_copy(0, 0).start()
        idx_copy(0, 0).wait()

        def body(jj, c):
            for k in range(SUBLANES):
                row = idx_smem[jj * SUBLANES + k]
                pltpu.make_async_copy(uv_hbm.at[row], buf.at[0, jj, pl.ds(k, 1)],
                                      row_sem.at[0]).start(priority=k % 2)
            return c
        lax.fori_loop(0, rows // SUBLANES, body, 0)
        idx_copy(1, 1).start()

    def step(cur):
        nxt = 1 - cur
        idx_copy(i + 1, nxt).wait()

        @pl.when(i + 2 <= n)
        def _():
            idx_copy(i + 2, cur).start()

        slot_wait(cur)

        def issue(t, part):
            lo = part * groups // (2 * n_chunks)
            hi = (part + 1) * groups // (2 * n_chunks)
            for j in range(t * n_sel + lo * SUBLANES, t * n_sel + hi * SUBLANES):
                row_copy(nxt, j, idx_smem[nxt * rows + j]).start(priority=j % 2)

        def sel_rows(t, lane0):
            tile = buf[cur, pl.ds(t * groups, groups), :, pl.ds(lane0, LANES)]
            return tile.reshape(n_sel, LANES)

        for t in range(tb):
            acc = jnp.zeros((n_sel, LANES), jnp.float32)
            for c in range(n_chunks):
                issue(t, c)
                xc = x_ref[pl.ds(t, 1), pl.ds(c * LANES, LANES)]
                acc = acc + sel_rows(t, c * LANES) * xc
            s = jnp.sum(acc, axis=-1, keepdims=True)
            w = gt_ref[0, :, pl.ds(t, 1)] * _gelu_tanh(s)
            wb = jnp.broadcast_to(w, (n_sel, LANES))
            for c in range(n_chunks):
                issue(t, n_chunks + c)
                vc = sel_rows(t, d + c * LANES)
                oc = jnp.sum(wb * vc, axis=0, keepdims=True)
                o_ref[pl.ds(t, 1), pl.ds(c * LANES, LANES)] = (
                    res_ref[pl.ds(t, 1), pl.ds(c * LANES, LANES)] + oc)

        @pl.when(i == n - 1)
        def _():
            slot_wait(nxt)

    @pl.when(i % 2 == 0)
    def _():
        step(0)

    @pl.when(i % 2 == 1)
    def _():
        step(1)


def peer_mix(x, res, idx, gt_blocks, uv):
    t, d = x.shape
    n_sel = idx.shape[1]
    tb = PEER_TOKENS_PER_STEP
    steps = t // tb
    spare = (jnp.arange(tb * n_sel, dtype=jnp.int32) % uv.shape[0]).reshape(1, tb * n_sel)
    idx_steps = jnp.concatenate([idx.reshape(steps, tb * n_sel), spare], axis=0)
    nb, _, tq = gt_blocks.shape
    gt = jnp.transpose(gt_blocks.reshape(nb, n_sel, tq // tb, tb), (0, 2, 1, 3)).reshape(steps, n_sel, tb)
    kern = functools.partial(_peer_mix_kernel, tb=tb, n_sel=n_sel, d=d)
    return pl.pallas_call(
        kern,
        grid=(steps,),
        in_specs=[pl.BlockSpec(memory_space=pl.ANY),
                  pl.BlockSpec((tb, d), lambda i: (i, 0)),
                  pl.BlockSpec((1, n_sel, tb), lambda i: (i, 0, 0)),
                  pl.BlockSpec((tb, d), lambda i: (i, 0)),
                  pl.BlockSpec(memory_space=pl.ANY)],
        out_specs=pl.BlockSpec((tb, d), lambda i: (i, 0)),
        out_shape=jax.ShapeDtypeStruct((t, d), jnp.float32),
        scratch_shapes=[pltpu.SMEM((2 * tb * n_sel,), jnp.int32),
                        pltpu.VMEM((2, tb * n_sel // SUBLANES, SUBLANES, 2 * d), jnp.float32),
                        pltpu.SemaphoreType.DMA((2,)),
                        pltpu.SemaphoreType.DMA((2,))],
        compiler_params=pltpu.CompilerParams(
            dimension_semantics=("arbitrary",),
            vmem_limit_bytes=2 * tb * n_sel * 2 * d * 4 + (8 << 20)),
        name="peer_mix",
    )(idx_steps, x, gt, res, uv.reshape(uv.shape[0], 1, 2 * d))


def _topk_rows(work, k):
    n = work.shape[0]
    row = lax.broadcasted_iota(jnp.int32, work.shape, 0)
    vals, idxs = [], []
    for _ in range(k):
        mx = jnp.max(work, axis=0, keepdims=True)
        pos = jnp.min(jnp.where(work == mx, row, n), axis=0, keepdims=True)
        work = jnp.where(row == pos, -jnp.inf, work)
        vals.append(mx)
        idxs.append(pos)
    return jnp.concatenate(vals, axis=0), jnp.concatenate(idxs, axis=0)


def _peer_route_kernel(x_ref, wq_ref, keys_ref, idx_ref, gt_ref, q_s, idx_s, g_s, *, tq):
    bf16 = jnp.bfloat16
    k = PEER_TOPK
    half = PEER_D_KEY // 2
    q = jnp.dot(x_ref[...].astype(bf16), wq_ref[...], preferred_element_type=jnp.float32)
    for c in range(2 * PEER_HEADS):
        q_s[c] = q[:, c * half:(c + 1) * half].astype(bf16)

    def head_body(h, carry):
        tops = []
        for p in range(2):
            scores = _nt_dot(keys_ref[p], q_s[2 * h + p])
            tops.append(_topk_rows(scores, k))
        (v0, i0), (v1, i1) = tops
        keep = [k // (a + 1) for a in range(k)]
        n_cand = sum(keep)
        n_pad = -n_cand % SUBLANES
        cand = jnp.concatenate([v0[a:a + 1, :] + v1[:keep[a]] for a in range(k)]
                               + [jnp.full((n_pad, tq), -jnp.inf, jnp.float32)], axis=0)
        cand_idx = jnp.concatenate([i0[a:a + 1, :] * PEER_N_KEYS + i1[:keep[a]] for a in range(k)]
                                   + [jnp.full((n_pad, tq), -1, jnp.int32)], axis=0)
        flat = lax.broadcasted_iota(jnp.int32, cand.shape, 0)
        top_s, experts = [], []
        for _ in range(k):
            mx = jnp.max(cand, axis=0, keepdims=True)
            pos = jnp.min(jnp.where(cand == mx, flat, n_cand + n_pad), axis=0, keepdims=True)
            pick = flat == pos
            experts.append(jnp.max(jnp.where(pick, cand_idx, -1), axis=0, keepdims=True))
            cand = jnp.where(pick, -jnp.inf, cand)
            top_s.append(mx)
        top_s = jnp.concatenate(top_s, axis=0)
        ex = jnp.exp(top_s - jnp.max(top_s, axis=0, keepdims=True))
        r0 = pl.multiple_of(h * k, k)
        g_s[pl.ds(r0, k), :] = ex / jnp.sum(ex, axis=0, keepdims=True)
        idx_s[pl.ds(r0, k), :] = jnp.concatenate(experts, axis=0)
        return carry
    lax.fori_loop(0, PEER_HEADS, head_body, 0)

    idx_ref[...] = jnp.transpose(idx_s[...])
    gt_ref[0] = g_s[...]


def peer_route(h2, w_q, sub_keys):
    t, d = h2.shape
    tq = PEER_BLOCK
    n_sel = PEER_HEADS * PEER_TOPK
    half = PEER_D_KEY // 2
    return pl.pallas_call(
        functools.partial(_peer_route_kernel, tq=tq),
        grid=(t // tq,),
        in_specs=[pl.BlockSpec((tq, d), lambda i: (i, 0)),
                  pl.BlockSpec((d, PEER_HEADS * PEER_D_KEY), lambda i: (0, 0)),
                  pl.BlockSpec((2, PEER_N_KEYS, half), lambda i: (0, 0, 0))],
        out_specs=[pl.BlockSpec((tq, n_sel), lambda i: (i, 0)),
                   pl.BlockSpec((1, n_sel, tq), lambda i: (i, 0, 0))],
        out_shape=[jax.ShapeDtypeStruct((t, n_sel), jnp.int32),
                   jax.ShapeDtypeStruct((t // tq, n_sel, tq), jnp.float32)],
        scratch_shapes=[pltpu.VMEM((2 * PEER_HEADS, tq, half), jnp.bfloat16),
                        pltpu.VMEM((n_sel, tq), jnp.int32),
                        pltpu.VMEM((n_sel, tq), jnp.float32)],
        compiler_params=pltpu.CompilerParams(dimension_semantics=("arbitrary",),
                                             vmem_limit_bytes=32 << 20),
        name="peer_route",
    )(h2, w_q.astype(jnp.bfloat16), sub_keys.astype(jnp.bfloat16))


NSA_KV_TILE = 512


def _nt_dot(a, b):
    return lax.dot_general(a, b, (((1,), (1,)), ((), ())), preferred_element_type=jnp.float32)


def _nsa_kernel(qn_ref, qr_ref, kc_ref, vc_ref, ov_ref, ks_ref, vs_ref, kw_ref, vw_ref, gl_ref,
                o_ref, m_s, l_s, acc_s, out_s, la_s, lb_s, *, n_heads, qb, n_cmp_pad, n_slc):
    j = pl.program_id(1)
    s0 = j * qb
    hq = n_heads * qb
    bf16 = jnp.bfloat16
    scale = HEAD_DIM ** -0.5
    q_pos = s0 + lax.broadcasted_iota(jnp.int32, (qb, 1), 0)

    gates = jax.nn.sigmoid(gl_ref[0])

    def gate_col(br):
        return jnp.concatenate([gates[:, 3 * h + br:3 * h + br + 1] for h in range(n_heads)], axis=0)

    def masked_update(logits, mask, v_tile):
        kt = logits.shape[-1]
        lg = jnp.where(mask[None], logits.reshape(n_heads, qb, kt), NEG_INF)
        m_old = m_s[...].reshape(n_heads, qb, 1)
        m_new = jnp.maximum(m_old, jnp.max(lg, axis=-1, keepdims=True))
        p = jnp.where(mask[None], jnp.exp(lg - m_new), 0.0)
        alpha = jnp.exp(m_old - m_new)
        l_s[...] = (alpha * l_s[...].reshape(n_heads, qb, 1)
                    + jnp.sum(p, axis=-1, keepdims=True)).reshape(hq, 1)
        pv = jnp.dot(p.reshape(hq, kt).astype(bf16), v_tile, preferred_element_type=jnp.float32)
        acc_s[...] = alpha.reshape(hq, 1) * acc_s[...] + pv
        m_s[...] = m_new.reshape(hq, 1)

    def reset_state():
        m_s[...] = jnp.full((hq, 1), NEG_INF, jnp.float32)
        l_s[...] = jnp.zeros((hq, 1), jnp.float32)
        acc_s[...] = jnp.zeros((hq, HEAD_DIM), jnp.float32)

    def normalized():
        return acc_s[...] / jnp.maximum(l_s[...], 1e-30)

    qn = (qn_ref[0].reshape(hq, HEAD_DIM) * scale).astype(bf16)
    lc = _nt_dot(qn, kc_ref[0]).reshape(n_heads, qb, n_cmp_pad)
    c_end = (lax.broadcasted_iota(jnp.int32, (qb, n_cmp_pad), 1) * NSA_CMP_STRIDE
             + (NSA_CMP_BLOCK - 1))
    c_mask = c_end <= q_pos
    lc = jnp.where(c_mask[None], lc, NEG_INF)
    mc = jnp.max(lc, axis=-1, keepdims=True)
    ec = jnp.where(c_mask[None], jnp.exp(lc - mc), 0.0)
    pc = ec / jnp.maximum(jnp.sum(ec, axis=-1, keepdims=True), 1e-30)
    o_c = jnp.dot(pc.reshape(hq, n_cmp_pad).astype(bf16), vc_ref[0],
                  preferred_element_type=jnp.float32)
    out_s[...] = gate_col(0) * o_c

    p_sum = jnp.sum(pc, axis=0)
    p_hi = p_sum.astype(bf16)
    p_lo = (p_sum - p_hi.astype(jnp.float32)).astype(bf16)
    imp = (jnp.dot(p_hi, ov_ref[...], preferred_element_type=jnp.float32)
           + jnp.dot(p_lo, ov_ref[...], preferred_element_type=jnp.float32))
    blk = lax.broadcasted_iota(jnp.int32, (qb, n_slc), 1)
    cur = q_pos // NSA_SLC_BLOCK
    forced = (blk == 0) | (blk == cur) | (blk == cur - 1)
    imp = jnp.where(forced, NSA_FORCE_SCORE, imp)
    valid = blk <= cur
    work = jnp.transpose(jnp.where(valid, imp, NEG_INF))
    blk_t = lax.broadcasted_iota(jnp.int32, (n_slc, qb), 0)
    sel_t = jnp.zeros((n_slc, qb), jnp.float32)
    for _ in range(min(NSA_TOP_N, n_slc)):
        mx = jnp.max(work, axis=0, keepdims=True)
        first = jnp.min(jnp.where(work == mx, blk_t, n_slc), axis=0, keepdims=True)
        pick = blk_t == first
        sel_t = jnp.where(pick, 1.0, sel_t)
        work = jnp.where(pick, -jnp.inf, work)
    sel_bf = jnp.where(valid, jnp.transpose(sel_t), 0.0).astype(bf16)

    qr = (qr_ref[0].reshape(hq, HEAD_DIM) * scale).astype(bf16)
    reset_state()
    w_len = NSA_WINDOW + qb
    w0 = pl.multiple_of(jnp.maximum(s0 - NSA_WINDOW, 0), qb)
    dist = q_pos - (w0 + lax.broadcasted_iota(jnp.int32, (qb, w_len), 1))
    masked_update(_nt_dot(qr, kw_ref[0, pl.ds(w0, w_len), :]), (dist >= 0) & (dist < NSA_WINDOW),
                  vw_ref[0, pl.ds(w0, w_len), :])
    out_s[...] += gate_col(2) * normalized()

    reset_state()
    kt = NSA_KV_TILE
    blocks_per_tile = kt // NSA_SLC_BLOCK
    n_tiles = (s0 + qb - 1) // kt + 1

    def tile_start(t):
        return pl.multiple_of(jnp.minimum(t, n_tiles - 1) * kt, kt)

    def scores(t):
        return _nt_dot(qr, ks_ref[0, pl.ds(tile_start(t), kt), :])

    def consume(logits, t):
        k_idx = lax.broadcasted_iota(jnp.int32, (n_slc, kt), 1)
        expand = (lax.broadcasted_iota(jnp.int32, (n_slc, kt), 0)
                  == t * blocks_per_tile + k_idx // NSA_SLC_BLOCK)
        picked = jnp.dot(sel_bf, jnp.where(expand, 1.0, 0.0).astype(bf16),
                         preferred_element_type=jnp.float32)
        k_pos = t * kt + lax.broadcasted_iota(jnp.int32, (qb, kt), 1)
        mask = (picked > 0.5) & (k_pos <= q_pos)
        masked_update(logits, mask, vs_ref[0, pl.ds(tile_start(t), kt), :])

    la_s[...] = scores(0)

    def sel_body(u, c):
        t = 2 * u
        lb_s[...] = scores(t + 1)
        consume(la_s[...], t)
        la_s[...] = scores(t + 2)
        consume(lb_s[...], t + 1)
        return c
    lax.fori_loop(0, (n_tiles + 1) // 2, sel_body, 0)
    out_s[...] += gate_col(1) * normalized()

    o_ref[0] = out_s[...].reshape(n_heads, qb, HEAD_DIM)


def nsa_attention_pallas(qn, qr, kc, vc, overlap, ks, vs, kw, vw, gl):
    b, h, s, dh = qn.shape
    qb = Q_BLOCK
    n_cmp_pad = kc.shape[1]
    n_slc = overlap.shape[1]
    kern = functools.partial(_nsa_kernel, n_heads=h, qb=qb, n_cmp_pad=n_cmp_pad, n_slc=n_slc)
    q_spec = pl.BlockSpec((1, h, qb, dh), lambda i, j: (i, 0, j, 0))
    cmp_spec = pl.BlockSpec((1, n_cmp_pad, dh), lambda i, j: (i, 0, 0))
    kv_spec = pl.BlockSpec((1, s, dh), lambda i, j: (i, 0, 0))
    return pl.pallas_call(
        kern,
        grid=(b, s // qb),
        in_specs=[q_spec, q_spec, cmp_spec, cmp_spec,
                  pl.BlockSpec((n_cmp_pad, n_slc), lambda i, j: (0, 0)),
                  kv_spec, kv_spec, kv_spec, kv_spec,
                  pl.BlockSpec((1, qb, 3 * h), lambda i, j: (i, j, 0))],
        out_specs=q_spec,
        out_shape=jax.ShapeDtypeStruct((b, h, s, dh), jnp.float32),
        scratch_shapes=[pltpu.VMEM((h * qb, 1), jnp.float32),
                        pltpu.VMEM((h * qb, 1), jnp.float32),
                        pltpu.VMEM((h * qb, dh), jnp.float32),
                        pltpu.VMEM((h * qb, dh), jnp.float32),
                        pltpu.VMEM((h * qb, NSA_KV_TILE), jnp.float32),
                        pltpu.VMEM((h * qb, NSA_KV_TILE), jnp.float32)],
        compiler_params=pltpu.CompilerParams(
            dimension_semantics=("arbitrary", "arbitrary"),
            vmem_limit_bytes=40 << 20),
        name="nsa_attention",
    )(qn, qr, kc, vc, overlap, ks, vs, kw, vw, gl)


def _dilated_kernel(q_ref, k_ref, v_ref, o_ref, *, qb):
    j = pl.program_id(1)
    s0 = j * qb
    bf16 = jnp.bfloat16
    scale = HEAD_DIM ** -0.5
    q_pos = s0 + lax.broadcasted_iota(jnp.int32, (qb, 1), 0)
    lane = lax.broadcasted_iota(jnp.int32, (qb, LANES), 1)
    first_head = lane < HEAD_DIM
    rows = DIL_HEADS_PER_GROUP * qb

    outs, lses = [], []
    for g, (window, dil) in enumerate(DIL_PAIRS):
        c0 = g * LANES
        q = q_ref[0, :, pl.ds(c0, LANES)] * scale
        q2 = jnp.concatenate([jnp.where(first_head, q, 0.0),
                              jnp.where(first_head, 0.0, q)], axis=0).astype(bf16)
        span = window + qb
        k0 = pl.multiple_of(jnp.maximum(s0 - window, 0), qb)
        dist = q_pos - (k0 + lax.broadcasted_iota(jnp.int32, (qb, span), 1))
        mask = (dist >= 0) & (dist <= window) & ((dist & (dil - 1)) == 0)
        logits = _nt_dot(q2, k_ref[0, pl.ds(k0, span), pl.ds(c0, LANES)])
        lg = jnp.where(mask[None], logits.reshape(DIL_HEADS_PER_GROUP, qb, span), NEG_INF)
        m = jnp.max(lg, axis=-1, keepdims=True)
        e = jnp.exp(lg - m)
        l = jnp.sum(e, axis=-1, keepdims=True).reshape(rows, 1)
        pv = jnp.dot(e.reshape(rows, span).astype(bf16), v_ref[0, pl.ds(k0, span), pl.ds(c0, LANES)],
                     preferred_element_type=jnp.float32)
        o2 = pv / l
        lse2 = m.reshape(rows, 1) + jnp.log(l)
        outs.append(jnp.where(first_head, o2[:qb], o2[qb:]))
        lses.append(jnp.where(first_head, lse2[:qb], lse2[qb:]))

    lse_max = functools.reduce(jnp.maximum, lses)
    ws = [jnp.exp(x - lse_max) for x in lses]
    den = functools.reduce(lambda a, b: a + b, ws)
    num = functools.reduce(lambda a, b: a + b, [w * o for w, o in zip(ws, outs)])
    o_ref[0] = num / den


def dilated_attention_pallas(q, k, v):
    b, s, w = q.shape
    qb = Q_BLOCK
    return pl.pallas_call(
        functools.partial(_dilated_kernel, qb=qb),
        grid=(b, s // qb),
        in_specs=[pl.BlockSpec((1, qb, w), lambda i, j: (i, j, 0)),
                  pl.BlockSpec((1, s, w), lambda i, j: (i, 0, 0)),
                  pl.BlockSpec((1, s, w), lambda i, j: (i, 0, 0))],
        out_specs=pl.BlockSpec((1, qb, DIL_OUT_W), lambda i, j: (i, j, 0)),
        out_shape=jax.ShapeDtypeStruct((b, s, DIL_OUT_W), jnp.float32),
        compiler_params=pltpu.CompilerParams(
            dimension_semantics=("arbitrary", "arbitrary"),
            vmem_limit_bytes=48 << 20),
        name="dilated_attention",
    )(q, k, v)


FOX_KV_TILE = 512


def _fox_kernel(q_ref, k_ref, v_ref, cq_ref, ck_ref, o_ref, m_s, l_s, acc_s, la_s, lb_s, *, qb):
    j = pl.program_id(1)
    s0 = j * qb
    kt = FOX_KV_TILE
    bf16 = jnp.bfloat16
    scale = HEAD_DIM ** -0.5
    q_pos = s0 + lax.broadcasted_iota(jnp.int32, (qb, 1), 0)
    lane = lax.broadcasted_iota(jnp.int32, (qb, LANES), 1)
    first_head = lane < HEAD_DIM
    n_pairs = FOX_HEADS // 2
    rows = FOX_HEADS * qb
    n_tiles = (s0 + qb - 1) // kt + 1

    q2 = []
    for pair in range(n_pairs):
        q = q_ref[0, :, pl.ds(pair * LANES, LANES)] * scale
        q2.append(jnp.concatenate([jnp.where(first_head, q, 0.0),
                                   jnp.where(first_head, 0.0, q)], axis=0).astype(bf16))
    cq = [cq_ref[0, :, h:h + 1] for h in range(FOX_HEADS)]
    m_s[...] = jnp.full((rows, 1), NEG_INF, jnp.float32)
    l_s[...] = jnp.zeros((rows, 1), jnp.float32)
    acc_s[...] = jnp.zeros((rows, LANES), jnp.float32)

    def tile_start(t):
        return pl.multiple_of(jnp.minimum(t, n_tiles - 1) * kt, kt)

    def biased_logits(t):
        k0 = tile_start(t)
        logits = jnp.concatenate(
            [_nt_dot(q2[pair], k_ref[0, pl.ds(k0, kt), pl.ds(pair * LANES, LANES)])
             for pair in range(n_pairs)], axis=0)
        bias = jnp.concatenate([cq[h] - ck_ref[0, h:h + 1, pl.ds(k0, kt)] for h in range(FOX_HEADS)],
                               axis=0)
        return logits + bias

    def consume(logits, t):
        k0 = tile_start(t)
        mask = t * kt + lax.broadcasted_iota(jnp.int32, (qb, kt), 1) <= q_pos
        lg = jnp.where(mask[None], logits.reshape(FOX_HEADS, qb, kt), NEG_INF)
        m_old = m_s[...].reshape(FOX_HEADS, qb, 1)
        m_new = jnp.maximum(m_old, jnp.max(lg, axis=-1, keepdims=True))
        p = jnp.exp(lg - m_new)
        alpha = jnp.exp(m_old - m_new)
        l_s[...] = (alpha * l_s[...].reshape(FOX_HEADS, qb, 1)
                    + jnp.sum(p, axis=-1, keepdims=True)).reshape(rows, 1)
        pb = p.reshape(rows, kt).astype(bf16)
        pv = jnp.concatenate(
            [jnp.dot(pb[2 * pair * qb:2 * (pair + 1) * qb],
                     v_ref[0, pl.ds(k0, kt), pl.ds(pair * LANES, LANES)],
                     preferred_element_type=jnp.float32) for pair in range(n_pairs)], axis=0)
        acc_s[...] = alpha.reshape(rows, 1) * acc_s[...] + pv
        m_s[...] = m_new.reshape(rows, 1)

    la_s[...] = biased_logits(0)

    def body(u, c):
        t = 2 * u
        lb_s[...] = biased_logits(t + 1)
        consume(la_s[...], t)
        la_s[...] = biased_logits(t + 2)
        consume(lb_s[...], t + 1)
        return c
    lax.fori_loop(0, (n_tiles + 1) // 2, body, 0)

    o2 = acc_s[...] / l_s[...]
    for pair in range(n_pairs):
        r0 = 2 * pair * qb
        o_ref[0, :, pl.ds(pair * LANES, LANES)] = jnp.where(first_head, o2[r0:r0 + qb],
                                                             o2[r0 + qb:r0 + 2 * qb])


def fox_attention_pallas(q, k, v, cum):
    b, s, w = q.shape
    qb = Q_BLOCK
    return pl.pallas_call(
        functools.partial(_fox_kernel, qb=qb),
        grid=(b, s // qb),
        in_specs=[pl.BlockSpec((1, qb, w), lambda i, j: (i, j, 0)),
                  pl.BlockSpec((1, s, w), lambda i, j: (i, 0, 0)),
                  pl.BlockSpec((1, s, w), lambda i, j: (i, 0, 0)),
                  pl.BlockSpec((1, qb, FOX_HEADS), lambda i, j: (i, j, 0)),
                  pl.BlockSpec((1, FOX_HEADS, s), lambda i, j: (i, 0, 0))],
        out_specs=pl.BlockSpec((1, qb, w), lambda i, j: (i, j, 0)),
        out_shape=jax.ShapeDtypeStruct((b, s, w), jnp.float32),
        scratch_shapes=[pltpu.VMEM((FOX_HEADS * qb, 1), jnp.float32),
                        pltpu.VMEM((FOX_HEADS * qb, 1), jnp.float32),
                        pltpu.VMEM((FOX_HEADS * qb, LANES), jnp.float32),
                        pltpu.VMEM((FOX_HEADS * qb, FOX_KV_TILE), jnp.float32),
                        pltpu.VMEM((FOX_HEADS * qb, FOX_KV_TILE), jnp.float32)],
        compiler_params=pltpu.CompilerParams(
            dimension_semantics=("arbitrary", "arbitrary"),
            vmem_limit_bytes=40 << 20),
        name="fox_attention",
    )(q, k, v, cum, jnp.transpose(cum, (0, 2, 1)))


LRU_TILE = 512


def _lru_kernel(x_ref, gate_ref, cw_ref, cb_ref, wa_ref, ba_ref, wx_ref, bx_ref, lam_ref, o_ref,
                xbuf, a_s, u_s, h_s, hlast, *, tt):
    j = pl.program_id(1)
    c = x_ref.shape[-1]
    bf16 = jnp.bfloat16

    @pl.when(j == 0)
    def _():
        xbuf[pl.ds(0, SUBLANES), :] = jnp.zeros((SUBLANES, c), jnp.float32)
        hlast[...] = jnp.zeros((1, c), jnp.float32)

    x = x_ref[0]
    xbuf[pl.ds(SUBLANES, tt), :] = x
    xc = cb_ref[...] + cw_ref[LRU_CONV - 1:LRU_CONV, :] * x
    for back in range(1, LRU_CONV):
        xc = xc + cw_ref[LRU_CONV - 1 - back:LRU_CONV - back, :] * xbuf[pl.ds(SUBLANES - back, tt), :]
    xbuf[pl.ds(0, SUBLANES), :] = x[tt - SUBLANES:, :]

    xb = xc.astype(bf16)
    r = jax.nn.sigmoid(jnp.dot(xb, wa_ref[...], preferred_element_type=jnp.float32) + ba_ref[...])
    i_g = jax.nn.sigmoid(jnp.dot(xb, wx_ref[...], preferred_element_type=jnp.float32) + bx_ref[...])
    lam = lam_ref[...]
    softplus_neg = jnp.maximum(-lam, 0.0) + jnp.log1p(jnp.exp(-jnp.abs(lam)))
    log_a = -LRU_C * r * softplus_neg
    a_s[...] = jnp.exp(log_a)
    u_s[...] = jnp.sqrt(1.0 - jnp.exp(2.0 * log_a)) * (i_g * xc)

    def group(gi, h):
        r0 = pl.multiple_of(gi * SUBLANES, SUBLANES)
        a8 = a_s[pl.ds(r0, SUBLANES), :]
        u8 = u_s[pl.ds(r0, SUBLANES), :]
        rows = []
        for s in range(SUBLANES):
            h = a8[s:s + 1, :] * h + u8[s:s + 1, :]
            rows.append(h)
        h_s[pl.ds(r0, SUBLANES), :] = jnp.concatenate(rows, axis=0)
        return h
    hlast[...] = lax.fori_loop(0, tt // SUBLANES, group, hlast[...])
    o_ref[0] = h_s[...] * jax.nn.gelu(gate_ref[0])


def _block_diag(w):
    n, c, d = w.shape
    eye = jnp.eye(n, dtype=w.dtype)
    return (eye[:, None, :, None] * w[:, :, None, :]).reshape(n * c, n * d)


def rg_lru_pallas(x_in, gate, conv_w, conv_b, w_a, b_a, w_x, b_x, lam):
    b, s, c = x_in.shape
    tt = LRU_TILE
    row = lambda v: v.reshape(1, c)
    tok_spec = pl.BlockSpec((1, tt, c), lambda i, j: (i, j, 0))
    full = lambda shape: pl.BlockSpec(shape, lambda i, j: (0,) * len(shape))
    return pl.pallas_call(
        functools.partial(_lru_kernel, tt=tt),
        grid=(b, s // tt),
        in_specs=[tok_spec, tok_spec, full((LRU_CONV, c)), full((1, c)), full((c, c)), full((1, c)),
                  full((c, c)), full((1, c)), full((1, c))],
        out_specs=tok_spec,
        out_shape=jax.ShapeDtypeStruct((b, s, c), jnp.float32),
        scratch_shapes=[pltpu.VMEM((tt + SUBLANES, c), jnp.float32),
                        pltpu.VMEM((tt, c), jnp.float32),
                        pltpu.VMEM((tt, c), jnp.float32),
                        pltpu.VMEM((tt, c), jnp.float32),
                        pltpu.VMEM((1, c), jnp.float32)],
        compiler_params=pltpu.CompilerParams(dimension_semantics=("arbitrary", "arbitrary")),
        name="rg_lru",
    )(x_in, gate, conv_w, row(conv_b), _block_diag(w_a).astype(jnp.bfloat16), row(b_a),
      _block_diag(w_x).astype(jnp.bfloat16), row(b_x), row(lam))


DENSE_ROWS = 512


def _norm_matmul_kernel(x_ref, g_ref, w_ref, o_ref, xn_s):
    @pl.when(pl.program_id(1) == 0)
    def _():
        x = x_ref[...]
        y = x * lax.rsqrt(jnp.mean(x * x, axis=-1, keepdims=True) + NORM_EPS) * g_ref[...]
        xn_s[...] = y.astype(jnp.bfloat16)
    o_ref[...] = jnp.dot(xn_s[...], w_ref[...], preferred_element_type=jnp.float32)


def norm_matmul(x, g, w, tn):
    t, d = x.shape
    n = w.shape[1]
    tm = DENSE_ROWS
    return pl.pallas_call(
        _norm_matmul_kernel,
        grid=(t // tm, n // tn),
        in_specs=[pl.BlockSpec((tm, d), lambda i, j: (i, 0)),
                  pl.BlockSpec((1, d), lambda i, j: (0, 0)),
                  pl.BlockSpec((d, tn), lambda i, j: (0, j))],
        out_specs=pl.BlockSpec((tm, tn), lambda i, j: (i, j)),
        out_shape=jax.ShapeDtypeStruct((t, n), jnp.float32),
        scratch_shapes=[pltpu.VMEM((tm, d), jnp.bfloat16)],
        compiler_params=pltpu.CompilerParams(dimension_semantics=("arbitrary", "arbitrary"),
                                             vmem_limit_bytes=40 << 20),
        name="norm_matmul",
    )(x, g.reshape(1, d), w)


def _merge_out_kernel(x_ref, mg_ref, of_ref, ol_ref, od_ref, on_ref, wf_ref, wl_ref, wd_ref, wn_ref,
                      wo_ref, g2_ref, y_ref, h2_ref, *, d):
    bf16 = jnp.bfloat16
    merged = None
    for b, (o_ref, w_ref) in enumerate(((of_ref, wf_ref), (ol_ref, wl_ref), (od_ref, wd_ref),
                                        (on_ref, wn_ref))):
        y = jnp.dot(o_ref[...].astype(bf16), w_ref[...], preferred_element_type=jnp.float32)
        term = jax.nn.sigmoid(mg_ref[:, pl.ds(b * d, d)]) * y
        merged = term if merged is None else merged + term
    y = x_ref[...] + jnp.dot(merged.astype(bf16), wo_ref[...], preferred_element_type=jnp.float32)
    y_ref[...] = y
    h2_ref[...] = y * lax.rsqrt(jnp.mean(y * y, axis=-1, keepdims=True) + NORM_EPS) * g2_ref[...]


def merge_out(x, mg, o_fox, o_lru, o_dil, o_nsa, w_fox, w_lru, w_dil, w_nsa, w_out, g2):
    t, d = x.shape
    tm = DENSE_ROWS // 2
    bf16 = jnp.bfloat16
    tok = lambda a: pl.BlockSpec((tm, a.shape[1]), lambda i: (i, 0))
    full = lambda a: pl.BlockSpec(a.shape, lambda i: (0, 0))
    ws = [w.astype(bf16) for w in (w_fox, w_lru, w_dil, w_nsa, w_out)]
    g2r = g2.reshape(1, d)
    acts = (x, mg, o_fox, o_lru, o_dil, o_nsa)
    return pl.pallas_call(
        functools.partial(_merge_out_kernel, d=d),
        grid=(t // tm,),
        in_specs=[tok(a) for a in acts] + [full(w) for w in ws] + [full(g2r)],
        out_specs=[pl.BlockSpec((tm, d), lambda i: (i, 0)), pl.BlockSpec((tm, d), lambda i: (i, 0))],
        out_shape=[jax.ShapeDtypeStruct((t, d), jnp.float32), jax.ShapeDtypeStruct((t, d), jnp.float32)],
        compiler_params=pltpu.CompilerParams(dimension_semantics=("arbitrary",),
                                             vmem_limit_bytes=48 << 20),
        name="merge_out",
    )(*acts, *ws, g2r)


def rope(x, pos):
    half = x.shape[-1] // 2
    freqs = ROPE_THETA ** (-jnp.arange(half, dtype=jnp.float32) / half)
    ang = pos.astype(jnp.float32)[:, None] * freqs[None, :]
    cos = jnp.cos(ang)[None, :, None, :]
    sin = jnp.sin(ang)[None, :, None, :]
    xf = x.astype(jnp.float32)
    x1, x2 = xf[..., :half], xf[..., half:]
    return jnp.concatenate([x1 * cos - x2 * sin, x2 * cos + x1 * sin], axis=-1).astype(x.dtype)


def nsa_branch(nq, nkc, nvc, nks, nvs, nkw, nvw, ng, pe_k, pe_v, w1_k, w2_k, w1_v, w2_v):
    B, S, _ = nq.shape
    dh = HEAD_DIM
    bf16 = jnp.bfloat16
    pos = jnp.arange(S)
    q4 = nq.reshape(B, S, NSA_HEADS, dh)
    qn = jnp.transpose(q4, (0, 2, 1, 3))
    qr = jnp.transpose(rope(q4, pos), (0, 2, 1, 3))
    ks = rope(nks.reshape(B, S, 1, dh), pos).reshape(B, S, dh).astype(bf16)
    kw = rope(nkw.reshape(B, S, 1, dh), pos).reshape(B, S, dh).astype(bf16)

    n_cmp = (S - NSA_CMP_BLOCK) // NSA_CMP_STRIDE + 1
    n_cmp_pad = S // NSA_CMP_STRIDE
    n_slc = S // NSA_SLC_BLOCK

    def compress(kv, pe, w1, w2):
        chunks = kv.reshape(B, n_cmp_pad, NSA_CMP_STRIDE * dh)
        flat = jnp.concatenate([chunks[:, :-1], chunks[:, 1:]], axis=-1) + pe.reshape(-1)
        out = jax.nn.gelu(flat @ w1) @ w2
        return jnp.pad(out, ((0, 0), (0, n_cmp_pad - n_cmp), (0, 0))).astype(bf16)

    kc = compress(nkc, pe_k, w1_k, w2_k)
    vc = compress(nvc, pe_v, w1_v, w2_v)
    cmp_start = jnp.arange(n_cmp_pad) * NSA_CMP_STRIDE
    slc_start = jnp.arange(n_slc) * NSA_SLC_BLOCK
    overlap = ((cmp_start[:, None] < slc_start[None, :] + NSA_SLC_BLOCK)
               & (cmp_start[:, None] + NSA_CMP_BLOCK > slc_start[None, :])
               & (jnp.arange(n_cmp_pad)[:, None] < n_cmp)).astype(bf16)
    out = nsa_attention_pallas(qn, qr, kc, vc, overlap, ks, nvs.astype(bf16), kw,
                               nvw.astype(bf16), ng)
    return jnp.transpose(out, (0, 2, 1, 3)).reshape(B, S, NSA_W)


def peer_ffn(x_res, h2, w_q, sub_keys, u, v):
    expert, gt = peer_route(h2, w_q, sub_keys)
    uv = jnp.concatenate([u, v], axis=1)
    return peer_mix(h2, x_res, expert, gt, uv)


def kernel(x, norm1_g, w_in, fox_b_f, lru_conv_w, lru_conv_b, lru_w_a, lru_b_a, lru_w_x, lru_b_x, lru_lambda, nsa_pe_k, nsa_pe_v, nsa_w1_k, nsa_w2_k, nsa_w1_v, nsa_w2_v, w_br_fox, w_br_lru, w_br_dil, w_br_nsa, w_out, norm2_g, peer_w_q, peer_sub_keys, peer_u, peer_v, final_g):
    B, S, D = x.shape
    T = B * S
    pos = jnp.arange(S)
    n_mix = sum(IN_SPLITS[:-1])
    mix_tile = 640
    n_mix_pad = -(-n_mix // mix_tile) * mix_tile
    split_points = np.cumsum(IN_SPLITS[:-1])[:-1].tolist()
    bf16 = jnp.bfloat16

    def heads(t, n):
        return t.reshape(B, S, n, HEAD_DIM)

    x = x.reshape(T, D)
    for l in range(DEPTH):
        w_mix = jnp.pad(w_in[l][:, :n_mix], ((0, 0), (0, n_mix_pad - n_mix))).astype(bf16)
        proj = norm_matmul(x, norm1_g[l], w_mix, mix_tile)[:, :n_mix].reshape(B, S, n_mix)
        mg = norm_matmul(x, norm1_g[l], w_in[l][:, n_mix:].astype(bf16), D)
        (fq, fk, fv, ff, lx, lg, dq, dk, dv, nq, nkc, nvc, nks, nvs, nkw, nvw, ng) = \
            jnp.split(proj, split_points, axis=-1)

        cum = jnp.cumsum(jax.nn.log_sigmoid(ff + fox_b_f[l]), axis=1)
        o_fox = fox_attention_pallas(fq, fk.astype(jnp.bfloat16), fv.astype(jnp.bfloat16), cum)
        o_lru = rg_lru_pallas(lx, lg, lru_conv_w[l], lru_conv_b[l], lru_w_a[l], lru_b_a[l],
                              lru_w_x[l], lru_b_x[l], lru_lambda[l])
        o_dil = dilated_attention_pallas(
            rope(heads(dq, DIL_HEADS), pos).reshape(B, S, DIL_W),
            rope(heads(dk, DIL_HEADS), pos).reshape(B, S, DIL_W).astype(jnp.bfloat16),
            dv.astype(jnp.bfloat16))
        o_nsa = nsa_branch(nq, nkc, nvc, nks, nvs, nkw, nvw, ng,
                           nsa_pe_k[l], nsa_pe_v[l], nsa_w1_k[l], nsa_w2_k[l],
                           nsa_w1_v[l], nsa_w2_v[l])

        x, h2 = merge_out(x, mg, o_fox.reshape(T, FOX_W), o_lru.reshape(T, LRU_WIDTH),
                          o_dil.reshape(T, DIL_OUT_W), o_nsa.reshape(T, NSA_W),
                          w_br_fox[l], w_br_lru[l], w_br_dil[l], w_br_nsa[l], w_out[l], norm2_g[l])
        x = peer_ffn(x, h2, peer_w_q[l], peer_sub_keys[l], peer_u[l], peer_v[l])

    return rms_norm_pallas(x, final_g).reshape(B, S, D)
```

```python
import functools

import jax
import jax.numpy as jnp
import numpy as np
from jax import lax
from jax.experimental import pallas as pl
from jax.experimental.pallas import tpu as pltpu

LANES = 128
SUBLANES = 8
V7X_VMEM_BYTES = 64 << 20
VMEM_LIMIT_BYTES = V7X_VMEM_BYTES * 3 // 4

D_MODEL = 1024
DEPTH = 2
HEAD_DIM = 64
Q_BLOCK = 128
ROPE_THETA = 10000.0
NORM_EPS = 1e-6
NEG_INF = -1e30
N_BRANCHES = 4
FOX_HEADS = 4
FOX_W = FOX_HEADS * HEAD_DIM
LRU_WIDTH = 256
LRU_CONV = 4
LRU_C = 8.0
DIL_PAIRS = ((128, 1), (512, 4), (2048, 16))
DIL_GROUPS = len(DIL_PAIRS)
DIL_HEADS_PER_GROUP = 2
DIL_HEADS = DIL_GROUPS * DIL_HEADS_PER_GROUP
DIL_W = DIL_HEADS * HEAD_DIM
DIL_OUT_W = DIL_HEADS_PER_GROUP * HEAD_DIM
NSA_HEADS = 4
NSA_KV_HEADS = 1
NSA_W = NSA_HEADS * HEAD_DIM
NSA_KV_W = NSA_KV_HEADS * HEAD_DIM
NSA_CMP_BLOCK = 32
NSA_CMP_STRIDE = 16
NSA_SLC_BLOCK = 64
NSA_TOP_N = 16
NSA_WINDOW = 512
NSA_FORCE_SCORE = 1e6
PEER_HEADS = 8
PEER_N_KEYS = 128
PEER_TOPK = 16
PEER_D_KEY = 256
PEER_BLOCK = 128

IN_SPLITS = (FOX_W, FOX_W, FOX_W, FOX_HEADS,
             LRU_WIDTH, LRU_WIDTH,
             DIL_W, DIL_W, DIL_W,
             NSA_W, NSA_KV_W, NSA_KV_W, NSA_KV_W, NSA_KV_W, NSA_KV_W, NSA_KV_W, NSA_HEADS * 3,
             N_BRANCHES * D_MODEL)


def _rms_norm_kernel(x_ref, g_ref, o_ref):
    x = x_ref[...]
    ms = jnp.mean(x * x, axis=-1, keepdims=True)
    o_ref[...] = x * lax.rsqrt(ms + NORM_EPS) * g_ref[...]


def rms_norm_pallas(x, g):
    t, d = x.shape
    rows = DENSE_ROWS
    return pl.pallas_call(
        _rms_norm_kernel,
        grid=(t // rows,),
        in_specs=[pl.BlockSpec((rows, d), lambda i: (i, 0)),
                  pl.BlockSpec((1, d), lambda i: (0, 0))],
        out_specs=pl.BlockSpec((rows, d), lambda i: (i, 0)),
        out_shape=jax.ShapeDtypeStruct((t, d), x.dtype),
        name="rms_norm",
    )(x, g.reshape(1, d))


def _gelu_tanh(x):
    return 0.5 * x * (1.0 + jnp.tanh(0.7978845608028654 * (x + 0.044715 * x * x * x)))


PEER_TOKENS_PER_STEP = 8


def _peer_mix_kernel(idx_hbm, x_ref, gt_ref, res_ref, uv_hbm, o_ref,
                     idx_smem, buf, idx_sem, row_sem, *, tb, n_sel, d):
    i = pl.program_id(0)
    n = pl.num_programs(0)
    rows = tb * n_sel
    n_chunks = d // LANES
    groups = n_sel // SUBLANES

    def idx_copy(step, s):
        return pltpu.make_async_copy(idx_hbm.at[step], idx_smem.at[pl.ds(s * rows, rows)],
                                     idx_sem.at[s])

    def row_copy(s, j, row):
        return pltpu.make_async_copy(uv_hbm.at[row], buf.at[s, j // SUBLANES, pl.ds(j % SUBLANES, 1)],
                                     row_sem.at[s])

    def slot_wait(s):
        pltpu.make_async_copy(buf.at[s], buf.at[s], row_sem.at[s]).wait()

    @pl.when(i == 0)
    def _():
        idx_copy(0, 0).start()
        idx_copy(0, 0).wait()

        def body(jj, c):
            for k in range(SUBLANES):
                row = idx_smem[jj * SUBLANES + k]
                pltpu.make_async_copy(uv_hbm.at[row], buf.at[0, jj, pl.ds(k, 1)],
                                      row_sem.at[0]).start(priority=k % 2)
            return c
        lax.fori_loop(0, rows // SUBLANES, body, 0)
        idx_copy(1, 1).start()

    def step(cur):
        nxt = 1 - cur
        idx_copy(i + 1, nxt).wait()

        @pl.when(i + 2 <= n)
        def _():
            idx_copy(i + 2, cur).start()

        slot_wait(cur)

        def issue(t, part):
            lo = part * groups // (2 * n_chunks)
            hi = (part + 1) * groups // (2 * n_chunks)
            for j in range(t * n_sel + lo * SUBLANES, t * n_sel + hi * SUBLANES):
                row_copy(nxt, j, idx_smem[nxt * rows + j]).start(priority=j % 2)

        def sel_rows(t, lane0):
            tile = buf[cur, pl.ds(t * groups, groups), :, pl.ds(lane0, LANES)]
            return tile.reshape(n_sel, LANES)

        for t in range(tb):
            acc = jnp.zeros((n_sel, LANES), jnp.float32)
            for c in range(n_chunks):
                issue(t, c)
                xc = x_ref[pl.ds(t, 1), pl.ds(c * LANES, LANES)]
                acc = acc + sel_rows(t, c * LANES) * xc
            s = jnp.sum(acc, axis=-1, keepdims=True)
            w = gt_ref[0, :, pl.ds(t, 1)] * _gelu_tanh(s)
            wb = jnp.broadcast_to(w, (n_sel, LANES))
            for c in range(n_chunks):
                issue(t, n_chunks + c)
                vc = sel_rows(t, d + c * LANES)
                oc = jnp.sum(wb * vc, axis=0, keepdims=True)
                o_ref[pl.ds(t, 1), pl.ds(c * LANES, LANES)] = (
                    res_ref[pl.ds(t, 1), pl.ds(c * LANES, LANES)] + oc)

        @pl.when(i == n - 1)
        def _():
            slot_wait(nxt)

    @pl.when(i % 2 == 0)
    def _():
        step(0)

    @pl.when(i % 2 == 1)
    def _():
        step(1)


def peer_mix(x, res, idx, gt_blocks, uv):
    t, d = x.shape
    n_sel = idx.shape[1]
    tb = PEER_TOKENS_PER_STEP
    steps = t // tb
    spare = (jnp.arange(tb * n_sel, dtype=jnp.int32) % uv.shape[0]).reshape(1, tb * n_sel)
    idx_steps = jnp.concatenate([idx.reshape(steps, tb * n_sel), spare], axis=0)
    nb, _, tq = gt_blocks.shape
    gt = jnp.transpose(gt_blocks.reshape(nb, n_sel, tq // tb, tb), (0, 2, 1, 3)).reshape(steps, n_sel, tb)
    kern = functools.partial(_peer_mix_kernel, tb=tb, n_sel=n_sel, d=d)
    return pl.pallas_call(
        kern,
        grid=(steps,),
        in_specs=[pl.BlockSpec(memory_space=pl.ANY),
                  pl.BlockSpec((tb, d), lambda i: (i, 0)),
                  pl.BlockSpec((1, n_sel, tb), lambda i: (i, 0, 0)),
                  pl.BlockSpec((tb, d), lambda i: (i, 0)),
                  pl.BlockSpec(memory_space=pl.ANY)],
        out_specs=pl.BlockSpec((tb, d), lambda i: (i, 0)),
        out_shape=jax.ShapeDtypeStruct((t, d), jnp.float32),
        scratch_shapes=[pltpu.SMEM((2 * tb * n_sel,), jnp.int32),
                        pltpu.VMEM((2, tb * n_sel // SUBLANES, SUBLANES, 2 * d), jnp.float32),
                        pltpu.SemaphoreType.DMA((2,)),
                        pltpu.SemaphoreType.DMA((2,))],
        compiler_params=pltpu.CompilerParams(
            dimension_semantics=("arbitrary",),
            vmem_limit_bytes=VMEM_LIMIT_BYTES),
        name="peer_mix",
    )(idx_steps, x, gt, res, uv.reshape(uv.shape[0], 1, 2 * d))


def _topk_rows(work, k):
    n = work.shape[0]
    row = lax.broadcasted_iota(jnp.int32, work.shape, 0)
    vals, idxs = [], []
    for _ in range(k):
        mx = jnp.max(work, axis=0, keepdims=True)
        pos = jnp.min(jnp.where(work == mx, row, n), axis=0, keepdims=True)
        work = jnp.where(row == pos, -jnp.inf, work)
        vals.append(mx)
        idxs.append(pos)
    return jnp.concatenate(vals, axis=0), jnp.concatenate(idxs, axis=0)


def _peer_route_kernel(x_ref, wq_ref, keys_ref, idx_ref, gt_ref, q_s, idx_s, g_s, *, tq):
    bf16 = jnp.bfloat16
    k = PEER_TOPK
    half = PEER_D_KEY // 2
    q = jnp.dot(x_ref[...].astype(bf16), wq_ref[...], preferred_element_type=jnp.float32)
    for c in range(2 * PEER_HEADS):
        q_s[c] = q[:, c * half:(c + 1) * half].astype(bf16)

    def head_body(h, carry):
        tops = []
        for p in range(2):
            scores = _nt_dot(keys_ref[p], q_s[2 * h + p])
            tops.append(_topk_rows(scores, k))
        (v0, i0), (v1, i1) = tops
        keep = [k // (a + 1) for a in range(k)]
        n_cand = sum(keep)
        n_pad = -n_cand % SUBLANES
        cand = jnp.concatenate([v0[a:a + 1, :] + v1[:keep[a]] for a in range(k)]
                               + [jnp.full((n_pad, tq), -jnp.inf, jnp.float32)], axis=0)
        cand_idx = jnp.concatenate([i0[a:a + 1, :] * PEER_N_KEYS + i1[:keep[a]] for a in range(k)]
                                   + [jnp.full((n_pad, tq), -1, jnp.int32)], axis=0)
        flat = lax.broadcasted_iota(jnp.int32, cand.shape, 0)
        top_s, experts = [], []
        for _ in range(k):
            mx = jnp.max(cand, axis=0, keepdims=True)
            pos = jnp.min(jnp.where(cand == mx, flat, n_cand + n_pad), axis=0, keepdims=True)
            pick = flat == pos
            experts.append(jnp.max(jnp.where(pick, cand_idx, -1), axis=0, keepdims=True))
            cand = jnp.where(pick, -jnp.inf, cand)
            top_s.append(mx)
        top_s = jnp.concatenate(top_s, axis=0)
        ex = jnp.exp(top_s - jnp.max(top_s, axis=0, keepdims=True))
        r0 = pl.multiple_of(h * k, k)
        g_s[pl.ds(r0, k), :] = ex / jnp.sum(ex, axis=0, keepdims=True)
        idx_s[pl.ds(r0, k), :] = jnp.concatenate(experts, axis=0)
        return carry
    lax.fori_loop(0, PEER_HEADS, head_body, 0)

    idx_ref[...] = jnp.transpose(idx_s[...])
    gt_ref[0] = g_s[...]


def peer_route(h2, w_q, sub_keys):
    t, d = h2.shape
    tq = PEER_BLOCK
    n_sel = PEER_HEADS * PEER_TOPK
    half = PEER_D_KEY // 2
    return pl.pallas_call(
        functools.partial(_peer_route_kernel, tq=tq),
        grid=(t // tq,),
        in_specs=[pl.BlockSpec((tq, d), lambda i: (i, 0)),
                  pl.BlockSpec((d, PEER_HEADS * PEER_D_KEY), lambda i: (0, 0)),
                  pl.BlockSpec((2, PEER_N_KEYS, half), lambda i: (0, 0, 0))],
        out_specs=[pl.BlockSpec((tq, n_sel), lambda i: (i, 0)),
                   pl.BlockSpec((1, n_sel, tq), lambda i: (i, 0, 0))],
        out_shape=[jax.ShapeDtypeStruct((t, n_sel), jnp.int32),
                   jax.ShapeDtypeStruct((t // tq, n_sel, tq), jnp.float32)],
        scratch_shapes=[pltpu.VMEM((2 * PEER_HEADS, tq, half), jnp.bfloat16),
                        pltpu.VMEM((n_sel, tq), jnp.int32),
                        pltpu.VMEM((n_sel, tq), jnp.float32)],
        compiler_params=pltpu.CompilerParams(dimension_semantics=("arbitrary",),
                                             vmem_limit_bytes=VMEM_LIMIT_BYTES),
        name="peer_route",
    )(h2, w_q.astype(jnp.bfloat16), sub_keys.astype(jnp.bfloat16))


NSA_KV_TILE = 512


def _nt_dot(a, b):
    return lax.dot_general(a, b, (((1,), (1,)), ((), ())), preferred_element_type=jnp.float32)


def _nsa_kernel(qn_ref, qr_ref, kc_ref, vc_ref, ov_ref, ks_ref, vs_ref, kw_ref, vw_ref, gl_ref,
                o_ref, m_s, l_s, acc_s, out_s, la_s, lb_s, *, n_heads, qb, n_cmp_pad, n_slc):
    j = pl.program_id(1)
    s0 = j * qb
    hq = n_heads * qb
    bf16 = jnp.bfloat16
    scale = HEAD_DIM ** -0.5
    q_pos = s0 + lax.broadcasted_iota(jnp.int32, (qb, 1), 0)

    gates = jax.nn.sigmoid(gl_ref[0])

    def gate_col(br):
        return jnp.concatenate([gates[:, 3 * h + br:3 * h + br + 1] for h in range(n_heads)], axis=0)

    def masked_update(logits, mask, v_tile):
        kt = logits.shape[-1]
        lg = jnp.where(mask[None], logits.reshape(n_heads, qb, kt), NEG_INF)
        m_old = m_s[...].reshape(n_heads, qb, 1)
        m_new = jnp.maximum(m_old, jnp.max(lg, axis=-1, keepdims=True))
        p = jnp.where(mask[None], jnp.exp(lg - m_new), 0.0)
        alpha = jnp.exp(m_old - m_new)
        l_s[...] = (alpha * l_s[...].reshape(n_heads, qb, 1)
                    + jnp.sum(p, axis=-1, keepdims=True)).reshape(hq, 1)
        pv = jnp.dot(p.reshape(hq, kt).astype(bf16), v_tile, preferred_element_type=jnp.float32)
        acc_s[...] = alpha.reshape(hq, 1) * acc_s[...] + pv
        m_s[...] = m_new.reshape(hq, 1)

    def reset_state():
        m_s[...] = jnp.full((hq, 1), NEG_INF, jnp.float32)
        l_s[...] = jnp.zeros((hq, 1), jnp.float32)
        acc_s[...] = jnp.zeros((hq, HEAD_DIM), jnp.float32)

    def normalized():
        return acc_s[...] / jnp.maximum(l_s[...], 1e-30)

    qn = (qn_ref[0].reshape(hq, HEAD_DIM) * scale).astype(bf16)
    lc = _nt_dot(qn, kc_ref[0]).reshape(n_heads, qb, n_cmp_pad)
    c_end = (lax.broadcasted_iota(jnp.int32, (qb, n_cmp_pad), 1) * NSA_CMP_STRIDE
             + (NSA_CMP_BLOCK - 1))
    c_mask = c_end <= q_pos
    lc = jnp.where(c_mask[None], lc, NEG_INF)
    mc = jnp.max(lc, axis=-1, keepdims=True)
    ec = jnp.where(c_mask[None], jnp.exp(lc - mc), 0.0)
    pc = ec / jnp.maximum(jnp.sum(ec, axis=-1, keepdims=True), 1e-30)
    o_c = jnp.dot(pc.reshape(hq, n_cmp_pad).astype(bf16), vc_ref[0],
                  preferred_element_type=jnp.float32)
    out_s[...] = gate_col(0) * o_c

    p_sum = jnp.sum(pc, axis=0)
    p_hi = p_sum.astype(bf16)
    p_lo = (p_sum - p_hi.astype(jnp.float32)).astype(bf16)
    imp = (jnp.dot(p_hi, ov_ref[...], preferred_element_type=jnp.float32)
           + jnp.dot(p_lo, ov_ref[...], preferred_element_type=jnp.float32))
    blk = lax.broadcasted_iota(jnp.int32, (qb, n_slc), 1)
    cur = q_pos // NSA_SLC_BLOCK
    forced = (blk == 0) | (blk == cur) | (blk == cur - 1)
    imp = jnp.where(forced, NSA_FORCE_SCORE, imp)
    valid = blk <= cur
    work = jnp.transpose(jnp.where(valid, imp, NEG_INF))
    blk_t = lax.broadcasted_iota(jnp.int32, (n_slc, qb), 0)
    sel_t = jnp.zeros((n_slc, qb), jnp.float32)
    for _ in range(min(NSA_TOP_N, n_slc)):
        mx = jnp.max(work, axis=0, keepdims=True)
        first = jnp.min(jnp.where(work == mx, blk_t, n_slc), axis=0, keepdims=True)
        pick = blk_t == first
        sel_t = jnp.where(pick, 1.0, sel_t)
        work = jnp.where(pick, -jnp.inf, work)
    sel_bf = jnp.where(valid, jnp.transpose(sel_t), 0.0).astype(bf16)

    qr = (qr_ref[0].reshape(hq, HEAD_DIM) * scale).astype(bf16)
    reset_state()
    w_len = NSA_WINDOW + qb
    w0 = pl.multiple_of(jnp.maximum(s0 - NSA_WINDOW, 0), qb)
    dist = q_pos - (w0 + lax.broadcasted_iota(jnp.int32, (qb, w_len), 1))
    masked_update(_nt_dot(qr, kw_ref[0, pl.ds(w0, w_len), :]), (dist >= 0) & (dist < NSA_WINDOW),
                  vw_ref[0, pl.ds(w0, w_len), :])
    out_s[...] += gate_col(2) * normalized()

    reset_state()
    kt = NSA_KV_TILE
    blocks_per_tile = kt // NSA_SLC_BLOCK
    n_tiles = (s0 + qb - 1) // kt + 1

    def tile_start(t):
        return pl.multiple_of(jnp.minimum(t, n_tiles - 1) * kt, kt)

    def scores(t):
        return _nt_dot(qr, ks_ref[0, pl.ds(tile_start(t), kt), :])

    def consume(logits, t):
        k_idx = lax.broadcasted_iota(jnp.int32, (n_slc, kt), 1)
        expand = (lax.broadcasted_iota(jnp.int32, (n_slc, kt), 0)
                  == t * blocks_per_tile + k_idx // NSA_SLC_BLOCK)
        picked = jnp.dot(sel_bf, jnp.where(expand, 1.0, 0.0).astype(bf16),
                         preferred_element_type=jnp.float32)
        k_pos = t * kt + lax.broadcasted_iota(jnp.int32, (qb, kt), 1)
        mask = (picked > 0.5) & (k_pos <= q_pos)
        masked_update(logits, mask, vs_ref[0, pl.ds(tile_start(t), kt), :])

    la_s[...] = scores(0)

    def sel_body(u, c):
        t = 2 * u
        lb_s[...] = scores(t + 1)
        consume(la_s[...], t)
        la_s[...] = scores(t + 2)
        consume(lb_s[...], t + 1)
        return c
    lax.fori_loop(0, (n_tiles + 1) // 2, sel_body, 0)
    out_s[...] += gate_col(1) * normalized()

    o_ref[0] = out_s[...].reshape(n_heads, qb, HEAD_DIM)


def _nsa_compress_kernel(x_ref, pe_ref, w1_ref, w2_ref, o_ref, *, n_cmp):
    bf16 = jnp.bfloat16
    x = x_ref[0]
    n_chunks, half = x.shape
    head = jnp.dot((x + pe_ref[0:1, :]).astype(bf16), w1_ref[pl.ds(0, half), :],
                   preferred_element_type=jnp.float32)
    tail = jnp.dot((x + pe_ref[1:2, :]).astype(bf16), w1_ref[pl.ds(half, half), :],
                   preferred_element_type=jnp.float32)
    hidden = jax.nn.gelu(head + pltpu.roll(tail, n_chunks - 1, 0))
    out = jnp.dot(hidden.astype(bf16), w2_ref[...], preferred_element_type=jnp.float32)
    row = lax.broadcasted_iota(jnp.int32, out.shape, 0)
    o_ref[0] = jnp.where(row < n_cmp, out, 0.0).astype(o_ref.dtype)


def nsa_compress(kv, pe, w1, w2):
    b, s, dh = kv.shape
    n_chunks = s // NSA_CMP_STRIDE
    n_cmp = (s - NSA_CMP_BLOCK) // NSA_CMP_STRIDE + 1
    half = NSA_CMP_STRIDE * dh
    hidden = w1.shape[1]
    return pl.pallas_call(
        functools.partial(_nsa_compress_kernel, n_cmp=n_cmp),
        grid=(b,),
        in_specs=[pl.BlockSpec((1, n_chunks, half), lambda i: (i, 0, 0)),
                  pl.BlockSpec((2, half), lambda i: (0, 0)),
                  pl.BlockSpec((2 * half, hidden), lambda i: (0, 0)),
                  pl.BlockSpec((hidden, dh), lambda i: (0, 0))],
        out_specs=pl.BlockSpec((1, n_chunks, dh), lambda i: (i, 0, 0)),
        out_shape=jax.ShapeDtypeStruct((b, n_chunks, dh), jnp.bfloat16),
        compiler_params=pltpu.CompilerParams(dimension_semantics=("arbitrary",)),
        name="nsa_compress",
    )(kv.reshape(b, n_chunks, half), pe.reshape(2, half), w1.astype(jnp.bfloat16),
      w2.astype(jnp.bfloat16))


def nsa_attention_pallas(qn, qr, kc, vc, overlap, ks, vs, kw, vw, gl):
    b, h, s, dh = qn.shape
    qb = Q_BLOCK
    n_cmp_pad = kc.shape[1]
    n_slc = overlap.shape[1]
    kern = functools.partial(_nsa_kernel, n_heads=h, qb=qb, n_cmp_pad=n_cmp_pad, n_slc=n_slc)
    q_spec = pl.BlockSpec((1, h, qb, dh), lambda i, j: (i, 0, j, 0))
    cmp_spec = pl.BlockSpec((1, n_cmp_pad, dh), lambda i, j: (i, 0, 0))
    kv_spec = pl.BlockSpec((1, s, dh), lambda i, j: (i, 0, 0))
    return pl.pallas_call(
        kern,
        grid=(b, s // qb),
        in_specs=[q_spec, q_spec, cmp_spec, cmp_spec,
                  pl.BlockSpec((n_cmp_pad, n_slc), lambda i, j: (0, 0)),
                  kv_spec, kv_spec, kv_spec, kv_spec,
                  pl.BlockSpec((1, qb, 3 * h), lambda i, j: (i, j, 0))],
        out_specs=q_spec,
        out_shape=jax.ShapeDtypeStruct((b, h, s, dh), jnp.float32),
        scratch_shapes=[pltpu.VMEM((h * qb, 1), jnp.float32),
                        pltpu.VMEM((h * qb, 1), jnp.float32),
                        pltpu.VMEM((h * qb, dh), jnp.float32),
                        pltpu.VMEM((h * qb, dh), jnp.float32),
                        pltpu.VMEM((h * qb, NSA_KV_TILE), jnp.float32),
                        pltpu.VMEM((h * qb, NSA_KV_TILE), jnp.float32)],
        compiler_params=pltpu.CompilerParams(
            dimension_semantics=("arbitrary", "arbitrary"),
            vmem_limit_bytes=VMEM_LIMIT_BYTES),
        name="nsa_attention",
    )(qn, qr, kc, vc, overlap, ks, vs, kw, vw, gl)


def _dilated_kernel(q_ref, k_ref, v_ref, o_ref, *, qb):
    j = pl.program_id(1)
    s0 = j * qb
    bf16 = jnp.bfloat16
    scale = HEAD_DIM ** -0.5
    q_pos = s0 + lax.broadcasted_iota(jnp.int32, (qb, 1), 0)
    lane = lax.broadcasted_iota(jnp.int32, (qb, LANES), 1)
    first_head = lane < HEAD_DIM
    rows = DIL_HEADS_PER_GROUP * qb

    outs, lses = [], []
    for g, (window, dil) in enumerate(DIL_PAIRS):
        c0 = g * LANES
        q = q_ref[0, :, pl.ds(c0, LANES)] * scale
        q2 = jnp.concatenate([jnp.where(first_head, q, 0.0),
                              jnp.where(first_head, 0.0, q)], axis=0).astype(bf16)
        span = window + qb
        k0 = pl.multiple_of(jnp.maximum(s0 - window, 0), qb)
        dist = q_pos - (k0 + lax.broadcasted_iota(jnp.int32, (qb, span), 1))
        mask = (dist >= 0) & (dist <= window) & ((dist & (dil - 1)) == 0)
        logits = _nt_dot(q2, k_ref[0, pl.ds(k0, span), pl.ds(c0, LANES)])
        lg = jnp.where(mask[None], logits.reshape(DIL_HEADS_PER_GROUP, qb, span), NEG_INF)
        m = jnp.max(lg, axis=-1, keepdims=True)
        e = jnp.exp(lg - m)
        l = jnp.sum(e, axis=-1, keepdims=True).reshape(rows, 1)
        pv = jnp.dot(e.reshape(rows, span).astype(bf16), v_ref[0, pl.ds(k0, span), pl.ds(c0, LANES)],
                     preferred_element_type=jnp.float32)
        o2 = pv / l
        lse2 = m.reshape(rows, 1) + jnp.log(l)
        outs.append(jnp.where(first_head, o2[:qb], o2[qb:]))
        lses.append(jnp.where(first_head, lse2[:qb], lse2[qb:]))

    lse_max = functools.reduce(jnp.maximum, lses)
    ws = [jnp.exp(x - lse_max) for x in lses]
    den = functools.reduce(lambda a, b: a + b, ws)
    num = functools.reduce(lambda a, b: a + b, [w * o for w, o in zip(ws, outs)])
    o_ref[0] = num / den


def dilated_attention_pallas(q, k, v):
    b, s, w = q.shape
    qb = Q_BLOCK
    return pl.pallas_call(
        functools.partial(_dilated_kernel, qb=qb),
        grid=(b, s // qb),
        in_specs=[pl.BlockSpec((1, qb, w), lambda i, j: (i, j, 0)),
                  pl.BlockSpec((1, s, w), lambda i, j: (i, 0, 0)),
                  pl.BlockSpec((1, s, w), lambda i, j: (i, 0, 0))],
        out_specs=pl.BlockSpec((1, qb, DIL_OUT_W), lambda i, j: (i, j, 0)),
        out_shape=jax.ShapeDtypeStruct((b, s, DIL_OUT_W), jnp.float32),
        compiler_params=pltpu.CompilerParams(
            dimension_semantics=("arbitrary", "arbitrary"),
            vmem_limit_bytes=VMEM_LIMIT_BYTES),
        name="dilated_attention",
    )(q, k, v)


FOX_KV_TILE = 512


def _fox_kernel(q_ref, k_ref, v_ref, cq_ref, ck_ref, o_ref, m_s, l_s, acc_s, la_s, lb_s, *, qb):
    j = pl.program_id(1)
    s0 = j * qb
    kt = FOX_KV_TILE
    bf16 = jnp.bfloat16
    scale = HEAD_DIM ** -0.5
    q_pos = s0 + lax.broadcasted_iota(jnp.int32, (qb, 1), 0)
    lane = lax.broadcasted_iota(jnp.int32, (qb, LANES), 1)
    first_head = lane < HEAD_DIM
    n_pairs = FOX_HEADS // 2
    rows = FOX_HEADS * qb
    n_tiles = (s0 + qb - 1) // kt + 1

    q2 = []
    for pair in range(n_pairs):
        q = q_ref[0, :, pl.ds(pair * LANES, LANES)] * scale
        q2.append(jnp.concatenate([jnp.where(first_head, q, 0.0),
                                   jnp.where(first_head, 0.0, q)], axis=0).astype(bf16))
    cq = [cq_ref[0, :, h:h + 1] for h in range(FOX_HEADS)]
    m_s[...] = jnp.full((rows, 1), NEG_INF, jnp.float32)
    l_s[...] = jnp.zeros((rows, 1), jnp.float32)
    acc_s[...] = jnp.zeros((rows, LANES), jnp.float32)

    def tile_start(t):
        return pl.multiple_of(jnp.minimum(t, n_tiles - 1) * kt, kt)

    def biased_logits(t):
        k0 = tile_start(t)
        logits = jnp.concatenate(
            [_nt_dot(q2[pair], k_ref[0, pl.ds(k0, kt), pl.ds(pair * LANES, LANES)])
             for pair in range(n_pairs)], axis=0)
        bias = jnp.concatenate([cq[h] - ck_ref[0, h:h + 1, pl.ds(k0, kt)] for h in range(FOX_HEADS)],
                               axis=0)
        return logits + bias

    def consume(logits, t):
        k0 = tile_start(t)
        mask = t * kt + lax.broadcasted_iota(jnp.int32, (qb, kt), 1) <= q_pos
        lg = jnp.where(mask[None], logits.reshape(FOX_HEADS, qb, kt), NEG_INF)
        m_old = m_s[...].reshape(FOX_HEADS, qb, 1)
        m_new = jnp.maximum(m_old, jnp.max(lg, axis=-1, keepdims=True))
        p = jnp.exp(lg - m_new)
        alpha = jnp.exp(m_old - m_new)
        l_s[...] = (alpha * l_s[...].reshape(FOX_HEADS, qb, 1)
                    + jnp.sum(p, axis=-1, keepdims=True)).reshape(rows, 1)
        pb = p.reshape(rows, kt).astype(bf16)
        pv = jnp.concatenate(
            [jnp.dot(pb[2 * pair * qb:2 * (pair + 1) * qb],
                     v_ref[0, pl.ds(k0, kt), pl.ds(pair * LANES, LANES)],
                     preferred_element_type=jnp.float32) for pair in range(n_pairs)], axis=0)
        acc_s[...] = alpha.reshape(rows, 1) * acc_s[...] + pv
        m_s[...] = m_new.reshape(rows, 1)

    la_s[...] = biased_logits(0)

    def body(u, c):
        t = 2 * u
        lb_s[...] = biased_logits(t + 1)
        consume(la_s[...], t)
        la_s[...] = biased_logits(t + 2)
        consume(lb_s[...], t + 1)
        return c
    lax.fori_loop(0, (n_tiles + 1) // 2, body, 0)

    o2 = acc_s[...] / l_s[...]
    for pair in range(n_pairs):
        r0 = 2 * pair * qb
        o_ref[0, :, pl.ds(pair * LANES, LANES)] = jnp.where(first_head, o2[r0:r0 + qb],
                                                             o2[r0 + qb:r0 + 2 * qb])


def fox_attention_pallas(q, k, v, cum):
    b, s, w = q.shape
    qb = Q_BLOCK
    return pl.pallas_call(
        functools.partial(_fox_kernel, qb=qb),
        grid=(b, s // qb),
        in_specs=[pl.BlockSpec((1, qb, w), lambda i, j: (i, j, 0)),
                  pl.BlockSpec((1, s, w), lambda i, j: (i, 0, 0)),
                  pl.BlockSpec((1, s, w), lambda i, j: (i, 0, 0)),
                  pl.BlockSpec((1, qb, FOX_HEADS), lambda i, j: (i, j, 0)),
                  pl.BlockSpec((1, FOX_HEADS, s), lambda i, j: (i, 0, 0))],
        out_specs=pl.BlockSpec((1, qb, w), lambda i, j: (i, j, 0)),
        out_shape=jax.ShapeDtypeStruct((b, s, w), jnp.float32),
        scratch_shapes=[pltpu.VMEM((FOX_HEADS * qb, 1), jnp.float32),
                        pltpu.VMEM((FOX_HEADS * qb, 1), jnp.float32),
                        pltpu.VMEM((FOX_HEADS * qb, LANES), jnp.float32),
                        pltpu.VMEM((FOX_HEADS * qb, FOX_KV_TILE), jnp.float32),
                        pltpu.VMEM((FOX_HEADS * qb, FOX_KV_TILE), jnp.float32)],
        compiler_params=pltpu.CompilerParams(
            dimension_semantics=("arbitrary", "arbitrary"),
            vmem_limit_bytes=VMEM_LIMIT_BYTES),
        name="fox_attention",
    )(q, k, v, cum, jnp.transpose(cum, (0, 2, 1)))


LRU_TILE = 512


def _lru_kernel(x_ref, gate_ref, cw_ref, cb_ref, wa_ref, ba_ref, wx_ref, bx_ref, lam_ref, o_ref,
                xbuf, a_s, u_s, h_s, hlast, *, tt):
    j = pl.program_id(1)
    c = x_ref.shape[-1]
    bf16 = jnp.bfloat16

    @pl.when(j == 0)
    def _():
        xbuf[pl.ds(0, SUBLANES), :] = jnp.zeros((SUBLANES, c), jnp.float32)
        hlast[...] = jnp.zeros((1, c), jnp.float32)

    x = x_ref[0]
    xbuf[pl.ds(SUBLANES, tt), :] = x
    xc = cb_ref[...] + cw_ref[LRU_CONV - 1:LRU_CONV, :] * x
    for back in range(1, LRU_CONV):
        xc = xc + cw_ref[LRU_CONV - 1 - back:LRU_CONV - back, :] * xbuf[pl.ds(SUBLANES - back, tt), :]
    xbuf[pl.ds(0, SUBLANES), :] = x[tt - SUBLANES:, :]

    xb = xc.astype(bf16)
    r = jax.nn.sigmoid(jnp.dot(xb, wa_ref[...], preferred_element_type=jnp.float32) + ba_ref[...])
    i_g = jax.nn.sigmoid(jnp.dot(xb, wx_ref[...], preferred_element_type=jnp.float32) + bx_ref[...])
    lam = lam_ref[...]
    softplus_neg = jnp.maximum(-lam, 0.0) + jnp.log1p(jnp.exp(-jnp.abs(lam)))
    log_a = -LRU_C * r * softplus_neg
    a_s[...] = jnp.exp(log_a)
    u_s[...] = jnp.sqrt(1.0 - jnp.exp(2.0 * log_a)) * (i_g * xc)

    def group(gi, h):
        r0 = pl.multiple_of(gi * SUBLANES, SUBLANES)
        a8 = a_s[pl.ds(r0, SUBLANES), :]
        u8 = u_s[pl.ds(r0, SUBLANES), :]
        rows = []
        for s in range(SUBLANES):
            h = a8[s:s + 1, :] * h + u8[s:s + 1, :]
            rows.append(h)
        h_s[pl.ds(r0, SUBLANES), :] = jnp.concatenate(rows, axis=0)
        return h
    hlast[...] = lax.fori_loop(0, tt // SUBLANES, group, hlast[...])
    o_ref[0] = h_s[...] * jax.nn.gelu(gate_ref[0])


def _block_diag(w):
    n, c, d = w.shape
    eye = jnp.eye(n, dtype=w.dtype)
    return (eye[:, None, :, None] * w[:, :, None, :]).reshape(n * c, n * d)


def rg_lru_pallas(x_in, gate, conv_w, conv_b, w_a, b_a, w_x, b_x, lam):
    b, s, c = x_in.shape
    tt = LRU_TILE
    row = lambda v: v.reshape(1, c)
    tok_spec = pl.BlockSpec((1, tt, c), lambda i, j: (i, j, 0))
    full = lambda shape: pl.BlockSpec(shape, lambda i, j: (0,) * len(shape))
    return pl.pallas_call(
        functools.partial(_lru_kernel, tt=tt),
        grid=(b, s // tt),
        in_specs=[tok_spec, tok_spec, full((LRU_CONV, c)), full((1, c)), full((c, c)), full((1, c)),
                  full((c, c)), full((1, c)), full((1, c))],
        out_specs=tok_spec,
        out_shape=jax.ShapeDtypeStruct((b, s, c), jnp.float32),
        scratch_shapes=[pltpu.VMEM((tt + SUBLANES, c), jnp.float32),
                        pltpu.VMEM((tt, c), jnp.float32),
                        pltpu.VMEM((tt, c), jnp.float32),
                        pltpu.VMEM((tt, c), jnp.float32),
                        pltpu.VMEM((1, c), jnp.float32)],
        compiler_params=pltpu.CompilerParams(dimension_semantics=("arbitrary", "arbitrary")),
        name="rg_lru",
    )(x_in, gate, conv_w, row(conv_b), _block_diag(w_a).astype(jnp.bfloat16), row(b_a),
      _block_diag(w_x).astype(jnp.bfloat16), row(b_x), row(lam))


DENSE_ROWS = 512


def _norm_matmul_kernel(x_ref, g_ref, w_ref, o_ref, xn_s):
    @pl.when(pl.program_id(1) == 0)
    def _():
        x = x_ref[...]
        y = x * lax.rsqrt(jnp.mean(x * x, axis=-1, keepdims=True) + NORM_EPS) * g_ref[...]
        xn_s[...] = y.astype(jnp.bfloat16)
    o_ref[...] = jnp.dot(xn_s[...], w_ref[...], preferred_element_type=jnp.float32)


def norm_matmul(x, g, w, tn):
    t, d = x.shape
    n = w.shape[1]
    tm = DENSE_ROWS
    return pl.pallas_call(
        _norm_matmul_kernel,
        grid=(t // tm, n // tn),
        in_specs=[pl.BlockSpec((tm, d), lambda i, j: (i, 0)),
                  pl.BlockSpec((1, d), lambda i, j: (0, 0)),
                  pl.BlockSpec((d, tn), lambda i, j: (0, j))],
        out_specs=pl.BlockSpec((tm, tn), lambda i, j: (i, j)),
        out_shape=jax.ShapeDtypeStruct((t, n), jnp.float32),
        scratch_shapes=[pltpu.VMEM((tm, d), jnp.bfloat16)],
        compiler_params=pltpu.CompilerParams(dimension_semantics=("arbitrary", "arbitrary"),
                                             vmem_limit_bytes=VMEM_LIMIT_BYTES),
        name="norm_matmul",
    )(x, g.reshape(1, d), w)


def _merge_out_kernel(x_ref, mg_ref, of_ref, ol_ref, od_ref, on_ref, wf_ref, wl_ref, wd_ref, wn_ref,
                      wo_ref, g2_ref, y_ref, h2_ref, *, d):
    bf16 = jnp.bfloat16
    merged = None
    for b, (o_ref, w_ref) in enumerate(((of_ref, wf_ref), (ol_ref, wl_ref), (od_ref, wd_ref),
                                        (on_ref, wn_ref))):
        y = jnp.dot(o_ref[...].astype(bf16), w_ref[...], preferred_element_type=jnp.float32)
        term = jax.nn.sigmoid(mg_ref[:, pl.ds(b * d, d)]) * y
        merged = term if merged is None else merged + term
    y = x_ref[...] + jnp.dot(merged.astype(bf16), wo_ref[...], preferred_element_type=jnp.float32)
    y_ref[...] = y
    h2_ref[...] = y * lax.rsqrt(jnp.mean(y * y, axis=-1, keepdims=True) + NORM_EPS) * g2_ref[...]


def merge_out(x, mg, o_fox, o_lru, o_dil, o_nsa, w_fox, w_lru, w_dil, w_nsa, w_out, g2):
    t, d = x.shape
    tm = DENSE_ROWS // 2
    bf16 = jnp.bfloat16
    tok = lambda a: pl.BlockSpec((tm, a.shape[1]), lambda i: (i, 0))
    full = lambda a: pl.BlockSpec(a.shape, lambda i: (0, 0))
    ws = [w.astype(bf16) for w in (w_fox, w_lru, w_dil, w_nsa, w_out)]
    g2r = g2.reshape(1, d)
    acts = (x, mg, o_fox, o_lru, o_dil, o_nsa)
    return pl.pallas_call(
        functools.partial(_merge_out_kernel, d=d),
        grid=(t // tm,),
        in_specs=[tok(a) for a in acts] + [full(w) for w in ws] + [full(g2r)],
        out_specs=[pl.BlockSpec((tm, d), lambda i: (i, 0)), pl.BlockSpec((tm, d), lambda i: (i, 0))],
        out_shape=[jax.ShapeDtypeStruct((t, d), jnp.float32), jax.ShapeDtypeStruct((t, d), jnp.float32)],
        compiler_params=pltpu.CompilerParams(dimension_semantics=("arbitrary",),
                                             vmem_limit_bytes=VMEM_LIMIT_BYTES),
        name="merge_out",
    )(*acts, *ws, g2r)


def rope(x, pos):
    half = x.shape[-1] // 2
    freqs = ROPE_THETA ** (-jnp.arange(half, dtype=jnp.float32) / half)
    ang = pos.astype(jnp.float32)[:, None] * freqs[None, :]
    cos = jnp.cos(ang)[None, :, None, :]
    sin = jnp.sin(ang)[None, :, None, :]
    xf = x.astype(jnp.float32)
    x1, x2 = xf[..., :half], xf[..., half:]
    return jnp.concatenate([x1 * cos - x2 * sin, x2 * cos + x1 * sin], axis=-1).astype(x.dtype)


def nsa_branch(nq, nkc, nvc, nks, nvs, nkw, nvw, ng, pe_k, pe_v, w1_k, w2_k, w1_v, w2_v):
    B, S, _ = nq.shape
    dh = HEAD_DIM
    bf16 = jnp.bfloat16
    pos = jnp.arange(S)
    q4 = nq.reshape(B, S, NSA_HEADS, dh)
    qn = jnp.transpose(q4, (0, 2, 1, 3))
    qr = jnp.transpose(rope(q4, pos), (0, 2, 1, 3))
    ks = rope(nks.reshape(B, S, 1, dh), pos).reshape(B, S, dh).astype(bf16)
    kw = rope(nkw.reshape(B, S, 1, dh), pos).reshape(B, S, dh).astype(bf16)

    n_cmp = (S - NSA_CMP_BLOCK) // NSA_CMP_STRIDE + 1
    n_cmp_pad = S // NSA_CMP_STRIDE
    n_slc = S // NSA_SLC_BLOCK

    kc = nsa_compress(nkc, pe_k, w1_k, w2_k)
    vc = nsa_compress(nvc, pe_v, w1_v, w2_v)
    cmp_start = jnp.arange(n_cmp_pad) * NSA_CMP_STRIDE
    slc_start = jnp.arange(n_slc) * NSA_SLC_BLOCK
    overlap = ((cmp_start[:, None] < slc_start[None, :] + NSA_SLC_BLOCK)
               & (cmp_start[:, None] + NSA_CMP_BLOCK > slc_start[None, :])
               & (jnp.arange(n_cmp_pad)[:, None] < n_cmp)).astype(bf16)
    out = nsa_attention_pallas(qn, qr, kc, vc, overlap, ks, nvs.astype(bf16), kw,
                               nvw.astype(bf16), ng)
    return jnp.transpose(out, (0, 2, 1, 3)).reshape(B, S, NSA_W)


def peer_ffn(x_res, h2, w_q, sub_keys, u, v):
    expert, gt = peer_route(h2, w_q, sub_keys)
    uv = jnp.concatenate([u, v], axis=1)
    return peer_mix(h2, x_res, expert, gt, uv)


def kernel(x, norm1_g, w_in, fox_b_f, lru_conv_w, lru_conv_b, lru_w_a, lru_b_a, lru_w_x, lru_b_x, lru_lambda, nsa_pe_k, nsa_pe_v, nsa_w1_k, nsa_w2_k, nsa_w1_v, nsa_w2_v, w_br_fox, w_br_lru, w_br_dil, w_br_nsa, w_out, norm2_g, peer_w_q, peer_sub_keys, peer_u, peer_v, final_g):
    B, S, D = x.shape
    T = B * S
    pos = jnp.arange(S)
    n_mix = sum(IN_SPLITS[:-1])
    mix_tile = 640
    n_mix_pad = -(-n_mix // mix_tile) * mix_tile
    split_points = np.cumsum(IN_SPLITS[:-1])[:-1].tolist()
    bf16 = jnp.bfloat16

    def heads(t, n):
        return t.reshape(B, S, n, HEAD_DIM)

    x = x.reshape(T, D)
    for l in range(DEPTH):
        w_mix = jnp.pad(w_in[l][:, :n_mix], ((0, 0), (0, n_mix_pad - n_mix))).astype(bf16)
        proj = norm_matmul(x, norm1_g[l], w_mix, mix_tile)[:, :n_mix].reshape(B, S, n_mix)
        mg = norm_matmul(x, norm1_g[l], w_in[l][:, n_mix:].astype(bf16), D)
        (fq, fk, fv, ff, lx, lg, dq, dk, dv, nq, nkc, nvc, nks, nvs, nkw, nvw, ng) = \
            jnp.split(proj, split_points, axis=-1)

        cum = jnp.cumsum(jax.nn.log_sigmoid(ff + fox_b_f[l]), axis=1)
        o_fox = fox_attention_pallas(fq, fk.astype(jnp.bfloat16), fv.astype(jnp.bfloat16), cum)
        o_lru = rg_lru_pallas(lx, lg, lru_conv_w[l], lru_conv_b[l], lru_w_a[l], lru_b_a[l],
                              lru_w_x[l], lru_b_x[l], lru_lambda[l])
        o_dil = dilated_attention_pallas(
            rope(heads(dq, DIL_HEADS), pos).reshape(B, S, DIL_W),
            rope(heads(dk, DIL_HEADS), pos).reshape(B, S, DIL_W).astype(jnp.bfloat16),
            dv.astype(jnp.bfloat16))
        o_nsa = nsa_branch(nq, nkc, nvc, nks, nvs, nkw, nvw, ng,
                           nsa_pe_k[l], nsa_pe_v[l], nsa_w1_k[l], nsa_w2_k[l],
                           nsa_w1_v[l], nsa_w2_v[l])

        x, h2 = merge_out(x, mg, o_fox.reshape(T, FOX_W), o_lru.reshape(T, LRU_WIDTH),
                          o_dil.reshape(T, DIL_OUT_W), o_nsa.reshape(T, NSA_W),
                          w_br_fox[l], w_br_lru[l], w_br_dil[l], w_br_nsa[l], w_out[l], norm2_g[l])
        x = peer_ffn(x, h2, peer_w_q[l], peer_sub_keys[l], peer_u[l], peer_v[l])

    return rms_norm_pallas(x, final_g).reshape(B, S, D)
```

```python
import functools

import jax
import jax.numpy as jnp
import numpy as np
from jax import lax
from jax.experimental import pallas as pl
from jax.experimental.pallas import tpu as pltpu

LANES = 128
SUBLANES = 8
MXU_COLS = 256
V7X_VMEM_BYTES = 64 << 20
VMEM_LIMIT_BYTES = V7X_VMEM_BYTES * 3 // 4

D_MODEL = 1024
DEPTH = 2
HEAD_DIM = 64
Q_BLOCK = 128
ROPE_THETA = 10000.0
NORM_EPS = 1e-6
NEG_INF = -1e30
N_BRANCHES = 4
FOX_HEADS = 4
FOX_W = FOX_HEADS * HEAD_DIM
LRU_WIDTH = 256
LRU_CONV = 4
LRU_C = 8.0
DIL_PAIRS = ((128, 1), (512, 4), (2048, 16))
DIL_GROUPS = len(DIL_PAIRS)
DIL_HEADS_PER_GROUP = 2
DIL_HEADS = DIL_GROUPS * DIL_HEADS_PER_GROUP
DIL_W = DIL_HEADS * HEAD_DIM
DIL_OUT_W = DIL_HEADS_PER_GROUP * HEAD_DIM
NSA_HEADS = 4
NSA_KV_HEADS = 1
NSA_W = NSA_HEADS * HEAD_DIM
NSA_KV_W = NSA_KV_HEADS * HEAD_DIM
NSA_CMP_BLOCK = 32
NSA_CMP_STRIDE = 16
NSA_SLC_BLOCK = 64
NSA_TOP_N = 16
NSA_WINDOW = 512
NSA_FORCE_SCORE = 1e6
PEER_HEADS = 8
PEER_N_KEYS = 128
PEER_TOPK = 16
PEER_D_KEY = 256
PEER_BLOCK = 128

IN_SPLITS = (FOX_W, FOX_W, FOX_W, FOX_HEADS,
             LRU_WIDTH, LRU_WIDTH,
             DIL_W, DIL_W, DIL_W,
             NSA_W, NSA_KV_W, NSA_KV_W, NSA_KV_W, NSA_KV_W, NSA_KV_W, NSA_KV_W, NSA_HEADS * 3,
             N_BRANCHES * D_MODEL)


def _rms_norm_kernel(x_ref, g_ref, o_ref):
    x = x_ref[...]
    ms = jnp.mean(x * x, axis=-1, keepdims=True)
    o_ref[...] = x * lax.rsqrt(ms + NORM_EPS) * g_ref[...]


def rms_norm_pallas(x, g):
    t, d = x.shape
    rows = DENSE_ROWS
    return pl.pallas_call(
        _rms_norm_kernel,
        grid=(t // rows,),
        in_specs=[pl.BlockSpec((rows, d), lambda i: (i, 0)),
                  pl.BlockSpec((1, d), lambda i: (0, 0))],
        out_specs=pl.BlockSpec((rows, d), lambda i: (i, 0)),
        out_shape=jax.ShapeDtypeStruct((t, d), x.dtype),
        name="rms_norm",
    )(x, g.reshape(1, d))


def _gelu_tanh(x):
    return 0.5 * x * (1.0 + jnp.tanh(0.7978845608028654 * (x + 0.044715 * x * x * x)))


PEER_TOKENS_PER_STEP = 8


def _peer_mix_kernel(idx_hbm, x_ref, gt_ref, res_ref, uv_hbm, o_ref,
                     idx_smem, buf, idx_sem, row_sem, *, tb, n_sel, d):
    i = pl.program_id(0)
    n = pl.num_programs(0)
    rows = tb * n_sel
    n_chunks = d // LANES
    groups = n_sel // SUBLANES

    def idx_copy(step, s):
        return pltpu.make_async_copy(idx_hbm.at[step], idx_smem.at[pl.ds(s * rows, rows)],
                                     idx_sem.at[s])

    def row_copy(s, j, row):
        return pltpu.make_async_copy(uv_hbm.at[row], buf.at[s, j // SUBLANES, pl.ds(j % SUBLANES, 1)],
                                     row_sem.at[s])

    def slot_wait(s):
        pltpu.make_async_copy(buf.at[s], buf.at[s], row_sem.at[s]).wait()

    @pl.when(i == 0)
    def _():
        idx_copy(0, 0).start()
        idx_copy(0, 0).wait()

        def body(jj, c):
            for k in range(SUBLANES):
                row = idx_smem[jj * SUBLANES + k]
                pltpu.make_async_copy(uv_hbm.at[row], buf.at[0, jj, pl.ds(k, 1)],
                                      row_sem.at[0]).start(priority=k % 2)
            return c
        lax.fori_loop(0, rows // SUBLANES, body, 0)
        idx_copy(1, 1).start()

    def step(cur):
        nxt = 1 - cur
        idx_copy(i + 1, nxt).wait()

        @pl.when(i + 2 <= n)
        def _():
            idx_copy(i + 2, cur).start()

        slot_wait(cur)

        def issue(t, part):
            lo = part * groups // (2 * n_chunks)
            hi = (part + 1) * groups // (2 * n_chunks)
            for j in range(t * n_sel + lo * SUBLANES, t * n_sel + hi * SUBLANES):
                row_copy(nxt, j, idx_smem[nxt * rows + j]).start(priority=j % 2)

        def sel_rows(t, lane0):
            tile = buf[cur, pl.ds(t * groups, groups), :, pl.ds(lane0, LANES)]
            return tile.reshape(n_sel, LANES)

        for t in range(tb):
            acc = jnp.zeros((n_sel, LANES), jnp.float32)
            for c in range(n_chunks):
                issue(t, c)
                xc = x_ref[pl.ds(t, 1), pl.ds(c * LANES, LANES)]
                acc = acc + sel_rows(t, c * LANES) * xc
            s = jnp.sum(acc, axis=-1, keepdims=True)
            w = gt_ref[0, :, pl.ds(t, 1)] * _gelu_tanh(s)
            wb = jnp.broadcast_to(w, (n_sel, LANES))
            for c in range(n_chunks):
                issue(t, n_chunks + c)
                vc = sel_rows(t, d + c * LANES)
                oc = jnp.sum(wb * vc, axis=0, keepdims=True)
                o_ref[pl.ds(t, 1), pl.ds(c * LANES, LANES)] = (
                    res_ref[pl.ds(t, 1), pl.ds(c * LANES, LANES)] + oc)

        @pl.when(i == n - 1)
        def _():
            slot_wait(nxt)

    @pl.when(i % 2 == 0)
    def _():
        step(0)

    @pl.when(i % 2 == 1)
    def _():
        step(1)


def peer_mix(x, res, idx, gt_blocks, uv):
    t, d = x.shape
    n_sel = idx.shape[1]
    tb = PEER_TOKENS_PER_STEP
    steps = t // tb
    spare = (jnp.arange(tb * n_sel, dtype=jnp.int32) % uv.shape[0]).reshape(1, tb * n_sel)
    idx_steps = jnp.concatenate([idx.reshape(steps, tb * n_sel), spare], axis=0)
    nb, _, tq = gt_blocks.shape
    gt = jnp.transpose(gt_blocks.reshape(nb, n_sel, tq // tb, tb), (0, 2, 1, 3)).reshape(steps, n_sel, tb)
    kern = functools.partial(_peer_mix_kernel, tb=tb, n_sel=n_sel, d=d)
    return pl.pallas_call(
        kern,
        grid=(steps,),
        in_specs=[pl.BlockSpec(memory_space=pl.ANY),
                  pl.BlockSpec((tb, d), lambda i: (i, 0)),
                  pl.BlockSpec((1, n_sel, tb), lambda i: (i, 0, 0)),
                  pl.BlockSpec((tb, d), lambda i: (i, 0)),
                  pl.BlockSpec(memory_space=pl.ANY)],
        out_specs=pl.BlockSpec((tb, d), lambda i: (i, 0)),
        out_shape=jax.ShapeDtypeStruct((t, d), jnp.float32),
        scratch_shapes=[pltpu.SMEM((2 * tb * n_sel,), jnp.int32),
                        pltpu.VMEM((2, tb * n_sel // SUBLANES, SUBLANES, 2 * d), jnp.float32),
                        pltpu.SemaphoreType.DMA((2,)),
                        pltpu.SemaphoreType.DMA((2,))],
        compiler_params=pltpu.CompilerParams(
            dimension_semantics=("arbitrary",),
            vmem_limit_bytes=VMEM_LIMIT_BYTES),
        name="peer_mix",
    )(idx_steps, x, gt, res, uv.reshape(uv.shape[0], 1, 2 * d))


def _topk_rows(work, k):
    n = work.shape[0]
    row = lax.broadcasted_iota(jnp.int32, work.shape, 0)
    vals, idxs = [], []
    for _ in range(k):
        mx = jnp.max(work, axis=0, keepdims=True)
        pos = jnp.min(jnp.where(work == mx, row, n), axis=0, keepdims=True)
        work = jnp.where(row == pos, -jnp.inf, work)
        vals.append(mx)
        idxs.append(pos)
    return jnp.concatenate(vals, axis=0), jnp.concatenate(idxs, axis=0)


def _peer_route_kernel(x_ref, wq_ref, keys_ref, idx_ref, gt_ref, q_s, idx_s, g_s, *, tq):
    bf16 = jnp.bfloat16
    k = PEER_TOPK
    half = PEER_D_KEY // 2
    q = jnp.dot(x_ref[...].astype(bf16), wq_ref[...], preferred_element_type=jnp.float32)
    for c in range(2 * PEER_HEADS):
        q_s[c] = q[:, c * half:(c + 1) * half].astype(bf16)

    def head_body(h, carry):
        tops = []
        for p in range(2):
            scores = _nt_dot(keys_ref[p], q_s[2 * h + p])
            tops.append(_topk_rows(scores, k))
        (v0, i0), (v1, i1) = tops
        keep = [k // (a + 1) for a in range(k)]
        n_cand = sum(keep)
        n_pad = -n_cand % SUBLANES
        cand = jnp.concatenate([v0[a:a + 1, :] + v1[:keep[a]] for a in range(k)]
                               + [jnp.full((n_pad, tq), -jnp.inf, jnp.float32)], axis=0)
        cand_idx = jnp.concatenate([i0[a:a + 1, :] * PEER_N_KEYS + i1[:keep[a]] for a in range(k)]
                                   + [jnp.full((n_pad, tq), -1, jnp.int32)], axis=0)
        flat = lax.broadcasted_iota(jnp.int32, cand.shape, 0)
        top_s, experts = [], []
        for _ in range(k):
            mx = jnp.max(cand, axis=0, keepdims=True)
            pos = jnp.min(jnp.where(cand == mx, flat, n_cand + n_pad), axis=0, keepdims=True)
            pick = flat == pos
            experts.append(jnp.max(jnp.where(pick, cand_idx, -1), axis=0, keepdims=True))
            cand = jnp.where(pick, -jnp.inf, cand)
            top_s.append(mx)
        top_s = jnp.concatenate(top_s, axis=0)
        ex = jnp.exp(top_s - jnp.max(top_s, axis=0, keepdims=True))
        r0 = pl.multiple_of(h * k, k)
        g_s[pl.ds(r0, k), :] = ex / jnp.sum(ex, axis=0, keepdims=True)
        idx_s[pl.ds(r0, k), :] = jnp.concatenate(experts, axis=0)
        return carry
    lax.fori_loop(0, PEER_HEADS, head_body, 0)

    idx_ref[...] = jnp.transpose(idx_s[...])
    gt_ref[0] = g_s[...]


def peer_route(h2, w_q, sub_keys):
    t, d = h2.shape
    tq = PEER_BLOCK
    n_sel = PEER_HEADS * PEER_TOPK
    half = PEER_D_KEY // 2
    return pl.pallas_call(
        functools.partial(_peer_route_kernel, tq=tq),
        grid=(t // tq,),
        in_specs=[pl.BlockSpec((tq, d), lambda i: (i, 0)),
                  pl.BlockSpec((d, PEER_HEADS * PEER_D_KEY), lambda i: (0, 0)),
                  pl.BlockSpec((2, PEER_N_KEYS, half), lambda i: (0, 0, 0))],
        out_specs=[pl.BlockSpec((tq, n_sel), lambda i: (i, 0)),
                   pl.BlockSpec((1, n_sel, tq), lambda i: (i, 0, 0))],
        out_shape=[jax.ShapeDtypeStruct((t, n_sel), jnp.int32),
                   jax.ShapeDtypeStruct((t // tq, n_sel, tq), jnp.float32)],
        scratch_shapes=[pltpu.VMEM((2 * PEER_HEADS, tq, half), jnp.bfloat16),
                        pltpu.VMEM((n_sel, tq), jnp.int32),
                        pltpu.VMEM((n_sel, tq), jnp.float32)],
        compiler_params=pltpu.CompilerParams(dimension_semantics=("arbitrary",),
                                             vmem_limit_bytes=VMEM_LIMIT_BYTES),
        name="peer_route",
    )(h2, w_q.astype(jnp.bfloat16), sub_keys.astype(jnp.bfloat16))


NSA_KV_TILE = 512


def _nt_dot(a, b):
    return lax.dot_general(a, b, (((1,), (1,)), ((), ())), preferred_element_type=jnp.float32)


def _nsa_kernel(qn_ref, qr_ref, kc_ref, vc_ref, ov_ref, ks_ref, vs_ref, kw_ref, vw_ref, gl_ref,
                o_ref, m_s, l_s, acc_s, out_s, la_s, lb_s, *, n_heads, qb, n_cmp_pad, n_slc):
    j = pl.program_id(1)
    s0 = j * qb
    hq = n_heads * qb
    bf16 = jnp.bfloat16
    scale = HEAD_DIM ** -0.5
    q_pos = s0 + lax.broadcasted_iota(jnp.int32, (qb, 1), 0)

    gates = jax.nn.sigmoid(gl_ref[0])

    def gate_col(br):
        return jnp.concatenate([gates[:, 3 * h + br:3 * h + br + 1] for h in range(n_heads)], axis=0)

    def masked_update(logits, mask, v_tile):
        kt = logits.shape[-1]
        lg = jnp.where(mask[None], logits.reshape(n_heads, qb, kt), NEG_INF)
        m_old = m_s[...].reshape(n_heads, qb, 1)
        m_new = jnp.maximum(m_old, jnp.max(lg, axis=-1, keepdims=True))
        p = jnp.where(mask[None], jnp.exp(lg - m_new), 0.0)
        alpha = jnp.exp(m_old - m_new)
        l_s[...] = (alpha * l_s[...].reshape(n_heads, qb, 1)
                    + jnp.sum(p, axis=-1, keepdims=True)).reshape(hq, 1)
        pv = jnp.dot(p.reshape(hq, kt).astype(bf16), v_tile, preferred_element_type=jnp.float32)
        acc_s[...] = alpha.reshape(hq, 1) * acc_s[...] + pv
        m_s[...] = m_new.reshape(hq, 1)

    def reset_state():
        m_s[...] = jnp.full((hq, 1), NEG_INF, jnp.float32)
        l_s[...] = jnp.zeros((hq, 1), jnp.float32)
        acc_s[...] = jnp.zeros((hq, HEAD_DIM), jnp.float32)

    def normalized():
        return acc_s[...] / jnp.maximum(l_s[...], 1e-30)

    qn = (qn_ref[0].reshape(hq, HEAD_DIM) * scale).astype(bf16)
    lc = _nt_dot(qn, kc_ref[0]).reshape(n_heads, qb, n_cmp_pad)
    c_end = (lax.broadcasted_iota(jnp.int32, (qb, n_cmp_pad), 1) * NSA_CMP_STRIDE
             + (NSA_CMP_BLOCK - 1))
    c_mask = c_end <= q_pos
    lc = jnp.where(c_mask[None], lc, NEG_INF)
    mc = jnp.max(lc, axis=-1, keepdims=True)
    ec = jnp.where(c_mask[None], jnp.exp(lc - mc), 0.0)
    pc = ec / jnp.maximum(jnp.sum(ec, axis=-1, keepdims=True), 1e-30)
    o_c = jnp.dot(pc.reshape(hq, n_cmp_pad).astype(bf16), vc_ref[0],
                  preferred_element_type=jnp.float32)
    out_s[...] = gate_col(0) * o_c

    p_sum = jnp.sum(pc, axis=0)
    p_hi = p_sum.astype(bf16)
    p_lo = (p_sum - p_hi.astype(jnp.float32)).astype(bf16)
    imp = (jnp.dot(p_hi, ov_ref[...], preferred_element_type=jnp.float32)
           + jnp.dot(p_lo, ov_ref[...], preferred_element_type=jnp.float32))
    blk = lax.broadcasted_iota(jnp.int32, (qb, n_slc), 1)
    cur = q_pos // NSA_SLC_BLOCK
    forced = (blk == 0) | (blk == cur) | (blk == cur - 1)
    imp = jnp.where(forced, NSA_FORCE_SCORE, imp)
    valid = blk <= cur
    work = jnp.transpose(jnp.where(valid, imp, NEG_INF))
    blk_t = lax.broadcasted_iota(jnp.int32, (n_slc, qb), 0)
    sel_t = jnp.zeros((n_slc, qb), jnp.float32)
    for _ in range(min(NSA_TOP_N, n_slc)):
        mx = jnp.max(work, axis=0, keepdims=True)
        first = jnp.min(jnp.where(work == mx, blk_t, n_slc), axis=0, keepdims=True)
        pick = blk_t == first
        sel_t = jnp.where(pick, 1.0, sel_t)
        work = jnp.where(pick, -jnp.inf, work)
    sel_bf = jnp.where(valid, jnp.transpose(sel_t), 0.0).astype(bf16)

    qr = (qr_ref[0].reshape(hq, HEAD_DIM) * scale).astype(bf16)
    reset_state()
    w_len = NSA_WINDOW + qb
    w0 = pl.multiple_of(jnp.maximum(s0 - NSA_WINDOW, 0), qb)
    dist = q_pos - (w0 + lax.broadcasted_iota(jnp.int32, (qb, w_len), 1))
    masked_update(_nt_dot(qr, kw_ref[0, pl.ds(w0, w_len), :]), (dist >= 0) & (dist < NSA_WINDOW),
                  vw_ref[0, pl.ds(w0, w_len), :])
    out_s[...] += gate_col(2) * normalized()

    reset_state()
    kt = NSA_KV_TILE
    blocks_per_tile = kt // NSA_SLC_BLOCK
    n_tiles = (s0 + qb - 1) // kt + 1

    def tile_start(t):
        return pl.multiple_of(jnp.minimum(t, n_tiles - 1) * kt, kt)

    def scores(t):
        return _nt_dot(qr, ks_ref[0, pl.ds(tile_start(t), kt), :])

    def consume(logits, t):
        k_idx = lax.broadcasted_iota(jnp.int32, (n_slc, kt), 1)
        expand = (lax.broadcasted_iota(jnp.int32, (n_slc, kt), 0)
                  == t * blocks_per_tile + k_idx // NSA_SLC_BLOCK)
        picked = jnp.dot(sel_bf, jnp.where(expand, 1.0, 0.0).astype(bf16),
                         preferred_element_type=jnp.float32)
        k_pos = t * kt + lax.broadcasted_iota(jnp.int32, (qb, kt), 1)
        mask = (picked > 0.5) & (k_pos <= q_pos)
        masked_update(logits, mask, vs_ref[0, pl.ds(tile_start(t), kt), :])

    la_s[...] = scores(0)

    def sel_body(u, c):
        t = 2 * u
        lb_s[...] = scores(t + 1)
        consume(la_s[...], t)
        la_s[...] = scores(t + 2)
        consume(lb_s[...], t + 1)
        return c
    lax.fori_loop(0, (n_tiles + 1) // 2, sel_body, 0)
    out_s[...] += gate_col(1) * normalized()

    o_ref[0] = out_s[...].reshape(n_heads, qb, HEAD_DIM)


def _nsa_compress_kernel(x_ref, pe_ref, w1_ref, w2_ref, o_ref, *, n_cmp):
    bf16 = jnp.bfloat16
    x = x_ref[0]
    n_chunks, half = x.shape
    head = jnp.dot((x + pe_ref[0:1, :]).astype(bf16), w1_ref[pl.ds(0, half), :],
                   preferred_element_type=jnp.float32)
    tail = jnp.dot((x + pe_ref[1:2, :]).astype(bf16), w1_ref[pl.ds(half, half), :],
                   preferred_element_type=jnp.float32)
    hidden = jax.nn.gelu(head + pltpu.roll(tail, n_chunks - 1, 0))
    out = jnp.dot(hidden.astype(bf16), w2_ref[...], preferred_element_type=jnp.float32)
    row = lax.broadcasted_iota(jnp.int32, out.shape, 0)
    o_ref[0] = jnp.where(row < n_cmp, out, 0.0).astype(o_ref.dtype)


def nsa_compress(kv, pe, w1, w2):
    b, s, dh = kv.shape
    n_chunks = s // NSA_CMP_STRIDE
    n_cmp = (s - NSA_CMP_BLOCK) // NSA_CMP_STRIDE + 1
    half = NSA_CMP_STRIDE * dh
    hidden = w1.shape[1]
    return pl.pallas_call(
        functools.partial(_nsa_compress_kernel, n_cmp=n_cmp),
        grid=(b,),
        in_specs=[pl.BlockSpec((1, n_chunks, half), lambda i: (i, 0, 0)),
                  pl.BlockSpec((2, half), lambda i: (0, 0)),
                  pl.BlockSpec((2 * half, hidden), lambda i: (0, 0)),
                  pl.BlockSpec((hidden, dh), lambda i: (0, 0))],
        out_specs=pl.BlockSpec((1, n_chunks, dh), lambda i: (i, 0, 0)),
        out_shape=jax.ShapeDtypeStruct((b, n_chunks, dh), jnp.bfloat16),
        compiler_params=pltpu.CompilerParams(dimension_semantics=("arbitrary",)),
        name="nsa_compress",
    )(kv.reshape(b, n_chunks, half), pe.reshape(2, half), w1.astype(jnp.bfloat16),
      w2.astype(jnp.bfloat16))


def nsa_attention_pallas(qn, qr, kc, vc, overlap, ks, vs, kw, vw, gl):
    b, h, s, dh = qn.shape
    qb = Q_BLOCK
    n_cmp_pad = kc.shape[1]
    n_slc = overlap.shape[1]
    kern = functools.partial(_nsa_kernel, n_heads=h, qb=qb, n_cmp_pad=n_cmp_pad, n_slc=n_slc)
    q_spec = pl.BlockSpec((1, h, qb, dh), lambda i, j: (i, 0, j, 0))
    cmp_spec = pl.BlockSpec((1, n_cmp_pad, dh), lambda i, j: (i, 0, 0))
    kv_spec = pl.BlockSpec((1, s, dh), lambda i, j: (i, 0, 0))
    return pl.pallas_call(
        kern,
        grid=(b, s // qb),
        in_specs=[q_spec, q_spec, cmp_spec, cmp_spec,
                  pl.BlockSpec((n_cmp_pad, n_slc), lambda i, j: (0, 0)),
                  kv_spec, kv_spec, kv_spec, kv_spec,
                  pl.BlockSpec((1, qb, 3 * h), lambda i, j: (i, j, 0))],
        out_specs=q_spec,
        out_shape=jax.ShapeDtypeStruct((b, h, s, dh), jnp.float32),
        scratch_shapes=[pltpu.VMEM((h * qb, 1), jnp.float32),
                        pltpu.VMEM((h * qb, 1), jnp.float32),
                        pltpu.VMEM((h * qb, dh), jnp.float32),
                        pltpu.VMEM((h * qb, dh), jnp.float32),
                        pltpu.VMEM((h * qb, NSA_KV_TILE), jnp.float32),
                        pltpu.VMEM((h * qb, NSA_KV_TILE), jnp.float32)],
        compiler_params=pltpu.CompilerParams(
            dimension_semantics=("arbitrary", "arbitrary"),
            vmem_limit_bytes=VMEM_LIMIT_BYTES),
        name="nsa_attention",
    )(qn, qr, kc, vc, overlap, ks, vs, kw, vw, gl)


def _dilated_kernel(q_ref, k_ref, v_ref, o_ref, *, qb):
    j = pl.program_id(1)
    s0 = j * qb
    bf16 = jnp.bfloat16
    scale = HEAD_DIM ** -0.5
    q_pos = s0 + lax.broadcasted_iota(jnp.int32, (qb, 1), 0)
    lane = lax.broadcasted_iota(jnp.int32, (qb, LANES), 1)
    first_head = lane < HEAD_DIM
    rows = DIL_HEADS_PER_GROUP * qb

    outs, lses = [], []
    for g, (window, dil) in enumerate(DIL_PAIRS):
        c0 = g * LANES
        q = q_ref[0, :, pl.ds(c0, LANES)] * scale
        q2 = jnp.concatenate([jnp.where(first_head, q, 0.0),
                              jnp.where(first_head, 0.0, q)], axis=0).astype(bf16)
        span = window + qb
        k0 = pl.multiple_of(jnp.maximum(s0 - window, 0), qb)
        dist = q_pos - (k0 + lax.broadcasted_iota(jnp.int32, (qb, span), 1))
        mask = (dist >= 0) & (dist <= window) & ((dist & (dil - 1)) == 0)
        logits = _nt_dot(q2, k_ref[0, pl.ds(k0, span), pl.ds(c0, LANES)])
        lg = jnp.where(mask[None], logits.reshape(DIL_HEADS_PER_GROUP, qb, span), NEG_INF)
        m = jnp.max(lg, axis=-1, keepdims=True)
        e = jnp.exp(lg - m)
        l = jnp.sum(e, axis=-1, keepdims=True).reshape(rows, 1)
        pv = jnp.dot(e.reshape(rows, span).astype(bf16), v_ref[0, pl.ds(k0, span), pl.ds(c0, LANES)],
                     preferred_element_type=jnp.float32)
        o2 = pv / l
        lse2 = m.reshape(rows, 1) + jnp.log(l)
        outs.append(jnp.where(first_head, o2[:qb], o2[qb:]))
        lses.append(jnp.where(first_head, lse2[:qb], lse2[qb:]))

    lse_max = functools.reduce(jnp.maximum, lses)
    ws = [jnp.exp(x - lse_max) for x in lses]
    den = functools.reduce(lambda a, b: a + b, ws)
    num = functools.reduce(lambda a, b: a + b, [w * o for w, o in zip(ws, outs)])
    o_ref[0] = num / den


def dilated_attention_pallas(q, k, v):
    b, s, w = q.shape
    qb = Q_BLOCK
    return pl.pallas_call(
        functools.partial(_dilated_kernel, qb=qb),
        grid=(b, s // qb),
        in_specs=[pl.BlockSpec((1, qb, w), lambda i, j: (i, j, 0)),
                  pl.BlockSpec((1, s, w), lambda i, j: (i, 0, 0)),
                  pl.BlockSpec((1, s, w), lambda i, j: (i, 0, 0))],
        out_specs=pl.BlockSpec((1, qb, DIL_OUT_W), lambda i, j: (i, j, 0)),
        out_shape=jax.ShapeDtypeStruct((b, s, DIL_OUT_W), jnp.float32),
        compiler_params=pltpu.CompilerParams(
            dimension_semantics=("arbitrary", "arbitrary"),
            vmem_limit_bytes=VMEM_LIMIT_BYTES),
        name="dilated_attention",
    )(q, k, v)


FOX_KV_TILE = 512


def _fox_kernel(q_ref, k_ref, v_ref, cq_ref, ck_ref, o_ref, m_s, l_s, acc_s, la_s, lb_s, *, qb):
    j = pl.program_id(1)
    s0 = j * qb
    kt = FOX_KV_TILE
    bf16 = jnp.bfloat16
    scale = HEAD_DIM ** -0.5
    q_pos = s0 + lax.broadcasted_iota(jnp.int32, (qb, 1), 0)
    lane = lax.broadcasted_iota(jnp.int32, (qb, LANES), 1)
    first_head = lane < HEAD_DIM
    n_pairs = FOX_HEADS // 2
    rows = FOX_HEADS * qb
    n_tiles = (s0 + qb - 1) // kt + 1

    q2 = []
    for pair in range(n_pairs):
        q = q_ref[0, :, pl.ds(pair * LANES, LANES)] * scale
        q2.append(jnp.concatenate([jnp.where(first_head, q, 0.0),
                                   jnp.where(first_head, 0.0, q)], axis=0).astype(bf16))
    cq = [cq_ref[0, :, h:h + 1] for h in range(FOX_HEADS)]
    m_s[...] = jnp.full((rows, 1), NEG_INF, jnp.float32)
    l_s[...] = jnp.zeros((rows, 1), jnp.float32)
    acc_s[...] = jnp.zeros((rows, LANES), jnp.float32)

    def tile_start(t):
        return pl.multiple_of(jnp.minimum(t, n_tiles - 1) * kt, kt)

    def biased_logits(t):
        k0 = tile_start(t)
        logits = jnp.concatenate(
            [_nt_dot(q2[pair], k_ref[0, pl.ds(k0, kt), pl.ds(pair * LANES, LANES)])
             for pair in range(n_pairs)], axis=0)
        bias = jnp.concatenate([cq[h] - ck_ref[0, h:h + 1, pl.ds(k0, kt)] for h in range(FOX_HEADS)],
                               axis=0)
        return logits + bias

    def consume(logits, t):
        k0 = tile_start(t)
        mask = t * kt + lax.broadcasted_iota(jnp.int32, (qb, kt), 1) <= q_pos
        lg = jnp.where(mask[None], logits.reshape(FOX_HEADS, qb, kt), NEG_INF)
        m_old = m_s[...].reshape(FOX_HEADS, qb, 1)
        m_new = jnp.maximum(m_old, jnp.max(lg, axis=-1, keepdims=True))
        p = jnp.exp(lg - m_new)
        alpha = jnp.exp(m_old - m_new)
        l_s[...] = (alpha * l_s[...].reshape(FOX_HEADS, qb, 1)
                    + jnp.sum(p, axis=-1, keepdims=True)).reshape(rows, 1)
        pb = p.reshape(rows, kt).astype(bf16)
        pv = jnp.concatenate(
            [jnp.dot(pb[2 * pair * qb:2 * (pair + 1) * qb],
                     v_ref[0, pl.ds(k0, kt), pl.ds(pair * LANES, LANES)],
                     preferred_element_type=jnp.float32) for pair in range(n_pairs)], axis=0)
        acc_s[...] = alpha.reshape(rows, 1) * acc_s[...] + pv
        m_s[...] = m_new.reshape(rows, 1)

    la_s[...] = biased_logits(0)

    def body(u, c):
        t = 2 * u
        lb_s[...] = biased_logits(t + 1)
        consume(la_s[...], t)
        la_s[...] = biased_logits(t + 2)
        consume(lb_s[...], t + 1)
        return c
    lax.fori_loop(0, (n_tiles + 1) // 2, body, 0)

    o2 = acc_s[...] / l_s[...]
    for pair in range(n_pairs):
        r0 = 2 * pair * qb
        o_ref[0, :, pl.ds(pair * LANES, LANES)] = jnp.where(first_head, o2[r0:r0 + qb],
                                                             o2[r0 + qb:r0 + 2 * qb])


def fox_attention_pallas(q, k, v, cum):
    b, s, w = q.shape
    qb = Q_BLOCK
    return pl.pallas_call(
        functools.partial(_fox_kernel, qb=qb),
        grid=(b, s // qb),
        in_specs=[pl.BlockSpec((1, qb, w), lambda i, j: (i, j, 0)),
                  pl.BlockSpec((1, s, w), lambda i, j: (i, 0, 0)),
                  pl.BlockSpec((1, s, w), lambda i, j: (i, 0, 0)),
                  pl.BlockSpec((1, qb, FOX_HEADS), lambda i, j: (i, j, 0)),
                  pl.BlockSpec((1, FOX_HEADS, s), lambda i, j: (i, 0, 0))],
        out_specs=pl.BlockSpec((1, qb, w), lambda i, j: (i, j, 0)),
        out_shape=jax.ShapeDtypeStruct((b, s, w), jnp.float32),
        scratch_shapes=[pltpu.VMEM((FOX_HEADS * qb, 1), jnp.float32),
                        pltpu.VMEM((FOX_HEADS * qb, 1), jnp.float32),
                        pltpu.VMEM((FOX_HEADS * qb, LANES), jnp.float32),
                        pltpu.VMEM((FOX_HEADS * qb, FOX_KV_TILE), jnp.float32),
                        pltpu.VMEM((FOX_HEADS * qb, FOX_KV_TILE), jnp.float32)],
        compiler_params=pltpu.CompilerParams(
            dimension_semantics=("arbitrary", "arbitrary"),
            vmem_limit_bytes=VMEM_LIMIT_BYTES),
        name="fox_attention",
    )(q, k, v, cum, jnp.transpose(cum, (0, 2, 1)))


LRU_TILE = 512


def _lru_kernel(x_ref, gate_ref, cw_ref, cb_ref, wa_ref, ba_ref, wx_ref, bx_ref, lam_ref, o_ref,
                xbuf, a_s, u_s, h_s, hlast, *, tt):
    j = pl.program_id(1)
    c = x_ref.shape[-1]
    bf16 = jnp.bfloat16

    @pl.when(j == 0)
    def _():
        xbuf[pl.ds(0, SUBLANES), :] = jnp.zeros((SUBLANES, c), jnp.float32)
        hlast[...] = jnp.zeros((1, c), jnp.float32)

    x = x_ref[0]
    xbuf[pl.ds(SUBLANES, tt), :] = x
    xc = cb_ref[...] + cw_ref[LRU_CONV - 1:LRU_CONV, :] * x
    for back in range(1, LRU_CONV):
        xc = xc + cw_ref[LRU_CONV - 1 - back:LRU_CONV - back, :] * xbuf[pl.ds(SUBLANES - back, tt), :]
    xbuf[pl.ds(0, SUBLANES), :] = x[tt - SUBLANES:, :]

    xb = xc.astype(bf16)
    r = jax.nn.sigmoid(jnp.dot(xb, wa_ref[...], preferred_element_type=jnp.float32) + ba_ref[...])
    i_g = jax.nn.sigmoid(jnp.dot(xb, wx_ref[...], preferred_element_type=jnp.float32) + bx_ref[...])
    lam = lam_ref[...]
    softplus_neg = jnp.maximum(-lam, 0.0) + jnp.log1p(jnp.exp(-jnp.abs(lam)))
    log_a = -LRU_C * r * softplus_neg
    a_s[...] = jnp.exp(log_a)
    u_s[...] = jnp.sqrt(1.0 - jnp.exp(2.0 * log_a)) * (i_g * xc)

    def group(gi, h):
        r0 = pl.multiple_of(gi * SUBLANES, SUBLANES)
        a8 = a_s[pl.ds(r0, SUBLANES), :]
        u8 = u_s[pl.ds(r0, SUBLANES), :]
        rows = []
        for s in range(SUBLANES):
            h = a8[s:s + 1, :] * h + u8[s:s + 1, :]
            rows.append(h)
        h_s[pl.ds(r0, SUBLANES), :] = jnp.concatenate(rows, axis=0)
        return h
    hlast[...] = lax.fori_loop(0, tt // SUBLANES, group, hlast[...])
    o_ref[0] = h_s[...] * jax.nn.gelu(gate_ref[0])


def _block_diag(w):
    n, c, d = w.shape
    eye = jnp.eye(n, dtype=w.dtype)
    return (eye[:, None, :, None] * w[:, :, None, :]).reshape(n * c, n * d)


def rg_lru_pallas(x_in, gate, conv_w, conv_b, w_a, b_a, w_x, b_x, lam):
    b, s, c = x_in.shape
    tt = LRU_TILE
    row = lambda v: v.reshape(1, c)
    tok_spec = pl.BlockSpec((1, tt, c), lambda i, j: (i, j, 0))
    full = lambda shape: pl.BlockSpec(shape, lambda i, j: (0,) * len(shape))
    return pl.pallas_call(
        functools.partial(_lru_kernel, tt=tt),
        grid=(b, s // tt),
        in_specs=[tok_spec, tok_spec, full((LRU_CONV, c)), full((1, c)), full((c, c)), full((1, c)),
                  full((c, c)), full((1, c)), full((1, c))],
        out_specs=tok_spec,
        out_shape=jax.ShapeDtypeStruct((b, s, c), jnp.float32),
        scratch_shapes=[pltpu.VMEM((tt + SUBLANES, c), jnp.float32),
                        pltpu.VMEM((tt, c), jnp.float32),
                        pltpu.VMEM((tt, c), jnp.float32),
                        pltpu.VMEM((tt, c), jnp.float32),
                        pltpu.VMEM((1, c), jnp.float32)],
        compiler_params=pltpu.CompilerParams(dimension_semantics=("arbitrary", "arbitrary")),
        name="rg_lru",
    )(x_in, gate, conv_w, row(conv_b), _block_diag(w_a).astype(jnp.bfloat16), row(b_a),
      _block_diag(w_x).astype(jnp.bfloat16), row(b_x), row(lam))


DENSE_ROWS = 512


def _norm_matmul_kernel(x_ref, g_ref, w_ref, o_ref, xn_s):
    @pl.when(pl.program_id(1) == 0)
    def _():
        x = x_ref[...]
        y = x * lax.rsqrt(jnp.mean(x * x, axis=-1, keepdims=True) + NORM_EPS) * g_ref[...]
        xn_s[...] = y.astype(jnp.bfloat16)
    o_ref[...] = jnp.dot(xn_s[...], w_ref[...], preferred_element_type=jnp.float32)


def norm_matmul(x, g, w, tn):
    t, d = x.shape
    n = w.shape[1]
    tm = DENSE_ROWS
    return pl.pallas_call(
        _norm_matmul_kernel,
        grid=(t // tm, n // tn),
        in_specs=[pl.BlockSpec((tm, d), lambda i, j: (i, 0)),
                  pl.BlockSpec((1, d), lambda i, j: (0, 0)),
                  pl.BlockSpec((d, tn), lambda i, j: (0, j))],
        out_specs=pl.BlockSpec((tm, tn), lambda i, j: (i, j)),
        out_shape=jax.ShapeDtypeStruct((t, n), jnp.float32),
        scratch_shapes=[pltpu.VMEM((tm, d), jnp.bfloat16)],
        compiler_params=pltpu.CompilerParams(dimension_semantics=("arbitrary", "arbitrary"),
                                             vmem_limit_bytes=VMEM_LIMIT_BYTES),
        name="norm_matmul",
    )(x, g.reshape(1, d), w)


def _merge_out_kernel(x_ref, mg_ref, of_ref, ol_ref, od_ref, on_ref, wf_ref, wl_ref, wd_ref, wn_ref,
                      wo_ref, g2_ref, y_ref, h2_ref, *, d):
    bf16 = jnp.bfloat16
    merged = None
    for b, (o_ref, w_ref) in enumerate(((of_ref, wf_ref), (ol_ref, wl_ref), (od_ref, wd_ref),
                                        (on_ref, wn_ref))):
        y = jnp.dot(o_ref[...].astype(bf16), w_ref[...], preferred_element_type=jnp.float32)
        term = jax.nn.sigmoid(mg_ref[:, pl.ds(b * d, d)]) * y
        merged = term if merged is None else merged + term
    y = x_ref[...] + jnp.dot(merged.astype(bf16), wo_ref[...], preferred_element_type=jnp.float32)
    y_ref[...] = y
    h2_ref[...] = y * lax.rsqrt(jnp.mean(y * y, axis=-1, keepdims=True) + NORM_EPS) * g2_ref[...]


def merge_out(x, mg, o_fox, o_lru, o_dil, o_nsa, w_fox, w_lru, w_dil, w_nsa, w_out, g2):
    t, d = x.shape
    tm = DENSE_ROWS // 2
    bf16 = jnp.bfloat16
    tok = lambda a: pl.BlockSpec((tm, a.shape[1]), lambda i: (i, 0))
    full = lambda a: pl.BlockSpec(a.shape, lambda i: (0, 0))
    ws = [w.astype(bf16) for w in (w_fox, w_lru, w_dil, w_nsa, w_out)]
    g2r = g2.reshape(1, d)
    acts = (x, mg, o_fox, o_lru, o_dil, o_nsa)
    return pl.pallas_call(
        functools.partial(_merge_out_kernel, d=d),
        grid=(t // tm,),
        in_specs=[tok(a) for a in acts] + [full(w) for w in ws] + [full(g2r)],
        out_specs=[pl.BlockSpec((tm, d), lambda i: (i, 0)), pl.BlockSpec((tm, d), lambda i: (i, 0))],
        out_shape=[jax.ShapeDtypeStruct((t, d), jnp.float32), jax.ShapeDtypeStruct((t, d), jnp.float32)],
        compiler_params=pltpu.CompilerParams(dimension_semantics=("arbitrary",),
                                             vmem_limit_bytes=VMEM_LIMIT_BYTES),
        name="merge_out",
    )(*acts, *ws, g2r)


def rope(x, pos):
    half = x.shape[-1] // 2
    freqs = ROPE_THETA ** (-jnp.arange(half, dtype=jnp.float32) / half)
    ang = pos.astype(jnp.float32)[:, None] * freqs[None, :]
    cos = jnp.cos(ang)[None, :, None, :]
    sin = jnp.sin(ang)[None, :, None, :]
    xf = x.astype(jnp.float32)
    x1, x2 = xf[..., :half], xf[..., half:]
    return jnp.concatenate([x1 * cos - x2 * sin, x2 * cos + x1 * sin], axis=-1).astype(x.dtype)


def nsa_branch(nq, nkc, nvc, nks, nvs, nkw, nvw, ng, pe_k, pe_v, w1_k, w2_k, w1_v, w2_v):
    B, S, _ = nq.shape
    dh = HEAD_DIM
    bf16 = jnp.bfloat16
    pos = jnp.arange(S)
    q4 = nq.reshape(B, S, NSA_HEADS, dh)
    qn = jnp.transpose(q4, (0, 2, 1, 3))
    qr = jnp.transpose(rope(q4, pos), (0, 2, 1, 3))
    ks = rope(nks.reshape(B, S, 1, dh), pos).reshape(B, S, dh).astype(bf16)
    kw = rope(nkw.reshape(B, S, 1, dh), pos).reshape(B, S, dh).astype(bf16)

    n_cmp = (S - NSA_CMP_BLOCK) // NSA_CMP_STRIDE + 1
    n_cmp_pad = S // NSA_CMP_STRIDE
    n_slc = S // NSA_SLC_BLOCK

    kc = nsa_compress(nkc, pe_k, w1_k, w2_k)
    vc = nsa_compress(nvc, pe_v, w1_v, w2_v)
    cmp_start = jnp.arange(n_cmp_pad) * NSA_CMP_STRIDE
    slc_start = jnp.arange(n_slc) * NSA_SLC_BLOCK
    overlap = ((cmp_start[:, None] < slc_start[None, :] + NSA_SLC_BLOCK)
               & (cmp_start[:, None] + NSA_CMP_BLOCK > slc_start[None, :])
               & (jnp.arange(n_cmp_pad)[:, None] < n_cmp)).astype(bf16)
    out = nsa_attention_pallas(qn, qr, kc, vc, overlap, ks, nvs.astype(bf16), kw,
                               nvw.astype(bf16), ng)
    return jnp.transpose(out, (0, 2, 1, 3)).reshape(B, S, NSA_W)


def peer_ffn(x_res, h2, w_q, sub_keys, u, v):
    expert, gt = peer_route(h2, w_q, sub_keys)
    uv = jnp.concatenate([u, v], axis=1)
    return peer_mix(h2, x_res, expert, gt, uv)


def kernel(x, norm1_g, w_in, fox_b_f, lru_conv_w, lru_conv_b, lru_w_a, lru_b_a, lru_w_x, lru_b_x, lru_lambda, nsa_pe_k, nsa_pe_v, nsa_w1_k, nsa_w2_k, nsa_w1_v, nsa_w2_v, w_br_fox, w_br_lru, w_br_dil, w_br_nsa, w_out, norm2_g, peer_w_q, peer_sub_keys, peer_u, peer_v, final_g):
    B, S, D = x.shape
    T = B * S
    pos = jnp.arange(S)
    n_mix = sum(IN_SPLITS[:-1])
    mix_tile = -(-n_mix // MXU_COLS) * MXU_COLS
    n_mix_pad = mix_tile
    split_points = np.cumsum(IN_SPLITS[:-1])[:-1].tolist()
    bf16 = jnp.bfloat16

    def heads(t, n):
        return t.reshape(B, S, n, HEAD_DIM)

    x = x.reshape(T, D)
    for l in range(DEPTH):
        w_mix = jnp.pad(w_in[l][:, :n_mix], ((0, 0), (0, n_mix_pad - n_mix))).astype(bf16)
        proj = norm_matmul(x, norm1_g[l], w_mix, mix_tile)[:, :n_mix].reshape(B, S, n_mix)
        mg = norm_matmul(x, norm1_g[l], w_in[l][:, n_mix:].astype(bf16), 2 * D)
        (fq, fk, fv, ff, lx, lg, dq, dk, dv, nq, nkc, nvc, nks, nvs, nkw, nvw, ng) = \
            jnp.split(proj, split_points, axis=-1)

        cum = jnp.cumsum(jax.nn.log_sigmoid(ff + fox_b_f[l]), axis=1)
        o_fox = fox_attention_pallas(fq, fk.astype(jnp.bfloat16), fv.astype(jnp.bfloat16), cum)
        o_lru = rg_lru_pallas(lx, lg, lru_conv_w[l], lru_conv_b[l], lru_w_a[l], lru_b_a[l],
                              lru_w_x[l], lru_b_x[l], lru_lambda[l])
        o_dil = dilated_attention_pallas(
            rope(heads(dq, DIL_HEADS), pos).reshape(B, S, DIL_W),
            rope(heads(dk, DIL_HEADS), pos).reshape(B, S, DIL_W).astype(jnp.bfloat16),
            dv.astype(jnp.bfloat16))
        o_nsa = nsa_branch(nq, nkc, nvc, nks, nvs, nkw, nvw, ng,
                           nsa_pe_k[l], nsa_pe_v[l], nsa_w1_k[l], nsa_w2_k[l],
                           nsa_w1_v[l], nsa_w2_v[l])

        x, h2 = merge_out(x, mg, o_fox.reshape(T, FOX_W), o_lru.reshape(T, LRU_WIDTH),
                          o_dil.reshape(T, DIL_OUT_W), o_nsa.reshape(T, NSA_W),
                          w_br_fox[l], w_br_lru[l], w_br_dil[l], w_br_nsa[l], w_out[l], norm2_g[l])
        x = peer_ffn(x, h2, peer_w_q[l], peer_sub_keys[l], peer_u[l], peer_v[l])

    return rms_norm_pallas(x, final_g).reshape(B, S, D)
```

```python
import functools

import jax
import jax.numpy as jnp
import numpy as np
from jax import lax
from jax.experimental import pallas as pl
from jax.experimental.pallas import tpu as pltpu

LANES = 128
SUBLANES = 8
MXU_COLS = 256
V7X_VMEM_BYTES = 64 << 20
VMEM_LIMIT_BYTES = V7X_VMEM_BYTES * 3 // 4

D_MODEL = 1024
DEPTH = 2
HEAD_DIM = 64
Q_BLOCK = 128
ROPE_THETA = 10000.0
NORM_EPS = 1e-6
NEG_INF = -1e30
N_BRANCHES = 4
FOX_HEADS = 4
FOX_W = FOX_HEADS * HEAD_DIM
LRU_WIDTH = 256
LRU_CONV = 4
LRU_C = 8.0
DIL_PAIRS = ((128, 1), (512, 4), (2048, 16))
DIL_GROUPS = len(DIL_PAIRS)
DIL_HEADS_PER_GROUP = 2
DIL_HEADS = DIL_GROUPS * DIL_HEADS_PER_GROUP
DIL_W = DIL_HEADS * HEAD_DIM
DIL_OUT_W = DIL_HEADS_PER_GROUP * HEAD_DIM
NSA_HEADS = 4
NSA_KV_HEADS = 1
NSA_W = NSA_HEADS * HEAD_DIM
NSA_KV_W = NSA_KV_HEADS * HEAD_DIM
NSA_CMP_BLOCK = 32
NSA_CMP_STRIDE = 16
NSA_SLC_BLOCK = 64
NSA_TOP_N = 16
NSA_WINDOW = 512
NSA_FORCE_SCORE = 1e6
PEER_HEADS = 8
PEER_N_KEYS = 128
PEER_TOPK = 16
PEER_D_KEY = 256
PEER_BLOCK = 128

IN_SPLITS = (FOX_W, FOX_W, FOX_W, FOX_HEADS,
             LRU_WIDTH, LRU_WIDTH,
             DIL_W, DIL_W, DIL_W,
             NSA_W, NSA_KV_W, NSA_KV_W, NSA_KV_W, NSA_KV_W, NSA_KV_W, NSA_KV_W, NSA_HEADS * 3,
             N_BRANCHES * D_MODEL)


def _rms_norm_kernel(x_ref, g_ref, o_ref):
    x = x_ref[...]
    ms = jnp.mean(x * x, axis=-1, keepdims=True)
    o_ref[...] = x * lax.rsqrt(ms + NORM_EPS) * g_ref[...]


def rms_norm_pallas(x, g):
    t, d = x.shape
    rows = DENSE_ROWS
    return pl.pallas_call(
        _rms_norm_kernel,
        grid=(t // rows,),
        in_specs=[pl.BlockSpec((rows, d), lambda i: (i, 0)),
                  pl.BlockSpec((1, d), lambda i: (0, 0))],
        out_specs=pl.BlockSpec((rows, d), lambda i: (i, 0)),
        out_shape=jax.ShapeDtypeStruct((t, d), x.dtype),
        name="rms_norm",
    )(x, g.reshape(1, d))


def _gelu_tanh(x):
    return 0.5 * x * (1.0 + jnp.tanh(0.7978845608028654 * (x + 0.044715 * x * x * x)))


PEER_TOKENS_PER_STEP = 8


def _peer_mix_kernel(idx_hbm, x_ref, gt_ref, res_ref, uv_hbm, o_ref,
                     idx_smem, buf, idx_sem, row_sem, *, tb, n_sel, d):
    i = pl.program_id(0)
    n = pl.num_programs(0)
    rows = tb * n_sel
    n_chunks = d // LANES
    groups = n_sel // SUBLANES

    def idx_copy(step, s):
        return pltpu.make_async_copy(idx_hbm.at[step], idx_smem.at[pl.ds(s * rows, rows)],
                                     idx_sem.at[s])

    def row_copy(s, j, row):
        return pltpu.make_async_copy(uv_hbm.at[row], buf.at[s, j // SUBLANES, pl.ds(j % SUBLANES, 1)],
                                     row_sem.at[s])

    def slot_wait(s):
        pltpu.make_async_copy(buf.at[s], buf.at[s], row_sem.at[s]).wait()

    @pl.when(i == 0)
    def _():
        idx_copy(0, 0).start()
        idx_copy(0, 0).wait()

        def body(jj, c):
            for k in range(SUBLANES):
                row = idx_smem[jj * SUBLANES + k]
                pltpu.make_async_copy(uv_hbm.at[row], buf.at[0, jj, pl.ds(k, 1)],
                                      row_sem.at[0]).start(priority=k % 2)
            return c
        lax.fori_loop(0, rows // SUBLANES, body, 0)
        idx_copy(1, 1).start()

    def step(cur):
        nxt = 1 - cur
        idx_copy(i + 1, nxt).wait()

        @pl.when(i + 2 <= n)
        def _():
            idx_copy(i + 2, cur).start()

        slot_wait(cur)

        def issue(t, part):
            lo = part * groups // (2 * n_chunks)
            hi = (part + 1) * groups // (2 * n_chunks)
            for j in range(t * n_sel + lo * SUBLANES, t * n_sel + hi * SUBLANES):
                row_copy(nxt, j, idx_smem[nxt * rows + j]).start(priority=j % 2)

        def sel_rows(t, lane0):
            tile = buf[cur, pl.ds(t * groups, groups), :, pl.ds(lane0, LANES)]
            return tile.reshape(n_sel, LANES)

        for t in range(tb):
            acc = jnp.zeros((n_sel, LANES), jnp.float32)
            for c in range(n_chunks):
                issue(t, c)
                xc = x_ref[pl.ds(t, 1), pl.ds(c * LANES, LANES)]
                acc = acc + sel_rows(t, c * LANES) * xc
            s = jnp.sum(acc, axis=-1, keepdims=True)
            w = gt_ref[0, :, pl.ds(t, 1)] * _gelu_tanh(s)
            wb = jnp.broadcast_to(w, (n_sel, LANES))
            for c in range(n_chunks):
                issue(t, n_chunks + c)
                vc = sel_rows(t, d + c * LANES)
                oc = jnp.sum(wb * vc, axis=0, keepdims=True)
                o_ref[pl.ds(t, 1), pl.ds(c * LANES, LANES)] = (
                    res_ref[pl.ds(t, 1), pl.ds(c * LANES, LANES)] + oc)

        @pl.when(i == n - 1)
        def _():
            slot_wait(nxt)

    @pl.when(i % 2 == 0)
    def _():
        step(0)

    @pl.when(i % 2 == 1)
    def _():
        step(1)


def peer_mix(x, res, idx, gt_blocks, uv):
    t, d = x.shape
    n_sel = idx.shape[1]
    tb = PEER_TOKENS_PER_STEP
    steps = t // tb
    spare = (jnp.arange(tb * n_sel, dtype=jnp.int32) % uv.shape[0]).reshape(1, tb * n_sel)
    idx_steps = jnp.concatenate([idx.reshape(steps, tb * n_sel), spare], axis=0)
    nb, _, tq = gt_blocks.shape
    gt = jnp.transpose(gt_blocks.reshape(nb, n_sel, tq // tb, tb), (0, 2, 1, 3)).reshape(steps, n_sel, tb)
    kern = functools.partial(_peer_mix_kernel, tb=tb, n_sel=n_sel, d=d)
    return pl.pallas_call(
        kern,
        grid=(steps,),
        in_specs=[pl.BlockSpec(memory_space=pl.ANY),
                  pl.BlockSpec((tb, d), lambda i: (i, 0)),
                  pl.BlockSpec((1, n_sel, tb), lambda i: (i, 0, 0)),
                  pl.BlockSpec((tb, d), lambda i: (i, 0)),
                  pl.BlockSpec(memory_space=pl.ANY)],
        out_specs=pl.BlockSpec((tb, d), lambda i: (i, 0)),
        out_shape=jax.ShapeDtypeStruct((t, d), jnp.float32),
        scratch_shapes=[pltpu.SMEM((2 * tb * n_sel,), jnp.int32),
                        pltpu.VMEM((2, tb * n_sel // SUBLANES, SUBLANES, 2 * d), jnp.float32),
                        pltpu.SemaphoreType.DMA((2,)),
                        pltpu.SemaphoreType.DMA((2,))],
        compiler_params=pltpu.CompilerParams(
            dimension_semantics=("arbitrary",),
            vmem_limit_bytes=VMEM_LIMIT_BYTES),
        name="peer_mix",
    )(idx_steps, x, gt, res, uv.reshape(uv.shape[0], 1, 2 * d))


def _topk_rows(work, k):
    n = work.shape[0]
    row = lax.broadcasted_iota(jnp.int32, work.shape, 0)
    vals, idxs = [], []
    for _ in range(k):
        mx = jnp.max(work, axis=0, keepdims=True)
        pos = jnp.min(jnp.where(work == mx, row, n), axis=0, keepdims=True)
        work = jnp.where(row == pos, -jnp.inf, work)
        vals.append(mx)
        idxs.append(pos)
    return jnp.concatenate(vals, axis=0), jnp.concatenate(idxs, axis=0)


def _peer_route_kernel(x_ref, wq_ref, keys_ref, idx_ref, gt_ref, q_s, idx_s, g_s, *, tq):
    bf16 = jnp.bfloat16
    k = PEER_TOPK
    half = PEER_D_KEY // 2
    q = jnp.dot(x_ref[...].astype(bf16), wq_ref[...], preferred_element_type=jnp.float32)
    for c in range(2 * PEER_HEADS):
        q_s[c] = q[:, c * half:(c + 1) * half].astype(bf16)

    def head_body(h, carry):
        tops = []
        for p in range(2):
            scores = _nt_dot(keys_ref[p], q_s[2 * h + p])
            tops.append(_topk_rows(scores, k))
        (v0, i0), (v1, i1) = tops
        keep = [k // (a + 1) for a in range(k)]
        n_cand = sum(keep)
        n_pad = -n_cand % SUBLANES
        cand = jnp.concatenate([v0[a:a + 1, :] + v1[:keep[a]] for a in range(k)]
                               + [jnp.full((n_pad, tq), -jnp.inf, jnp.float32)], axis=0)
        cand_idx = jnp.concatenate([i0[a:a + 1, :] * PEER_N_KEYS + i1[:keep[a]] for a in range(k)]
                                   + [jnp.full((n_pad, tq), -1, jnp.int32)], axis=0)
        flat = lax.broadcasted_iota(jnp.int32, cand.shape, 0)
        top_s, experts = [], []
        for _ in range(k):
            mx = jnp.max(cand, axis=0, keepdims=True)
            pos = jnp.min(jnp.where(cand == mx, flat, n_cand + n_pad), axis=0, keepdims=True)
            pick = flat == pos
            experts.append(jnp.max(jnp.where(pick, cand_idx, -1), axis=0, keepdims=True))
            cand = jnp.where(pick, -jnp.inf, cand)
            top_s.append(mx)
        top_s = jnp.concatenate(top_s, axis=0)
        ex = jnp.exp(top_s - jnp.max(top_s, axis=0, keepdims=True))
        r0 = pl.multiple_of(h * k, k)
        g_s[pl.ds(r0, k), :] = ex / jnp.sum(ex, axis=0, keepdims=True)
        idx_s[pl.ds(r0, k), :] = jnp.concatenate(experts, axis=0)
        return carry
    lax.fori_loop(0, PEER_HEADS, head_body, 0)

    idx_ref[...] = jnp.transpose(idx_s[...])
    gt_ref[0] = g_s[...]


def peer_route(h2, w_q, sub_keys):
    t, d = h2.shape
    tq = PEER_BLOCK
    n_sel = PEER_HEADS * PEER_TOPK
    half = PEER_D_KEY // 2
    return pl.pallas_call(
        functools.partial(_peer_route_kernel, tq=tq),
        grid=(t // tq,),
        in_specs=[pl.BlockSpec((tq, d), lambda i: (i, 0)),
                  pl.BlockSpec((d, PEER_HEADS * PEER_D_KEY), lambda i: (0, 0)),
                  pl.BlockSpec((2, PEER_N_KEYS, half), lambda i: (0, 0, 0))],
        out_specs=[pl.BlockSpec((tq, n_sel), lambda i: (i, 0)),
                   pl.BlockSpec((1, n_sel, tq), lambda i: (i, 0, 0))],
        out_shape=[jax.ShapeDtypeStruct((t, n_sel), jnp.int32),
                   jax.ShapeDtypeStruct((t // tq, n_sel, tq), jnp.float32)],
        scratch_shapes=[pltpu.VMEM((2 * PEER_HEADS, tq, half), jnp.bfloat16),
                        pltpu.VMEM((n_sel, tq), jnp.int32),
                        pltpu.VMEM((n_sel, tq), jnp.float32)],
        compiler_params=pltpu.CompilerParams(dimension_semantics=("arbitrary",),
                                             vmem_limit_bytes=VMEM_LIMIT_BYTES),
        name="peer_route",
    )(h2, w_q.astype(jnp.bfloat16), sub_keys.astype(jnp.bfloat16))


NSA_KV_TILE = 512


def _nt_dot(a, b):
    return lax.dot_general(a, b, (((1,), (1,)), ((), ())), preferred_element_type=jnp.float32)


def _nsa_kernel(qn_ref, qr_ref, kc_ref, vc_ref, ov_ref, ks_ref, vs_ref, kw_ref, vw_ref, gl_ref,
                o_ref, m_s, l_s, acc_s, out_s, la_s, lb_s, *, n_heads, qb, n_cmp_pad, n_slc):
    j = pl.program_id(1)
    s0 = j * qb
    hq = n_heads * qb
    bf16 = jnp.bfloat16
    scale = HEAD_DIM ** -0.5
    q_pos = s0 + lax.broadcasted_iota(jnp.int32, (qb, 1), 0)

    gates = jax.nn.sigmoid(gl_ref[0])

    def gate_col(br):
        return jnp.concatenate([gates[:, 3 * h + br:3 * h + br + 1] for h in range(n_heads)], axis=0)

    def masked_update(logits, mask, v_tile):
        kt = logits.shape[-1]
        lg = jnp.where(mask[None], logits.reshape(n_heads, qb, kt), NEG_INF)
        m_old = m_s[...].reshape(n_heads, qb, 1)
        m_new = jnp.maximum(m_old, jnp.max(lg, axis=-1, keepdims=True))
        p = jnp.where(mask[None], jnp.exp(lg - m_new), 0.0)
        alpha = jnp.exp(m_old - m_new)
        l_s[...] = (alpha * l_s[...].reshape(n_heads, qb, 1)
                    + jnp.sum(p, axis=-1, keepdims=True)).reshape(hq, 1)
        pv = jnp.dot(p.reshape(hq, kt).astype(bf16), v_tile, preferred_element_type=jnp.float32)
        acc_s[...] = alpha.reshape(hq, 1) * acc_s[...] + pv
        m_s[...] = m_new.reshape(hq, 1)

    def reset_state():
        m_s[...] = jnp.full((hq, 1), NEG_INF, jnp.float32)
        l_s[...] = jnp.zeros((hq, 1), jnp.float32)
        acc_s[...] = jnp.zeros((hq, HEAD_DIM), jnp.float32)

    def normalized():
        return acc_s[...] / jnp.maximum(l_s[...], 1e-30)

    qn = (qn_ref[0].reshape(hq, HEAD_DIM) * scale).astype(bf16)
    lc = _nt_dot(qn, kc_ref[0]).reshape(n_heads, qb, n_cmp_pad)
    c_end = (lax.broadcasted_iota(jnp.int32, (qb, n_cmp_pad), 1) * NSA_CMP_STRIDE
             + (NSA_CMP_BLOCK - 1))
    c_mask = c_end <= q_pos
    lc = jnp.where(c_mask[None], lc, NEG_INF)
    mc = jnp.max(lc, axis=-1, keepdims=True)
    ec = jnp.where(c_mask[None], jnp.exp(lc - mc), 0.0)
    pc = ec / jnp.maximum(jnp.sum(ec, axis=-1, keepdims=True), 1e-30)
    o_c = jnp.dot(pc.reshape(hq, n_cmp_pad).astype(bf16), vc_ref[0],
                  preferred_element_type=jnp.float32)
    out_s[...] = gate_col(0) * o_c

    p_sum = jnp.sum(pc, axis=0)
    p_hi = p_sum.astype(bf16)
    p_lo = (p_sum - p_hi.astype(jnp.float32)).astype(bf16)
    imp = (jnp.dot(p_hi, ov_ref[...], preferred_element_type=jnp.float32)
           + jnp.dot(p_lo, ov_ref[...], preferred_element_type=jnp.float32))
    blk = lax.broadcasted_iota(jnp.int32, (qb, n_slc), 1)
    cur = q_pos // NSA_SLC_BLOCK
    forced = (blk == 0) | (blk == cur) | (blk == cur - 1)
    imp = jnp.where(forced, NSA_FORCE_SCORE, imp)
    valid = blk <= cur
    work = jnp.transpose(jnp.where(valid, imp, NEG_INF))
    blk_t = lax.broadcasted_iota(jnp.int32, (n_slc, qb), 0)
    sel_t = jnp.zeros((n_slc, qb), jnp.float32)
    for _ in range(min(NSA_TOP_N, n_slc)):
        mx = jnp.max(work, axis=0, keepdims=True)
        first = jnp.min(jnp.where(work == mx, blk_t, n_slc), axis=0, keepdims=True)
        pick = blk_t == first
        sel_t = jnp.where(pick, 1.0, sel_t)
        work = jnp.where(pick, -jnp.inf, work)
    sel_bf = jnp.where(valid, jnp.transpose(sel_t), 0.0).astype(bf16)

    qr = (qr_ref[0].reshape(hq, HEAD_DIM) * scale).astype(bf16)
    reset_state()
    w_len = NSA_WINDOW + qb
    w0 = pl.multiple_of(jnp.maximum(s0 - NSA_WINDOW, 0), qb)
    dist = q_pos - (w0 + lax.broadcasted_iota(jnp.int32, (qb, w_len), 1))
    masked_update(_nt_dot(qr, kw_ref[0, pl.ds(w0, w_len), :]), (dist >= 0) & (dist < NSA_WINDOW),
                  vw_ref[0, pl.ds(w0, w_len), :])
    out_s[...] += gate_col(2) * normalized()

    reset_state()
    kt = NSA_KV_TILE
    blocks_per_tile = kt // NSA_SLC_BLOCK
    n_tiles = (s0 + qb - 1) // kt + 1

    def tile_start(t):
        return pl.multiple_of(jnp.minimum(t, n_tiles - 1) * kt, kt)

    def scores(t):
        return _nt_dot(qr, ks_ref[0, pl.ds(tile_start(t), kt), :])

    def consume(logits, t):
        k_idx = lax.broadcasted_iota(jnp.int32, (n_slc, kt), 1)
        expand = (lax.broadcasted_iota(jnp.int32, (n_slc, kt), 0)
                  == t * blocks_per_tile + k_idx // NSA_SLC_BLOCK)
        picked = jnp.dot(sel_bf, jnp.where(expand, 1.0, 0.0).astype(bf16),
                         preferred_element_type=jnp.float32)
        k_pos = t * kt + lax.broadcasted_iota(jnp.int32, (qb, kt), 1)
        mask = (picked > 0.5) & (k_pos <= q_pos)
        masked_update(logits, mask, vs_ref[0, pl.ds(tile_start(t), kt), :])

    la_s[...] = scores(0)

    def sel_body(u, c):
        t = 2 * u
        lb_s[...] = scores(t + 1)
        consume(la_s[...], t)
        la_s[...] = scores(t + 2)
        consume(lb_s[...], t + 1)
        return c
    lax.fori_loop(0, (n_tiles + 1) // 2, sel_body, 0)
    out_s[...] += gate_col(1) * normalized()

    o_ref[0] = out_s[...].reshape(n_heads, qb, HEAD_DIM)


def _nsa_compress_kernel(x_ref, pe_ref, w1_ref, w2_ref, o_ref, *, n_cmp):
    bf16 = jnp.bfloat16
    x = x_ref[0]
    n_chunks, half = x.shape
    head = jnp.dot((x + pe_ref[0:1, :]).astype(bf16), w1_ref[pl.ds(0, half), :],
                   preferred_element_type=jnp.float32)
    tail = jnp.dot((x + pe_ref[1:2, :]).astype(bf16), w1_ref[pl.ds(half, half), :],
                   preferred_element_type=jnp.float32)
    hidden = jax.nn.gelu(head + pltpu.roll(tail, n_chunks - 1, 0))
    out = jnp.dot(hidden.astype(bf16), w2_ref[...], preferred_element_type=jnp.float32)
    row = lax.broadcasted_iota(jnp.int32, out.shape, 0)
    o_ref[0] = jnp.where(row < n_cmp, out, 0.0).astype(o_ref.dtype)


def nsa_compress(kv, pe, w1, w2):
    b, s, dh = kv.shape
    n_chunks = s // NSA_CMP_STRIDE
    n_cmp = (s - NSA_CMP_BLOCK) // NSA_CMP_STRIDE + 1
    half = NSA_CMP_STRIDE * dh
    hidden = w1.shape[1]
    return pl.pallas_call(
        functools.partial(_nsa_compress_kernel, n_cmp=n_cmp),
        grid=(b,),
        in_specs=[pl.BlockSpec((1, n_chunks, half), lambda i: (i, 0, 0)),
                  pl.BlockSpec((2, half), lambda i: (0, 0)),
                  pl.BlockSpec((2 * half, hidden), lambda i: (0, 0)),
                  pl.BlockSpec((hidden, dh), lambda i: (0, 0))],
        out_specs=pl.BlockSpec((1, n_chunks, dh), lambda i: (i, 0, 0)),
        out_shape=jax.ShapeDtypeStruct((b, n_chunks, dh), jnp.bfloat16),
        compiler_params=pltpu.CompilerParams(dimension_semantics=("arbitrary",)),
        name="nsa_compress",
    )(kv.reshape(b, n_chunks, half), pe.reshape(2, half), w1.astype(jnp.bfloat16),
      w2.astype(jnp.bfloat16))


def nsa_attention_pallas(qn, qr, kc, vc, overlap, ks, vs, kw, vw, gl):
    b, h, s, dh = qn.shape
    qb = Q_BLOCK
    n_cmp_pad = kc.shape[1]
    n_slc = overlap.shape[1]
    kern = functools.partial(_nsa_kernel, n_heads=h, qb=qb, n_cmp_pad=n_cmp_pad, n_slc=n_slc)
    q_spec = pl.BlockSpec((1, h, qb, dh), lambda i, j: (i, 0, j, 0))
    cmp_spec = pl.BlockSpec((1, n_cmp_pad, dh), lambda i, j: (i, 0, 0))
    kv_spec = pl.BlockSpec((1, s, dh), lambda i, j: (i, 0, 0))
    return pl.pallas_call(
        kern,
        grid=(b, s // qb),
        in_specs=[q_spec, q_spec, cmp_spec, cmp_spec,
                  pl.BlockSpec((n_cmp_pad, n_slc), lambda i, j: (0, 0)),
                  kv_spec, kv_spec, kv_spec, kv_spec,
                  pl.BlockSpec((1, qb, 3 * h), lambda i, j: (i, j, 0))],
        out_specs=q_spec,
        out_shape=jax.ShapeDtypeStruct((b, h, s, dh), jnp.float32),
        scratch_shapes=[pltpu.VMEM((h * qb, 1), jnp.float32),
                        pltpu.VMEM((h * qb, 1), jnp.float32),
                        pltpu.VMEM((h * qb, dh), jnp.float32),
                        pltpu.VMEM((h * qb, dh), jnp.float32),
                        pltpu.VMEM((h * qb, NSA_KV_TILE), jnp.float32),
                        pltpu.VMEM((h * qb, NSA_KV_TILE), jnp.float32)],
        compiler_params=pltpu.CompilerParams(
            dimension_semantics=("arbitrary", "arbitrary"),
            vmem_limit_bytes=VMEM_LIMIT_BYTES),
        name="nsa_attention",
    )(qn, qr, kc, vc, overlap, ks, vs, kw, vw, gl)


def _dilated_kernel(q_ref, k_ref, v_ref, o_ref, *, qb):
    j = pl.program_id(1)
    s0 = j * qb
    bf16 = jnp.bfloat16
    scale = HEAD_DIM ** -0.5
    q_pos = s0 + lax.broadcasted_iota(jnp.int32, (qb, 1), 0)
    lane = lax.broadcasted_iota(jnp.int32, (qb, LANES), 1)
    first_head = lane < HEAD_DIM
    rows = DIL_HEADS_PER_GROUP * qb

    outs, lses = [], []
    for g, (window, dil) in enumerate(DIL_PAIRS):
        c0 = g * LANES
        q = q_ref[0, :, pl.ds(c0, LANES)] * scale
        q2 = jnp.concatenate([jnp.where(first_head, q, 0.0),
                              jnp.where(first_head, 0.0, q)], axis=0).astype(bf16)
        span = window + qb
        k0 = pl.multiple_of(jnp.maximum(s0 - window, 0), qb)
        dist = q_pos - (k0 + lax.broadcasted_iota(jnp.int32, (qb, span), 1))
        mask = (dist >= 0) & (dist <= window) & ((dist & (dil - 1)) == 0)
        logits = _nt_dot(q2, k_ref[0, pl.ds(k0, span), pl.ds(c0, LANES)])
        lg = jnp.where(mask[None], logits.reshape(DIL_HEADS_PER_GROUP, qb, span), NEG_INF)
        m = jnp.max(lg, axis=-1, keepdims=True)
        e = jnp.exp(lg - m)
        l = jnp.sum(e, axis=-1, keepdims=True).reshape(rows, 1)
        pv = jnp.dot(e.reshape(rows, span).astype(bf16), v_ref[0, pl.ds(k0, span), pl.ds(c0, LANES)],
                     preferred_element_type=jnp.float32)
        o2 = pv / l
        lse2 = m.reshape(rows, 1) + jnp.log(l)
        outs.append(jnp.where(first_head, o2[:qb], o2[qb:]))
        lses.append(jnp.where(first_head, lse2[:qb], lse2[qb:]))

    lse_max = functools.reduce(jnp.maximum, lses)
    ws = [jnp.exp(x - lse_max) for x in lses]
    den = functools.reduce(lambda a, b: a + b, ws)
    num = functools.reduce(lambda a, b: a + b, [w * o for w, o in zip(ws, outs)])
    o_ref[0] = num / den


def dilated_attention_pallas(q, k, v):
    b, s, w = q.shape
    qb = Q_BLOCK
    return pl.pallas_call(
        functools.partial(_dilated_kernel, qb=qb),
        grid=(b, s // qb),
        in_specs=[pl.BlockSpec((1, qb, w), lambda i, j: (i, j, 0)),
                  pl.BlockSpec((1, s, w), lambda i, j: (i, 0, 0)),
                  pl.BlockSpec((1, s, w), lambda i, j: (i, 0, 0))],
        out_specs=pl.BlockSpec((1, qb, DIL_OUT_W), lambda i, j: (i, j, 0)),
        out_shape=jax.ShapeDtypeStruct((b, s, DIL_OUT_W), jnp.float32),
        compiler_params=pltpu.CompilerParams(
            dimension_semantics=("arbitrary", "arbitrary"),
            vmem_limit_bytes=VMEM_LIMIT_BYTES),
        name="dilated_attention",
    )(q, k, v)


FOX_KV_TILE = 512


def _fox_kernel(q_ref, k_ref, v_ref, cq_ref, ck_ref, o_ref, m_s, l_s, acc_s, la_s, lb_s, *, qb):
    j = pl.program_id(1)
    s0 = j * qb
    kt = FOX_KV_TILE
    bf16 = jnp.bfloat16
    scale = HEAD_DIM ** -0.5
    q_pos = s0 + lax.broadcasted_iota(jnp.int32, (qb, 1), 0)
    lane = lax.broadcasted_iota(jnp.int32, (qb, LANES), 1)
    first_head = lane < HEAD_DIM
    n_pairs = FOX_HEADS // 2
    rows = FOX_HEADS * qb
    n_tiles = (s0 + qb - 1) // kt + 1

    q2 = []
    for pair in range(n_pairs):
        q = q_ref[0, :, pl.ds(pair * LANES, LANES)] * scale
        q2.append(jnp.concatenate([jnp.where(first_head, q, 0.0),
                                   jnp.where(first_head, 0.0, q)], axis=0).astype(bf16))
    cq = [cq_ref[0, :, h:h + 1] for h in range(FOX_HEADS)]
    m_s[...] = jnp.full((rows, 1), NEG_INF, jnp.float32)
    l_s[...] = jnp.zeros((rows, 1), jnp.float32)
    acc_s[...] = jnp.zeros((rows, LANES), jnp.float32)

    def tile_start(t):
        return pl.multiple_of(jnp.minimum(t, n_tiles - 1) * kt, kt)

    def biased_logits(t):
        k0 = tile_start(t)
        logits = jnp.concatenate(
            [_nt_dot(q2[pair], k_ref[0, pl.ds(k0, kt), pl.ds(pair * LANES, LANES)])
             for pair in range(n_pairs)], axis=0)
        bias = jnp.concatenate([cq[h] - ck_ref[0, h:h + 1, pl.ds(k0, kt)] for h in range(FOX_HEADS)],
                               axis=0)
        return logits + bias

    def consume(logits, t):
        k0 = tile_start(t)
        mask = t * kt + lax.broadcasted_iota(jnp.int32, (qb, kt), 1) <= q_pos
        lg = jnp.where(mask[None], logits.reshape(FOX_HEADS, qb, kt), NEG_INF)
        m_old = m_s[...].reshape(FOX_HEADS, qb, 1)
        m_new = jnp.maximum(m_old, jnp.max(lg, axis=-1, keepdims=True))
        p = jnp.exp(lg - m_new)
        alpha = jnp.exp(m_old - m_new)
        l_s[...] = (alpha * l_s[...].reshape(FOX_HEADS, qb, 1)
                    + jnp.sum(p, axis=-1, keepdims=True)).reshape(rows, 1)
        pb = p.reshape(rows, kt).astype(bf16)
        pv = jnp.concatenate(
            [jnp.dot(pb[2 * pair * qb:2 * (pair + 1) * qb],
                     v_ref[0, pl.ds(k0, kt), pl.ds(pair * LANES, LANES)],
                     preferred_element_type=jnp.float32) for pair in range(n_pairs)], axis=0)
        acc_s[...] = alpha.reshape(rows, 1) * acc_s[...] + pv
        m_s[...] = m_new.reshape(rows, 1)

    la_s[...] = biased_logits(0)

    def body(u, c):
        t = 2 * u
        lb_s[...] = biased_logits(t + 1)
        consume(la_s[...], t)
        la_s[...] = biased_logits(t + 2)
        consume(lb_s[...], t + 1)
        return c
    lax.fori_loop(0, (n_tiles + 1) // 2, body, 0)

    o2 = acc_s[...] / l_s[...]
    for pair in range(n_pairs):
        r0 = 2 * pair * qb
        o_ref[0, :, pl.ds(pair * LANES, LANES)] = jnp.where(first_head, o2[r0:r0 + qb],
                                                             o2[r0 + qb:r0 + 2 * qb])


def fox_attention_pallas(q, k, v, cum):
    b, s, w = q.shape
    qb = Q_BLOCK
    return pl.pallas_call(
        functools.partial(_fox_kernel, qb=qb),
        grid=(b, s // qb),
        in_specs=[pl.BlockSpec((1, qb, w), lambda i, j: (i, j, 0)),
                  pl.BlockSpec((1, s, w), lambda i, j: (i, 0, 0)),
                  pl.BlockSpec((1, s, w), lambda i, j: (i, 0, 0)),
                  pl.BlockSpec((1, qb, FOX_HEADS), lambda i, j: (i, j, 0)),
                  pl.BlockSpec((1, FOX_HEADS, s), lambda i, j: (i, 0, 0))],
        out_specs=pl.BlockSpec((1, qb, w), lambda i, j: (i, j, 0)),
        out_shape=jax.ShapeDtypeStruct((b, s, w), jnp.float32),
        scratch_shapes=[pltpu.VMEM((FOX_HEADS * qb, 1), jnp.float32),
                        pltpu.VMEM((FOX_HEADS * qb, 1), jnp.float32),
                        pltpu.VMEM((FOX_HEADS * qb, LANES), jnp.float32),
                        pltpu.VMEM((FOX_HEADS * qb, FOX_KV_TILE), jnp.float32),
                        pltpu.VMEM((FOX_HEADS * qb, FOX_KV_TILE), jnp.float32)],
        compiler_params=pltpu.CompilerParams(
            dimension_semantics=("arbitrary", "arbitrary"),
            vmem_limit_bytes=VMEM_LIMIT_BYTES),
        name="fox_attention",
    )(q, k, v, cum, jnp.transpose(cum, (0, 2, 1)))


LRU_TILE = 512


def _lru_kernel(x_ref, gate_ref, cw_ref, cb_ref, wa_ref, ba_ref, wx_ref, bx_ref, lam_ref, o_ref,
                xbuf, a_s, u_s, h_s, hlast, *, tt):
    j = pl.program_id(1)
    c = x_ref.shape[-1]
    bf16 = jnp.bfloat16

    @pl.when(j == 0)
    def _():
        xbuf[pl.ds(0, SUBLANES), :] = jnp.zeros((SUBLANES, c), jnp.float32)
        hlast[...] = jnp.zeros((1, c), jnp.float32)

    x = x_ref[0]
    xbuf[pl.ds(SUBLANES, tt), :] = x
    xc = cb_ref[...] + cw_ref[LRU_CONV - 1:LRU_CONV, :] * x
    for back in range(1, LRU_CONV):
        xc = xc + cw_ref[LRU_CONV - 1 - back:LRU_CONV - back, :] * xbuf[pl.ds(SUBLANES - back, tt), :]
    xbuf[pl.ds(0, SUBLANES), :] = x[tt - SUBLANES:, :]

    xb = xc.astype(bf16)
    r = jax.nn.sigmoid(jnp.dot(xb, wa_ref[...], preferred_element_type=jnp.float32) + ba_ref[...])
    i_g = jax.nn.sigmoid(jnp.dot(xb, wx_ref[...], preferred_element_type=jnp.float32) + bx_ref[...])
    lam = lam_ref[...]
    softplus_neg = jnp.maximum(-lam, 0.0) + jnp.log1p(jnp.exp(-jnp.abs(lam)))
    log_a = -LRU_C * r * softplus_neg
    a_s[...] = jnp.exp(log_a)
    u_s[...] = jnp.sqrt(1.0 - jnp.exp(2.0 * log_a)) * (i_g * xc)

    def group(gi, h):
        r0 = pl.multiple_of(gi * SUBLANES, SUBLANES)
        a8 = a_s[pl.ds(r0, SUBLANES), :]
        u8 = u_s[pl.ds(r0, SUBLANES), :]
        rows = []
        for s in range(SUBLANES):
            h = a8[s:s + 1, :] * h + u8[s:s + 1, :]
            rows.append(h)
        h_s[pl.ds(r0, SUBLANES), :] = jnp.concatenate(rows, axis=0)
        return h
    hlast[...] = lax.fori_loop(0, tt // SUBLANES, group, hlast[...])
    o_ref[0] = h_s[...] * jax.nn.gelu(gate_ref[0])


def _block_diag(w):
    n, c, d = w.shape
    eye = jnp.eye(n, dtype=w.dtype)
    return (eye[:, None, :, None] * w[:, :, None, :]).reshape(n * c, n * d)


def rg_lru_pallas(x_in, gate, conv_w, conv_b, w_a, b_a, w_x, b_x, lam):
    b, s, c = x_in.shape
    tt = LRU_TILE
    row = lambda v: v.reshape(1, c)
    tok_spec = pl.BlockSpec((1, tt, c), lambda i, j: (i, j, 0))
    full = lambda shape: pl.BlockSpec(shape, lambda i, j: (0,) * len(shape))
    return pl.pallas_call(
        functools.partial(_lru_kernel, tt=tt),
        grid=(b, s // tt),
        in_specs=[tok_spec, tok_spec, full((LRU_CONV, c)), full((1, c)), full((c, c)), full((1, c)),
                  full((c, c)), full((1, c)), full((1, c))],
        out_specs=tok_spec,
        out_shape=jax.ShapeDtypeStruct((b, s, c), jnp.float32),
        scratch_shapes=[pltpu.VMEM((tt + SUBLANES, c), jnp.float32),
                        pltpu.VMEM((tt, c), jnp.float32),
                        pltpu.VMEM((tt, c), jnp.float32),
                        pltpu.VMEM((tt, c), jnp.float32),
                        pltpu.VMEM((1, c), jnp.float32)],
        compiler_params=pltpu.CompilerParams(dimension_semantics=("arbitrary", "arbitrary")),
        name="rg_lru",
    )(x_in, gate, conv_w, row(conv_b), _block_diag(w_a).astype(jnp.bfloat16), row(b_a),
      _block_diag(w_x).astype(jnp.bfloat16), row(b_x), row(lam))


DENSE_ROWS = 512


def _norm_matmul_kernel(x_ref, g_ref, w_ref, o_ref, xn_s):
    @pl.when(pl.program_id(1) == 0)
    def _():
        x = x_ref[...]
        y = x * lax.rsqrt(jnp.mean(x * x, axis=-1, keepdims=True) + NORM_EPS) * g_ref[...]
        xn_s[...] = y.astype(jnp.bfloat16)
    o_ref[...] = jnp.dot(xn_s[...], w_ref[...], preferred_element_type=jnp.float32)


def norm_matmul(x, g, w, tn):
    t, d = x.shape
    n = w.shape[1]
    tm = DENSE_ROWS
    return pl.pallas_call(
        _norm_matmul_kernel,
        grid=(t // tm, n // tn),
        in_specs=[pl.BlockSpec((tm, d), lambda i, j: (i, 0)),
                  pl.BlockSpec((1, d), lambda i, j: (0, 0)),
                  pl.BlockSpec((d, tn), lambda i, j: (0, j))],
        out_specs=pl.BlockSpec((tm, tn), lambda i, j: (i, j)),
        out_shape=jax.ShapeDtypeStruct((t, n), jnp.float32),
        scratch_shapes=[pltpu.VMEM((tm, d), jnp.bfloat16)],
        compiler_params=pltpu.CompilerParams(dimension_semantics=("arbitrary", "arbitrary"),
                                             vmem_limit_bytes=VMEM_LIMIT_BYTES),
        name="norm_matmul",
    )(x, g.reshape(1, d), w)


def _merge_out_kernel(x_ref, mg_ref, of_ref, ol_ref, od_ref, on_ref, wf_ref, wl_ref, wd_ref, wn_ref,
                      wo_ref, g2_ref, y_ref, h2_ref, *, d):
    bf16 = jnp.bfloat16
    merged = None
    for b, (o_ref, w_ref) in enumerate(((of_ref, wf_ref), (ol_ref, wl_ref), (od_ref, wd_ref),
                                        (on_ref, wn_ref))):
        y = jnp.dot(o_ref[...].astype(bf16), w_ref[...], preferred_element_type=jnp.float32)
        term = jax.nn.sigmoid(mg_ref[:, pl.ds(b * d, d)]) * y
        merged = term if merged is None else merged + term
    y = x_ref[...] + jnp.dot(merged.astype(bf16), wo_ref[...], preferred_element_type=jnp.float32)
    y_ref[...] = y
    h2_ref[...] = y * lax.rsqrt(jnp.mean(y * y, axis=-1, keepdims=True) + NORM_EPS) * g2_ref[...]


def merge_out(x, mg, o_fox, o_lru, o_dil, o_nsa, w_fox, w_lru, w_dil, w_nsa, w_out, g2):
    t, d = x.shape
    tm = DENSE_ROWS // 2
    bf16 = jnp.bfloat16
    tok = lambda a: pl.BlockSpec((tm, a.shape[1]), lambda i: (i, 0))
    full = lambda a: pl.BlockSpec(a.shape, lambda i: (0, 0))
    ws = [w.astype(bf16) for w in (w_fox, w_lru, w_dil, w_nsa, w_out)]
    g2r = g2.reshape(1, d)
    acts = (x, mg, o_fox, o_lru, o_dil, o_nsa)
    return pl.pallas_call(
        functools.partial(_merge_out_kernel, d=d),
        grid=(t // tm,),
        in_specs=[tok(a) for a in acts] + [full(w) for w in ws] + [full(g2r)],
        out_specs=[pl.BlockSpec((tm, d), lambda i: (i, 0)), pl.BlockSpec((tm, d), lambda i: (i, 0))],
        out_shape=[jax.ShapeDtypeStruct((t, d), jnp.float32), jax.ShapeDtypeStruct((t, d), jnp.float32)],
        compiler_params=pltpu.CompilerParams(dimension_semantics=("arbitrary",),
                                             vmem_limit_bytes=VMEM_LIMIT_BYTES),
        name="merge_out",
    )(*acts, *ws, g2r)


def rope(x, pos):
    half = x.shape[-1] // 2
    freqs = ROPE_THETA ** (-jnp.arange(half, dtype=jnp.float32) / half)
    ang = pos.astype(jnp.float32)[:, None] * freqs[None, :]
    cos = jnp.cos(ang)[None, :, None, :]
    sin = jnp.sin(ang)[None, :, None, :]
    xf = x.astype(jnp.float32)
    x1, x2 = xf[..., :half], xf[..., half:]
    return jnp.concatenate([x1 * cos - x2 * sin, x2 * cos + x1 * sin], axis=-1).astype(x.dtype)


def nsa_branch(nq, nkc, nvc, nks, nvs, nkw, nvw, ng, pe_k, pe_v, w1_k, w2_k, w1_v, w2_v):
    B, S, _ = nq.shape
    dh = HEAD_DIM
    bf16 = jnp.bfloat16
    pos = jnp.arange(S)
    q4 = nq.reshape(B, S, NSA_HEADS, dh)
    qn = jnp.transpose(q4, (0, 2, 1, 3))
    qr = jnp.transpose(rope(q4, pos), (0, 2, 1, 3))
    ks = rope(nks.reshape(B, S, 1, dh), pos).reshape(B, S, dh).astype(bf16)
    kw = rope(nkw.reshape(B, S, 1, dh), pos).reshape(B, S, dh).astype(bf16)

    n_cmp = (S - NSA_CMP_BLOCK) // NSA_CMP_STRIDE + 1
    n_cmp_pad = S // NSA_CMP_STRIDE
    n_slc = S // NSA_SLC_BLOCK

    kc = nsa_compress(nkc, pe_k, w1_k, w2_k)
    vc = nsa_compress(nvc, pe_v, w1_v, w2_v)
    cmp_start = jnp.arange(n_cmp_pad) * NSA_CMP_STRIDE
    slc_start = jnp.arange(n_slc) * NSA_SLC_BLOCK
    overlap = ((cmp_start[:, None] < slc_start[None, :] + NSA_SLC_BLOCK)
               & (cmp_start[:, None] + NSA_CMP_BLOCK > slc_start[None, :])
               & (jnp.arange(n_cmp_pad)[:, None] < n_cmp)).astype(bf16)
    out = nsa_attention_pallas(qn, qr, kc, vc, overlap, ks, nvs.astype(bf16), kw,
                               nvw.astype(bf16), ng)
    return jnp.transpose(out, (0, 2, 1, 3)).reshape(B, S, NSA_W)


def peer_ffn(x_res, h2, w_q, sub_keys, u, v):
    expert, gt = peer_route(h2, w_q, sub_keys)
    uv = jnp.concatenate([u, v], axis=1)
    return peer_mix(h2, x_res, expert, gt, uv)


def kernel(x, norm1_g, w_in, fox_b_f, lru_conv_w, lru_conv_b, lru_w_a, lru_b_a, lru_w_x, lru_b_x, lru_lambda, nsa_pe_k, nsa_pe_v, nsa_w1_k, nsa_w2_k, nsa_w1_v, nsa_w2_v, w_br_fox, w_br_lru, w_br_dil, w_br_nsa, w_out, norm2_g, peer_w_q, peer_sub_keys, peer_u, peer_v, final_g):
    B, S, D = x.shape
    T = B * S
    pos = jnp.arange(S)
    n_mix = sum(IN_SPLITS[:-1])
    n_mix_pad = -(-n_mix // MXU_COLS) * MXU_COLS
    split_points = np.cumsum(IN_SPLITS[:-1]).tolist()
    bf16 = jnp.bfloat16

    def heads(t, n):
        return t.reshape(B, S, n, HEAD_DIM)

    x = x.reshape(T, D)
    for l in range(DEPTH):
        w_mix = jnp.pad(w_in[l][:, :n_mix], ((0, 0), (0, n_mix_pad - n_mix))).astype(bf16)
        proj = norm_matmul(x, norm1_g[l], w_mix, n_mix_pad).reshape(B, S, n_mix_pad)
        mg = norm_matmul(x, norm1_g[l], w_in[l][:, n_mix:].astype(bf16), N_BRANCHES * D)
        (fq, fk, fv, ff, lx, lg, dq, dk, dv, nq, nkc, nvc, nks, nvs, nkw, nvw, ng, _) = \
            jnp.split(proj, split_points, axis=-1)

        cum = jnp.cumsum(jax.nn.log_sigmoid(ff + fox_b_f[l]), axis=1)
        o_fox = fox_attention_pallas(fq, fk.astype(jnp.bfloat16), fv.astype(jnp.bfloat16), cum)
        o_lru = rg_lru_pallas(lx, lg, lru_conv_w[l], lru_conv_b[l], lru_w_a[l], lru_b_a[l],
                              lru_w_x[l], lru_b_x[l], lru_lambda[l])
        o_dil = dilated_attention_pallas(
            rope(heads(dq, DIL_HEADS), pos).reshape(B, S, DIL_W),
            rope(heads(dk, DIL_HEADS), pos).reshape(B, S, DIL_W).astype(jnp.bfloat16),
            dv.astype(jnp.bfloat16))
        o_nsa = nsa_branch(nq, nkc, nvc, nks, nvs, nkw, nvw, ng,
                           nsa_pe_k[l], nsa_pe_v[l], nsa_w1_k[l], nsa_w2_k[l],
                           nsa_w1_v[l], nsa_w2_v[l])

        x, h2 = merge_out(x, mg, o_fox.reshape(T, FOX_W), o_lru.reshape(T, LRU_WIDTH),
                          o_dil.reshape(T, DIL_OUT_W), o_nsa.reshape(T, NSA_W),
                          w_br_fox[l], w_br_lru[l], w_br_dil[l], w_br_nsa[l], w_out[l], norm2_g[l])
        x = peer_ffn(x, h2, peer_w_q[l], peer_sub_keys[l], peer_u[l], peer_v[l])

    return rms_norm_pallas(x, final_g).reshape(B, S, D)
```
